```python
import math
import jax, jax.numpy as jnp
from jax import lax
import numpy as np


D_MODEL = 1024
BATCH = 8
SEQ = 4096
DEPTH = 2

N_MIXERS = 2
N_A_LAYERS = (DEPTH + 1) // 2
N_B_LAYERS = DEPTH // 2
Q_BLOCK = 128

SB_HEADS = 16
SB_HEAD_DIM = D_MODEL // SB_HEADS

DSA_HEADS = 16
DSA_LATENT = 128
DSA_V_DIM = D_MODEL // DSA_HEADS
IDX_HEADS = 8
IDX_DIM = 64
TOPK_MAX = 256
DSA_IN = DSA_HEADS * DSA_LATENT + DSA_LATENT + IDX_HEADS * IDX_DIM + IDX_DIM + IDX_HEADS

NUM_BUCKETS = 32
MAX_DISTANCE = 128

FFN_DIM = 2816
CONV_W = 3

RMS_EPS = 1e-6
NEG = -1e30

kernel_name = 'hybrid_stickbreak_dsa_convffn_adaln'


def rms_norm(x, g):
    xf = x.astype(jnp.float32)
    y = xf * lax.rsqrt(jnp.mean(xf * xf, axis=-1, keepdims=True) + RMS_EPS)
    return (y * g.astype(jnp.float32)).astype(x.dtype)


def _blocks(a):
    B, S = a.shape[0], a.shape[1]
    a = a.reshape((B, S // Q_BLOCK, Q_BLOCK) + a.shape[2:])
    return jnp.moveaxis(a, 1, 0)


def _unblocks(a):
    a = jnp.moveaxis(a, 0, 1)
    return a.reshape((a.shape[0], a.shape[1] * a.shape[2]) + a.shape[3:])


def t5_bucket(dist):
    n = jnp.maximum(dist, 0)
    max_exact = NUM_BUCKETS // 2
    nf = jnp.maximum(n, 1).astype(jnp.float32)
    large = max_exact + (jnp.log(nf / max_exact) / math.log(MAX_DISTANCE / max_exact)
                         * (NUM_BUCKETS - max_exact)).astype(jnp.int32)
    large = jnp.minimum(large, NUM_BUCKETS - 1)
    return jnp.where(n < max_exact, n, large)


def stick_breaking_attention(h, w_in, w_out):
    B, S, _ = h.shape
    qkv = h @ w_in
    q, k, v = jnp.split(qkv, 3, axis=-1)
    q = q.reshape(B, S, SB_HEADS, SB_HEAD_DIM).astype(jnp.float32) * (SB_HEAD_DIM ** -0.5)
    k = k.reshape(B, S, SB_HEADS, SB_HEAD_DIM).astype(jnp.float32)
    v = v.reshape(B, S, SB_HEADS, SB_HEAD_DIM).astype(jnp.float32)
    key_pos = jnp.arange(S)

    def block(args):
        qb, t0 = args
        t = t0 + jnp.arange(Q_BLOCK)
        z = jnp.einsum('bqhd,bshd->bhqs', qb, k)
        strict = key_pos[None, :] < t[:, None]
        log_keep = jnp.where(strict, jax.nn.log_sigmoid(-z), 0.0)
        suffix = lax.cumsum(log_keep, axis=3, reverse=True) - log_keep
        a = jnp.where(strict, jnp.exp(jax.nn.log_sigmoid(z) + suffix), 0.0)
        return jnp.einsum('bhqs,bshd->bqhd', a, v)

    starts = jnp.arange(S // Q_BLOCK) * Q_BLOCK
    o = _unblocks(lax.map(block, (_blocks(q), starts)))
    return o.reshape(B, S, SB_HEADS * SB_HEAD_DIM).astype(h.dtype) @ w_out


def dsa_attention(h, w_in, q_gain, k_gain, w_uv, w_out, rel_bias):
    B, S, _ = h.shape
    topk = min(TOPK_MAX, S // 4)
    proj = h @ w_in
    o1 = DSA_HEADS * DSA_LATENT
    o2 = o1 + DSA_LATENT
    o3 = o2 + IDX_HEADS * IDX_DIM
    o4 = o3 + IDX_DIM
    q, lat, qi, ki, wi = jnp.split(proj, [o1, o2, o3, o4], axis=-1)
    q = rms_norm(q.reshape(B, S, DSA_HEADS, DSA_LATENT), q_gain).astype(jnp.float32)
    k = rms_norm(lat, k_gain).astype(jnp.float32)
    vals = lat.astype(jnp.float32)
    qi = qi.reshape(B, S, IDX_HEADS, IDX_DIM).astype(jnp.float32)
    ki = ki.astype(jnp.float32)
    wi = wi.astype(jnp.float32) * (IDX_HEADS ** -0.5)
    bidx = jnp.arange(B)[:, None, None]
    key_pos = jnp.arange(S)
    scale = DSA_LATENT ** -0.5

    def block(args):
        qb, qib, wib, t0 = args
        t = t0 + jnp.arange(Q_BLOCK)
        isc = jax.nn.relu(jnp.einsum('bqhd,bsd->bqhs', qib, ki))
        isc = jnp.einsum('bqhs,bqh->bqs', isc, wib)
        causal = key_pos[None, :] <= t[:, None]
        isc = jnp.where(causal[None], isc, NEG)
        _, sel = lax.top_k(isc, topk)
        valid = sel <= t[None, :, None]
        kg = k[bidx, sel]
        vg = vals[bidx, sel]
        logits = jnp.einsum('bqhd,bqkd->bqhk', qb, kg) * scale
        bias = rel_bias[t5_bucket(t[None, :, None] - sel)]
        logits = logits + jnp.swapaxes(bias, -1, -2).astype(jnp.float32)
        logits = jnp.where(valid[:, :, None, :], logits, NEG)
        p = jax.nn.softmax(logits, axis=-1)
        return jnp.einsum('bqhk,bqkd->bqhd', p, vg)

    starts = jnp.arange(S // Q_BLOCK) * Q_BLOCK
    o_lat = _unblocks(lax.map(block, (_blocks(q), _blocks(qi), _blocks(wi), starts)))
    o = jnp.einsum('bshl,hlv->bshv', o_lat, w_uv.astype(jnp.float32))
    return o.reshape(B, S, DSA_HEADS * DSA_V_DIM).astype(h.dtype) @ w_out


def conv_ffn(h, w_up, conv_w, conv_b, w_down):
    S = h.shape[1]
    u = h @ w_up
    up = jnp.pad(u, ((0, 0), (CONV_W - 1, 0), (0, 0)))
    y = conv_b
    for j in range(CONV_W):
        y = y + up[:, j:j + S] * conv_w[j]
    gate, val = jnp.split(y, 2, axis=-1)
    return (jax.nn.silu(gate) * val) @ w_down


def setup_inputs(seed: int = 0) -> dict:
    key = jax.random.key(seed)
    ks = jax.random.split(key, 20)
    f32 = jnp.float32
    D = D_MODEL

    def nrm(k, shape, s):
        return jax.random.normal(k, shape, f32) * s

    return {
        'x': nrm(ks[0], (BATCH, SEQ, D), 1.0),
        'c': nrm(ks[1], (BATCH, D), 1.0),
        'ada_w': nrm(ks[2], (DEPTH, D, 6 * D), 0.5 * D ** -0.5),
        'ada_b': nrm(ks[3], (DEPTH, 6 * D), 0.01),
        'norm_mix': 1.0 + nrm(ks[4], (DEPTH, D), 0.05),
        'norm_ffn': 1.0 + nrm(ks[5], (DEPTH, D), 0.05),
        'sb_w_in': nrm(ks[6], (N_A_LAYERS, D, 3 * SB_HEADS * SB_HEAD_DIM), D ** -0.5),
        'sb_w_out': nrm(ks[7], (N_A_LAYERS, SB_HEADS * SB_HEAD_DIM, D), (SB_HEADS * SB_HEAD_DIM) ** -0.5),
        'dsa_w_in': nrm(ks[8], (N_B_LAYERS, D, DSA_IN), D ** -0.5),
        'dsa_q_norm': 1.0 + nrm(ks[9], (N_B_LAYERS, DSA_LATENT), 0.05),
        'dsa_k_norm': 1.0 + nrm(ks[10], (N_B_LAYERS, DSA_LATENT), 0.05),
        'dsa_w_uv': nrm(ks[11], (N_B_LAYERS, DSA_HEADS, DSA_LATENT, DSA_V_DIM), DSA_LATENT ** -0.5),
        'dsa_w_out': nrm(ks[12], (N_B_LAYERS, DSA_HEADS * DSA_V_DIM, D), (DSA_HEADS * DSA_V_DIM) ** -0.5),
        'rel_bias': nrm(ks[13], (NUM_BUCKETS, DSA_HEADS), 0.5),
        'ffn_w_up': nrm(ks[14], (DEPTH, D, 2 * FFN_DIM), D ** -0.5),
        'ffn_conv_w': nrm(ks[15], (DEPTH, CONV_W, 2 * FFN_DIM), CONV_W ** -0.5),
        'ffn_conv_b': nrm(ks[16], (DEPTH, 2 * FFN_DIM), 0.01),
        'ffn_w_down': nrm(ks[17], (DEPTH, FFN_DIM, D), FFN_DIM ** -0.5),
    }


def reference(x, c, ada_w, ada_b, norm_mix, norm_ffn, sb_w_in, sb_w_out, dsa_w_in,
              dsa_q_norm, dsa_k_norm, dsa_w_uv, dsa_w_out, rel_bias, ffn_w_up,
              ffn_conv_w, ffn_conv_b, ffn_w_down):
    cond = jax.nn.silu(c)
    for i in range(DEPTH):
        mod = cond @ ada_w[i] + ada_b[i]
        sh1, sc1, g1, sh2, sc2, g2 = [m[:, None, :] for m in jnp.split(mod, 6, axis=-1)]
        h = rms_norm(x, norm_mix[i]) * (1.0 + sc1) + sh1
        j = i // N_MIXERS
        if i % N_MIXERS == 0:
            mix = stick_breaking_attention(h, sb_w_in[j], sb_w_out[j])
        else:
            mix = dsa_attention(h, dsa_w_in[j], dsa_q_norm[j], dsa_k_norm[j],
                                dsa_w_uv[j], dsa_w_out[j], rel_bias)
        x = x + g1 * mix
        h = rms_norm(x, norm_ffn[i]) * (1.0 + sc2) + sh2
        x = x + g2 * conv_ffn(h, ffn_w_up[i], ffn_conv_w[i], ffn_conv_b[i], ffn_w_down[i])
    return x
```

```python
import functools
import math

import numpy as np
import jax
import jax.numpy as jnp
from jax import lax
from jax.experimental import pallas as pl
from jax.experimental.pallas import tpu as pltpu

F32 = jnp.float32
BF16 = jnp.bfloat16
I32 = jnp.int32

LANES = 128
SUBLANES = 8
VMEM_LIMIT = 56 * 1024 * 1024

RMS_EPS = 1e-6
NEG = -1e30
INT_MIN = -(2 ** 31)

SB_HEADS = 16
SB_HEAD_DIM = 64
DSA_HEADS = 16
DSA_LATENT = 128
DSA_V_DIM = 64
IDX_HEADS = 8
IDX_DIM = 64
TOPK_MAX = 256
NUM_BUCKETS = 32
MAX_DISTANCE = 128
CONV_W = 3


def _cparams(*sem):
    return pltpu.CompilerParams(dimension_semantics=sem, vmem_limit_bytes=VMEM_LIMIT)


def _dot(a, b):
    return jnp.dot(a, b, preferred_element_type=F32)


def _dot_nt(a, b):
    return lax.dot_general(a, b, (((1,), (1,)), ((), ())), preferred_element_type=F32)


def _split_bf16(v):
    hi = v.astype(BF16)
    lo = (v - hi.astype(F32)).astype(BF16)
    return hi, lo


def _mod_kernel(c_ref, w_ref, b_ref, o_ref):
    c = c_ref[...]
    cond = c * (1.0 / (1.0 + jnp.exp(-c)))
    ch, cl = _split_bf16(cond)
    wh, wl = _split_bf16(w_ref[0])
    o_ref[0] = _dot(ch, wh) + _dot(ch, wl) + _dot(cl, wh) + b_ref[0]


def _modulation(c, ada_w, ada_b):
    depth, d, n = ada_w.shape
    b = c.shape[0]
    tn = 1024
    return pl.pallas_call(
        _mod_kernel,
        grid=(depth, n // tn),
        in_specs=[
            pl.BlockSpec((b, d), lambda i, j: (0, 0)),
            pl.BlockSpec((1, d, tn), lambda i, j: (i, 0, j)),
            pl.BlockSpec((1, 1, tn), lambda i, j: (i, 0, j)),
        ],
        out_specs=pl.BlockSpec((1, b, tn), lambda i, j: (i, 0, j)),
        out_shape=jax.ShapeDtypeStruct((depth, b, n), F32),
        compiler_params=_cparams("arbitrary", "arbitrary"),
        name="mod",
    )(c, ada_w, ada_b.reshape(depth, 1, n))


def _modulated_norm(x, g, sc, sh):
    ms = jnp.mean(x * x, axis=-1, keepdims=True)
    return (x * lax.rsqrt(ms + RMS_EPS) * g) * (1.0 + sc) + sh


def _norm_proj_kernel(x_ref, g_ref, sc_ref, sh_ref, w_ref, *o_refs, splits, chunk):
    hb = _modulated_norm(x_ref[0], g_ref[...], sc_ref[0], sh_ref[0]).astype(BF16)
    off = 0
    for o_ref, n in zip(o_refs, splits):
        for c0 in range(0, n, chunk):
            c1 = min(n, c0 + chunk)
            o_ref[0, :, c0:c1] = _dot(hb, w_ref[:, off + c0:off + c1]).astype(o_ref.dtype)
        off += n


def _norm_proj(x, g, sc, sh, w, splits, tm=512):
    b, s, d = x.shape
    n = w.shape[1]
    assert sum(splits) == n
    kern = functools.partial(_norm_proj_kernel, splits=tuple(splits), chunk=512)
    return pl.pallas_call(
        kern,
        grid=(b, s // tm),
        in_specs=[
            pl.BlockSpec((1, tm, d), lambda i, j: (i, j, 0)),
            pl.BlockSpec((1, d), lambda i, j: (0, 0)),
            pl.BlockSpec((1, 1, d), lambda i, j: (i, 0, 0)),
            pl.BlockSpec((1, 1, d), lambda i, j: (i, 0, 0)),
            pl.BlockSpec((d, n), lambda i, j: (0, 0)),
        ],
        out_specs=[pl.BlockSpec((1, tm, m), lambda i, j: (i, j, 0)) for m in splits],
        out_shape=[jax.ShapeDtypeStruct((b, s, m), BF16) for m in splits],
        compiler_params=_cparams("arbitrary", "arbitrary"),
        name="norm_proj",
    )(x, g.reshape(1, d), sc.reshape(b, 1, d), sh.reshape(b, 1, d), w)


def _out_res_kernel(o_ref, w_ref, x_ref, gate_ref, y_ref):
    y_ref[0] = x_ref[0] + gate_ref[0] * _dot(o_ref[0], w_ref[...])


def _out_res(o, w, x, gate, tm=512):
    b, s, d = x.shape
    k = o.shape[-1]
    return pl.pallas_call(
        _out_res_kernel,
        grid=(b, s // tm),
        in_specs=[
            pl.BlockSpec((1, tm, k), lambda i, j: (i, j, 0)),
            pl.BlockSpec((k, d), lambda i, j: (0, 0)),
            pl.BlockSpec((1, tm, d), lambda i, j: (i, j, 0)),
            pl.BlockSpec((1, 1, d), lambda i, j: (i, 0, 0)),
        ],
        out_specs=pl.BlockSpec((1, tm, d), lambda i, j: (i, j, 0)),
        out_shape=jax.ShapeDtypeStruct((b, s, d), F32),
        compiler_params=_cparams("arbitrary", "arbitrary"),
        name="out_res",
    )(o, w, x, gate.reshape(b, 1, d))


def _ffn_kernel(x_ref, g_ref, sc_ref, sh_ref, gate_ref, wug_ref, wuv_ref, cwg_ref, cwv_ref,
                cbg_ref, cbv_ref, wd_ref, y_ref, hb_scr, acc_scr, cg_scr, cv_scr, *, nf):
    si = pl.program_id(1)
    fi = pl.program_id(2)
    tm = hb_scr.shape[0]

    @pl.when(fi == 0)
    def _():
        hb_scr[...] = _modulated_norm(x_ref[0], g_ref[...], sc_ref[0], sh_ref[0]).astype(BF16)
        acc_scr[...] = jnp.zeros_like(acc_scr)

    @pl.when(si == 0)
    def _():
        cg_scr[fi] = jnp.zeros(cg_scr.shape[1:], F32)
        cv_scr[fi] = jnp.zeros(cv_scr.shape[1:], F32)

    hb = hb_scr[...]
    row = lax.broadcasted_iota(I32, (tm, 1), 0)

    def conv_branch(wu_ref, cw_ref, cb_ref, carry_scr):
        u = _dot(hb, wu_ref[...])
        prev = carry_scr[fi]
        carry_scr[fi] = u[tm - SUBLANES:, :]
        p1 = prev[SUBLANES - 1:SUBLANES, :]
        p2 = prev[SUBLANES - 2:SUBLANES - 1, :]
        u1 = jnp.where(row == 0, p1, pltpu.roll(u, 1, 0))
        u2 = jnp.where(row == 0, p2, jnp.where(row == 1, p1, pltpu.roll(u, 2, 0)))
        cw = cw_ref[...]
        return cb_ref[...] + u2 * cw[0:1, :] + u1 * cw[1:2, :] + u * cw[2:3, :]

    yg = conv_branch(wug_ref, cwg_ref, cbg_ref, cg_scr)
    yv = conv_branch(wuv_ref, cwv_ref, cbv_ref, cv_scr)
    act = yg * (1.0 / (1.0 + jnp.exp(-yg))) * yv
    acc_scr[...] += _dot(act.astype(BF16), wd_ref[...])

    @pl.when(fi == nf - 1)
    def _():
        y_ref[0] = x_ref[0] + gate_ref[0] * acc_scr[...]


def _ffn(x, g, sc, sh, gate, w_up, conv_w, conv_b, w_down, tm=1024, fc=256):
    b, s, d = x.shape
    f = w_down.shape[0]
    nf = f // fc
    kern = functools.partial(_ffn_kernel, nf=nf)
    vec = lambda a: a.reshape(b, 1, d)
    return pl.pallas_call(
        kern,
        grid=(b, s // tm, nf),
        in_specs=[
            pl.BlockSpec((1, tm, d), lambda i, j, k: (i, j, 0)),
            pl.BlockSpec((1, d), lambda i, j, k: (0, 0)),
            pl.BlockSpec((1, 1, d), lambda i, j, k: (i, 0, 0)),
            pl.BlockSpec((1, 1, d), lambda i, j, k: (i, 0, 0)),
            pl.BlockSpec((1, 1, d), lambda i, j, k: (i, 0, 0)),
            pl.BlockSpec((d, fc), lambda i, j, k: (0, k)),
            pl.BlockSpec((d, fc), lambda i, j, k: (0, nf + k)),
            pl.BlockSpec((CONV_W, fc), lambda i, j, k: (0, k)),
            pl.BlockSpec((CONV_W, fc), lambda i, j, k: (0, nf + k)),
            pl.BlockSpec((1, fc), lambda i, j, k: (0, k)),
            pl.BlockSpec((1, fc), lambda i, j, k: (0, nf + k)),
            pl.BlockSpec((fc, d), lambda i, j, k: (k, 0)),
        ],
        out_specs=pl.BlockSpec((1, tm, d), lambda i, j, k: (i, j, 0)),
        out_shape=jax.ShapeDtypeStruct((b, s, d), F32),
        scratch_shapes=[
            pltpu.VMEM((tm, d), BF16),
            pltpu.VMEM((tm, d), F32),
            pltpu.VMEM((nf, SUBLANES, fc), F32),
            pltpu.VMEM((nf, SUBLANES, fc), F32),
        ],
        compiler_params=_cparams("arbitrary", "arbitrary", "arbitrary"),
        name="ffn",
    )(x, g.reshape(1, d), vec(sc), vec(sh), vec(gate), w_up, w_up, conv_w, conv_w,
      conv_b.reshape(1, 2 * f), conv_b.reshape(1, 2 * f), w_down)


def _sb_kernel(q_ref, k_ref, v_ref, tri_ref, o_ref, *, tq):
    qi = pl.program_id(2)
    lane = lax.broadcasted_iota(I32, (1, LANES), 1)
    halves = (lane < SB_HEAD_DIM, lane >= SB_HEAD_DIM)
    q2 = q_ref[0] * (SB_HEAD_DIM ** -0.5)
    zero = jnp.zeros_like(q2)
    qs = tuple(jnp.where(hm, q2, zero) for hm in halves)
    tri = tri_ref[...]
    row = lax.broadcasted_iota(I32, (tq, tq), 0)
    col = lax.broadcasted_iota(I32, (tq, tq), 1)
    strict = col < row

    def block(jb, carry, masked):
        acc, r_a, r_b = carry
        start = pl.multiple_of(jb * tq, tq)
        kb = k_ref[0, pl.ds(start, tq), :]
        vb = v_ref[0, pl.ds(start, tq), :]
        r_new = []
        for hm, qh, r in zip(halves, qs, (r_a, r_b)):
            z = _dot_nt(qh, kb)
            lk = -(jnp.maximum(z, 0.0) + jnp.log1p(jnp.exp(-jnp.abs(z))))
            if masked:
                lk = jnp.where(strict, lk, 0.0)
            hi, lo = _split_bf16(lk)
            incl = _dot(hi, tri) + _dot(lo, tri)
            a = jnp.exp(z + incl + r)
            if masked:
                a = jnp.where(strict, a, 0.0)
            vh = jnp.where(hm, vb, jnp.zeros_like(vb))
            acc = acc + _dot(a.astype(BF16), vh)
            r_new.append(r + incl[:, 0:1])
        return acc, r_new[0], r_new[1]

    init = (jnp.zeros((tq, LANES), F32), jnp.zeros((tq, 1), F32), jnp.zeros((tq, 1), F32))
    carry = block(qi, init, True)
    carry = lax.fori_loop(0, qi, lambda i, c: block(qi - 1 - i, c, False), carry)
    o_ref[0] = carry[0].astype(o_ref.dtype)


def _sb_attention(qkv, tq=256):
    b, s, n3 = qkv.shape
    n = n3 // 3
    npair = n // LANES
    tri = (jnp.arange(tq)[:, None] >= jnp.arange(tq)[None, :]).astype(BF16)
    kern = functools.partial(_sb_kernel, tq=tq)
    return pl.pallas_call(
        kern,
        grid=(b, npair, s // tq),
        in_specs=[
            pl.BlockSpec((1, tq, LANES), lambda i, p, j: (i, j, p)),
            pl.BlockSpec((1, s, LANES), lambda i, p, j: (i, 0, npair + p)),
            pl.BlockSpec((1, s, LANES), lambda i, p, j: (i, 0, 2 * npair + p)),
            pl.BlockSpec((tq, tq), lambda i, p, j: (0, 0)),
        ],
        out_specs=pl.BlockSpec((1, tq, LANES), lambda i, p, j: (i, j, p)),
        out_shape=jax.ShapeDtypeStruct((b, s, n), BF16),
        compiler_params=_cparams("arbitrary", "arbitrary", "arbitrary"),
        name="sb_attn",
    )(qkv, qkv, qkv, tri)


def _t5_bucket_thresholds():
    max_exact = NUM_BUCKETS // 2
    n = np.arange(0, 4 * MAX_DISTANCE, dtype=np.int32)
    nf = np.maximum(n, 1).astype(np.float32)
    large = max_exact + (np.log(nf / np.float32(max_exact)) / np.float32(math.log(MAX_DISTANCE / max_exact))
                         * np.float32(NUM_BUCKETS - max_exact)).astype(np.int32)
    large = np.minimum(large, NUM_BUCKETS - 1)
    bucket = np.where(n < max_exact, n, large)
    assert np.all(np.diff(bucket) >= 0) and bucket[-1] == NUM_BUCKETS - 1
    return [int(np.argmax(bucket >= bb)) for bb in range(max_exact + 1, NUM_BUCKETS)]


def _sortable_key(v):
    v = jnp.where(v == 0.0, 0.0, v)
    bits = lax.bitcast_convert_type(v, I32)
    return bits ^ ((bits >> 31) & 0x7FFFFFFF)


def _dsa_kernel(rb_ref, q_ref, qidx_ref, lat_ref, kidx_ref, qg_ref, kg_ref, wuv_ref, tri_ref,
                o_ref, kn_scr, key_scr, qn_scr, acc_scr, m_scr, l_scr, eqc_scr, bias_scr,
                *, tq, topk, thresholds):
    bi = pl.program_id(0)
    qi = pl.program_id(1)
    nlane = tq // LANES
    row = lax.broadcasted_iota(I32, (tq, tq), 0)
    col = lax.broadcasted_iota(I32, (tq, tq), 1)
    row1 = lax.broadcasted_iota(I32, (tq, 1), 0)

    @pl.when((bi == 0) & (qi == 0))
    def _():
        max_exact = NUM_BUCKETS // 2
        for near in range(2):
            dist = row - col + near * tq
            large = jnp.full((tq, tq), max_exact, I32)
            for th in thresholds:
                large = large + jnp.where(dist >= th, 1, 0)
            bucket = jnp.where(dist < max_exact, jnp.maximum(dist, 0), large)

            def head_tile(h, _, near=near, bucket=bucket):
                far = rb_ref[NUM_BUCKETS - 1, h]
                tile = jnp.zeros((tq, tq), F32)
                for bb in range(NUM_BUCKETS - 1):
                    tile = jnp.where(bucket == bb, rb_ref[bb, h] - far, tile)
                bias_scr[near, h] = tile
                return 0

            lax.fori_loop(0, DSA_HEADS, head_tile, 0)

    @pl.when(qi == 0)
    def _():
        lat = lat_ref[0].astype(F32)
        ms = jnp.mean(lat * lat, axis=-1, keepdims=True)
        kn_scr[...] = (lat * lax.rsqrt(ms + RMS_EPS) * kg_ref[...]).astype(BF16)

    scale = DSA_LATENT ** -0.5
    for h in range(DSA_HEADS):
        qh = q_ref[0, :, h * DSA_LATENT:(h + 1) * DSA_LATENT].astype(F32)
        ms = jnp.mean(qh * qh, axis=-1, keepdims=True)
        qn_scr[h] = (qh * lax.rsqrt(ms + RMS_EPS) * (qg_ref[...] * scale)).astype(BF16)

    lane = lax.broadcasted_iota(I32, (1, LANES), 1)
    halves = (lane < IDX_DIM, lane >= IDX_DIM)
    n_qi = IDX_HEADS * IDX_DIM
    wi = qidx_ref[0, :, n_qi + LANES:n_qi + 2 * LANES].astype(F32) * (IDX_HEADS ** -0.5)
    qms = []
    for hh in range(IDX_HEADS):
        blk = qidx_ref[0, :, (hh // 2) * LANES:(hh // 2 + 1) * LANES]
        qms.append(jnp.where(halves[hh % 2], blk, jnp.zeros_like(blk)))

    def score_block(jb, _):
        start = pl.multiple_of(jb * tq, tq)
        kb = kidx_ref[0, pl.ds(start, tq), :]
        isc = jnp.zeros((tq, tq), F32)
        for hh in range(IDX_HEADS):
            isc = isc + jnp.maximum(_dot_nt(qms[hh], kb), 0.0) * wi[:, hh:hh + 1]
        key = _sortable_key(isc)
        key_scr[jb] = jnp.where(col + jb * tq <= row + qi * tq, key, INT_MIN)
        return 0

    lax.fori_loop(0, qi + 1, score_block, 0)

    def count(pred):
        def body(jb, part):
            c = jnp.where(pred(key_scr[jb]), 1.0, 0.0)
            for g in range(nlane):
                part = part + c[:, g * LANES:(g + 1) * LANES]
            return part
        part = lax.fori_loop(0, qi + 1, body, jnp.zeros((tq, LANES), F32))
        return jnp.sum(part, axis=1, keepdims=True)

    kf = float(topk)
    thr = jnp.where(count(lambda k: k >= 0) >= kf, 0, INT_MIN).astype(I32)

    def bisect(i, thr):
        cand = thr | lax.shift_left(jnp.int32(1), jnp.int32(30) - i)
        return jnp.where(count(lambda k: k >= cand) >= kf, cand, thr)

    thr = lax.fori_loop(0, 31, bisect, thr)
    thr = jnp.where(row1 + qi * tq + 1 <= topk, INT_MIN, thr)
    need = kf - count(lambda k: k > thr)

    m_scr[...] = jnp.full_like(m_scr, NEG)
    l_scr[...] = jnp.zeros_like(l_scr)
    acc_scr[...] = jnp.zeros_like(acc_scr)
    eqc_scr[...] = jnp.zeros_like(eqc_scr)
    tri = tri_ref[...]

    def attend_block(jb, near):
        start = pl.multiple_of(jb * tq, tq)
        key = key_scr[jb]
        eq = key == thr
        rank = _dot(jnp.where(eq, 1.0, 0.0).astype(BF16), tri) + eqc_scr[...]
        eqc_scr[...] = rank[:, tq - 1:tq]
        sel = (key > thr) | (eq & (rank <= need))
        sel = sel & (col + jb * tq <= row + qi * tq)
        madd = jnp.where(sel, 0.0, NEG)
        knb = kn_scr[pl.ds(start, tq), :]
        vb = lat_ref[0, pl.ds(start, tq), :]

        def head(h, _):
            lg = _dot_nt(qn_scr[h], knb) + madd
            if near is not None:
                lg = lg + bias_scr[near, h]
            m_old = m_scr[h]
            m_new = jnp.maximum(m_old, jnp.max(lg, axis=-1, keepdims=True))
            alpha = jnp.exp(m_old - m_new)
            p = jnp.exp(lg - m_new)
            l_scr[h] = alpha * l_scr[h] + jnp.sum(p, axis=-1, keepdims=True)
            acc_scr[h] = alpha * acc_scr[h] + _dot(p.astype(BF16), vb)
            m_scr[h] = m_new
            return 0

        lax.fori_loop(0, DSA_HEADS, head, 0)

    def far_block(jb, _):
        attend_block(jb, None)
        return 0

    lax.fori_loop(0, jnp.maximum(qi - 1, 0), far_block, 0)

    @pl.when(qi >= 1)
    def _():
        attend_block(qi - 1, 1)

    attend_block(qi, 0)

    for pr in range(DSA_HEADS // 2):
        out = jnp.zeros((tq, LANES), F32)
        for h in (2 * pr, 2 * pr + 1):
            ol = acc_scr[h] * (1.0 / l_scr[h])
            out = out + _dot(ol.astype(BF16), wuv_ref[h])
        o_ref[0, :, pr * LANES:(pr + 1) * LANES] = out.astype(o_ref.dtype)


def _dsa_attention(qp, idx, lat, q_gain, k_gain, wuv_ext, rel_bias, tq=256):
    b, s, nq = qp.shape
    topk = min(TOPK_MAX, s // 4)
    nkb = s // tq
    n_idx = idx.shape[-1]
    tri = (jnp.arange(tq)[:, None] <= jnp.arange(tq)[None, :]).astype(BF16)
    kern = functools.partial(_dsa_kernel, tq=tq, topk=topk, thresholds=tuple(_t5_bucket_thresholds()))
    return pl.pallas_call(
        kern,
        grid=(b, s // tq),
        in_specs=[
            pl.BlockSpec(memory_space=pltpu.SMEM),
            pl.BlockSpec((1, tq, nq), lambda i, j: (i, j, 0)),
            pl.BlockSpec((1, tq, n_idx), lambda i, j: (i, j, 0)),
            pl.BlockSpec((1, s, LANES), lambda i, j: (i, 0, 0)),
            pl.BlockSpec((1, s, LANES), lambda i, j: (i, 0, IDX_HEADS * IDX_DIM // LANES)),
            pl.BlockSpec((1, DSA_LATENT), lambda i, j: (0, 0)),
            pl.BlockSpec((1, DSA_LATENT), lambda i, j: (0, 0)),
            pl.BlockSpec((DSA_HEADS, DSA_LATENT, LANES), lambda i, j: (0, 0, 0)),
            pl.BlockSpec((tq, tq), lambda i, j: (0, 0)),
        ],
        out_specs=pl.BlockSpec((1, tq, DSA_HEADS * DSA_V_DIM), lambda i, j: (i, j, 0)),
        out_shape=jax.ShapeDtypeStruct((b, s, DSA_HEADS * DSA_V_DIM), BF16),
        scratch_shapes=[
            pltpu.VMEM((s, DSA_LATENT), BF16),
            pltpu.VMEM((nkb, tq, tq), I32),
            pltpu.VMEM((DSA_HEADS, tq, DSA_LATENT), BF16),
            pltpu.VMEM((DSA_HEADS, tq, DSA_LATENT), F32),
            pltpu.VMEM((DSA_HEADS, tq, 1), F32),
            pltpu.VMEM((DSA_HEADS, tq, 1), F32),
            pltpu.VMEM((tq, 1), F32),
            pltpu.VMEM((2, DSA_HEADS, tq, tq), F32),
        ],
        compiler_params=_cparams("arbitrary", "arbitrary"),
        name="dsa_attn",
    )(rel_bias, qp, idx, lat, idx, q_gain.reshape(1, -1), k_gain.reshape(1, -1), wuv_ext, tri)


def _dsa_weights(w_in, w_uv):
    o1 = DSA_HEADS * DSA_LATENT
    o2 = o1 + DSA_LATENT
    o3 = o2 + IDX_HEADS * IDX_DIM
    o4 = o3 + IDX_DIM
    d = w_in.shape[0]
    w_ki = w_in[:, o3:o4]
    w_wi = w_in[:, o4:]
    pad = jnp.zeros((d, LANES - IDX_HEADS), w_in.dtype)
    w = jnp.concatenate([w_in[:, :o3], w_ki, w_ki, w_wi, pad], axis=1).astype(BF16)
    zeros = jnp.zeros_like(w_uv)
    even = jnp.concatenate([w_uv, zeros], axis=-1)
    odd = jnp.concatenate([zeros, w_uv], axis=-1)
    is_even = (jnp.arange(DSA_HEADS) % 2 == 0)[:, None, None]
    wuv_ext = jnp.where(is_even, even, odd).astype(BF16)
    splits = (o1, DSA_LATENT, IDX_HEADS * IDX_DIM + 2 * LANES)
    return w, wuv_ext, splits


def kernel(x, c, ada_w, ada_b, norm_mix, norm_ffn, sb_w_in, sb_w_out, dsa_w_in, dsa_q_norm,
           dsa_k_norm, dsa_w_uv, dsa_w_out, rel_bias, ffn_w_up, ffn_conv_w, ffn_conv_b, ffn_w_down):
    depth = ada_w.shape[0]
    d = x.shape[-1]
    mod = _modulation(c, ada_w, ada_b)
    for i in range(depth):
        sh1, sc1, g1, sh2, sc2, g2 = [mod[i, :, k * d:(k + 1) * d] for k in range(6)]
        j = i // 2
        if i % 2 == 0:
            w_in = sb_w_in[j].astype(BF16)
            (qkv,) = _norm_proj(x, norm_mix[i], sc1, sh1, w_in, (w_in.shape[1],))
            o = _sb_attention(qkv)
            w_out = sb_w_out[j]
        else:
            w_in, wuv_ext, splits = _dsa_weights(dsa_w_in[j], dsa_w_uv[j])
            qp, lat, idx = _norm_proj(x, norm_mix[i], sc1, sh1, w_in, splits)
            o = _dsa_attention(qp, idx, lat, dsa_q_norm[j], dsa_k_norm[j], wuv_ext, rel_bias)
            w_out = dsa_w_out[j]
        x = _out_res(o, w_out.astype(BF16), x, g1)
        x = _ffn(x, norm_ffn[i], sc2, sh2, g2, ffn_w_up[i].astype(BF16), ffn_conv_w[i],
                 ffn_conv_b[i], ffn_w_down[i].astype(BF16))
    return x
```

```python
import functools
import math

import numpy as np
import jax
import jax.numpy as jnp
from jax import lax
from jax.experimental import pallas as pl
from jax.experimental.pallas import tpu as pltpu

F32 = jnp.float32
BF16 = jnp.bfloat16
I32 = jnp.int32

LANES = 128
SUBLANES = 8
VMEM_LIMIT = 56 * 1024 * 1024

RMS_EPS = 1e-6
NEG = -1e30
SB_EXIT = -104.0
INT_MIN = -(2 ** 31)

SB_HEADS = 16
SB_HEAD_DIM = 64
DSA_HEADS = 16
DSA_LATENT = 128
DSA_V_DIM = 64
IDX_HEADS = 8
IDX_DIM = 64
TOPK_MAX = 256
NUM_BUCKETS = 32
MAX_DISTANCE = 128
CONV_W = 3


def _cparams(*sem):
    return pltpu.CompilerParams(dimension_semantics=sem, vmem_limit_bytes=VMEM_LIMIT)


def _dot(a, b):
    return jnp.dot(a, b, preferred_element_type=F32)


def _dot_nt(a, b):
    return lax.dot_general(a, b, (((1,), (1,)), ((), ())), preferred_element_type=F32)


def _split_bf16(v):
    hi = v.astype(BF16)
    lo = (v - hi.astype(F32)).astype(BF16)
    return hi, lo


def _mod_kernel(c_ref, w_ref, b_ref, o_ref):
    c = c_ref[...]
    cond = c * (1.0 / (1.0 + jnp.exp(-c)))
    ch, cl = _split_bf16(cond)
    wh, wl = _split_bf16(w_ref[0])
    o_ref[0] = _dot(ch, wh) + _dot(ch, wl) + _dot(cl, wh) + b_ref[0]


def _modulation(c, ada_w, ada_b):
    depth, d, n = ada_w.shape
    b = c.shape[0]
    tn = 1024
    return pl.pallas_call(
        _mod_kernel,
        grid=(depth, n // tn),
        in_specs=[
            pl.BlockSpec((b, d), lambda i, j: (0, 0)),
            pl.BlockSpec((1, d, tn), lambda i, j: (i, 0, j)),
            pl.BlockSpec((1, 1, tn), lambda i, j: (i, 0, j)),
        ],
        out_specs=pl.BlockSpec((1, b, tn), lambda i, j: (i, 0, j)),
        out_shape=jax.ShapeDtypeStruct((depth, b, n), F32),
        compiler_params=_cparams("arbitrary", "arbitrary"),
        name="mod",
    )(c, ada_w, ada_b.reshape(depth, 1, n))


def _modulated_norm(x, g, sc, sh):
    ms = jnp.mean(x * x, axis=-1, keepdims=True)
    return (x * lax.rsqrt(ms + RMS_EPS) * g) * (1.0 + sc) + sh


def _norm_proj_kernel(x_ref, g_ref, sc_ref, sh_ref, w_ref, *o_refs, splits, chunk):
    hb = _modulated_norm(x_ref[0], g_ref[...], sc_ref[0], sh_ref[0]).astype(BF16)
    off = 0
    for o_ref, n in zip(o_refs, splits):
        for c0 in range(0, n, chunk):
            c1 = min(n, c0 + chunk)
            o_ref[0, :, c0:c1] = _dot(hb, w_ref[:, off + c0:off + c1]).astype(o_ref.dtype)
        off += n


def _norm_proj(x, g, sc, sh, w, splits, tm=512):
    b, s, d = x.shape
    n = w.shape[1]
    assert sum(splits) == n
    kern = functools.partial(_norm_proj_kernel, splits=tuple(splits), chunk=512)
    return pl.pallas_call(
        kern,
        grid=(b, s // tm),
        in_specs=[
            pl.BlockSpec((1, tm, d), lambda i, j: (i, j, 0)),
            pl.BlockSpec((1, d), lambda i, j: (0, 0)),
            pl.BlockSpec((1, 1, d), lambda i, j: (i, 0, 0)),
            pl.BlockSpec((1, 1, d), lambda i, j: (i, 0, 0)),
            pl.BlockSpec((d, n), lambda i, j: (0, 0)),
        ],
        out_specs=[pl.BlockSpec((1, tm, m), lambda i, j: (i, j, 0)) for m in splits],
        out_shape=[jax.ShapeDtypeStruct((b, s, m), BF16) for m in splits],
        compiler_params=_cparams("arbitrary", "arbitrary"),
        name="norm_proj",
    )(x, g.reshape(1, d), sc.reshape(b, 1, d), sh.reshape(b, 1, d), w)


def _out_res_kernel(o_ref, w_ref, x_ref, gate_ref, y_ref):
    y_ref[0] = x_ref[0] + gate_ref[0] * _dot(o_ref[0], w_ref[...])


def _out_res(o, w, x, gate, tm=512):
    b, s, d = x.shape
    k = o.shape[-1]
    return pl.pallas_call(
        _out_res_kernel,
        grid=(b, s // tm),
        in_specs=[
            pl.BlockSpec((1, tm, k), lambda i, j: (i, j, 0)),
            pl.BlockSpec((k, d), lambda i, j: (0, 0)),
            pl.BlockSpec((1, tm, d), lambda i, j: (i, j, 0)),
            pl.BlockSpec((1, 1, d), lambda i, j: (i, 0, 0)),
        ],
        out_specs=pl.BlockSpec((1, tm, d), lambda i, j: (i, j, 0)),
        out_shape=jax.ShapeDtypeStruct((b, s, d), F32),
        compiler_params=_cparams("arbitrary", "arbitrary"),
        name="out_res",
    )(o, w, x, gate.reshape(b, 1, d))


def _ffn_kernel(x_ref, g_ref, sc_ref, sh_ref, gate_ref, wug_ref, wuv_ref, cwg_ref, cwv_ref,
                cbg_ref, cbv_ref, wd_ref, y_ref, hb_scr, acc_scr, cg_scr, cv_scr, *, nf):
    si = pl.program_id(1)
    fi = pl.program_id(2)
    tm = hb_scr.shape[0]

    @pl.when(fi == 0)
    def _():
        hb_scr[...] = _modulated_norm(x_ref[0], g_ref[...], sc_ref[0], sh_ref[0]).astype(BF16)
        acc_scr[...] = jnp.zeros_like(acc_scr)

    @pl.when(si == 0)
    def _():
        cg_scr[fi] = jnp.zeros(cg_scr.shape[1:], F32)
        cv_scr[fi] = jnp.zeros(cv_scr.shape[1:], F32)

    hb = hb_scr[...]
    row = lax.broadcasted_iota(I32, (tm, 1), 0)

    def conv_branch(wu_ref, cw_ref, cb_ref, carry_scr):
        u = _dot(hb, wu_ref[...])
        prev = carry_scr[fi]
        carry_scr[fi] = u[tm - SUBLANES:, :]
        p1 = prev[SUBLANES - 1:SUBLANES, :]
        p2 = prev[SUBLANES - 2:SUBLANES - 1, :]
        u1 = jnp.where(row == 0, p1, pltpu.roll(u, 1, 0))
        u2 = jnp.where(row == 0, p2, jnp.where(row == 1, p1, pltpu.roll(u, 2, 0)))
        cw = cw_ref[...]
        return cb_ref[...] + u2 * cw[0:1, :] + u1 * cw[1:2, :] + u * cw[2:3, :]

    yg = conv_branch(wug_ref, cwg_ref, cbg_ref, cg_scr)
    yv = conv_branch(wuv_ref, cwv_ref, cbv_ref, cv_scr)
    act = yg * (1.0 / (1.0 + jnp.exp(-yg))) * yv
    acc_scr[...] += _dot(act.astype(BF16), wd_ref[...])

    @pl.when(fi == nf - 1)
    def _():
        y_ref[0] = x_ref[0] + gate_ref[0] * acc_scr[...]


def _ffn(x, g, sc, sh, gate, w_up, conv_w, conv_b, w_down, tm=1024, fc=256):
    b, s, d = x.shape
    f = w_down.shape[0]
    nf = f // fc
    kern = functools.partial(_ffn_kernel, nf=nf)
    vec = lambda a: a.reshape(b, 1, d)
    return pl.pallas_call(
        kern,
        grid=(b, s // tm, nf),
        in_specs=[
            pl.BlockSpec((1, tm, d), lambda i, j, k: (i, j, 0)),
            pl.BlockSpec((1, d), lambda i, j, k: (0, 0)),
            pl.BlockSpec((1, 1, d), lambda i, j, k: (i, 0, 0)),
            pl.BlockSpec((1, 1, d), lambda i, j, k: (i, 0, 0)),
            pl.BlockSpec((1, 1, d), lambda i, j, k: (i, 0, 0)),
            pl.BlockSpec((d, fc), lambda i, j, k: (0, k)),
            pl.BlockSpec((d, fc), lambda i, j, k: (0, nf + k)),
            pl.BlockSpec((CONV_W, fc), lambda i, j, k: (0, k)),
            pl.BlockSpec((CONV_W, fc), lambda i, j, k: (0, nf + k)),
            pl.BlockSpec((1, fc), lambda i, j, k: (0, k)),
            pl.BlockSpec((1, fc), lambda i, j, k: (0, nf + k)),
            pl.BlockSpec((fc, d), lambda i, j, k: (k, 0)),
        ],
        out_specs=pl.BlockSpec((1, tm, d), lambda i, j, k: (i, j, 0)),
        out_shape=jax.ShapeDtypeStruct((b, s, d), F32),
        scratch_shapes=[
            pltpu.VMEM((tm, d), BF16),
            pltpu.VMEM((tm, d), F32),
            pltpu.VMEM((nf, SUBLANES, fc), F32),
            pltpu.VMEM((nf, SUBLANES, fc), F32),
        ],
        compiler_params=_cparams("arbitrary", "arbitrary", "arbitrary"),
        name="ffn",
    )(x, g.reshape(1, d), vec(sc), vec(sh), vec(gate), w_up, w_up, conv_w, conv_w,
      conv_b.reshape(1, 2 * f), conv_b.reshape(1, 2 * f), w_down)


def _sb_kernel(q_ref, k_ref, v_ref, tri_ref, o_ref, *, tq):
    qi = pl.program_id(2)
    lane = lax.broadcasted_iota(I32, (1, LANES), 1)
    halves = (lane < SB_HEAD_DIM, lane >= SB_HEAD_DIM)
    q2 = q_ref[0] * (SB_HEAD_DIM ** -0.5)
    zero = jnp.zeros_like(q2)
    qs = tuple(jnp.where(hm, q2, zero) for hm in halves)
    tri = tri_ref[...]
    row = lax.broadcasted_iota(I32, (tq, tq), 0)
    col = lax.broadcasted_iota(I32, (tq, tq), 1)
    strict = col < row

    def block(jb, carry, masked):
        acc, r_a, r_b = carry
        start = pl.multiple_of(jb * tq, tq)
        kb = k_ref[0, pl.ds(start, tq), :]
        vb = v_ref[0, pl.ds(start, tq), :]
        r_new = []
        for hm, qh, r in zip(halves, qs, (r_a, r_b)):
            z = _dot_nt(qh, kb)
            lk = -(jnp.maximum(z, 0.0) + jnp.log(1.0 + jnp.exp(-jnp.abs(z))))
            if masked:
                lk = jnp.where(strict, lk, 0.0)
            hi, lo = _split_bf16(lk)
            incl = _dot(hi, tri) + _dot(lo, tri)
            a = jnp.exp(z + incl + r)
            if masked:
                a = jnp.where(strict, a, 0.0)
            vh = jnp.where(hm, vb, jnp.zeros_like(vb))
            acc = acc + _dot(a.astype(BF16), vh)
            r_new.append(r + incl[:, 0:1])
        return acc, r_new[0], r_new[1]

    def live(carry):
        return jnp.max(jnp.maximum(carry[1], carry[2])) > SB_EXIT

    def step(state):
        i, carry, _ = state
        carry = block(qi - 1 - i, carry, False)
        return i + 1, carry, live(carry)

    init = (jnp.zeros((tq, LANES), F32), jnp.zeros((tq, 1), F32), jnp.zeros((tq, 1), F32))
    carry = block(qi, init, True)
    _, carry, _ = lax.while_loop(lambda st: (st[0] < qi) & st[2], step, (jnp.int32(0), carry, live(carry)))
    o_ref[0] = carry[0].astype(o_ref.dtype)


def _sb_attention(qkv, tq=256):
    b, s, n3 = qkv.shape
    n = n3 // 3
    npair = n // LANES
    tri = (jnp.arange(tq)[:, None] >= jnp.arange(tq)[None, :]).astype(BF16)
    kern = functools.partial(_sb_kernel, tq=tq)
    return pl.pallas_call(
        kern,
        grid=(b, npair, s // tq),
        in_specs=[
            pl.BlockSpec((1, tq, LANES), lambda i, p, j: (i, j, p)),
            pl.BlockSpec((1, s, LANES), lambda i, p, j: (i, 0, npair + p)),
            pl.BlockSpec((1, s, LANES), lambda i, p, j: (i, 0, 2 * npair + p)),
            pl.BlockSpec((tq, tq), lambda i, p, j: (0, 0)),
        ],
        out_specs=pl.BlockSpec((1, tq, LANES), lambda i, p, j: (i, j, p)),
        out_shape=jax.ShapeDtypeStruct((b, s, n), BF16),
        compiler_params=_cparams("arbitrary", "arbitrary", "arbitrary"),
        name="sb_attn",
    )(qkv, qkv, qkv, tri)


def _t5_bucket_thresholds():
    max_exact = NUM_BUCKETS // 2
    n = np.arange(0, 4 * MAX_DISTANCE, dtype=np.int32)
    nf = np.maximum(n, 1).astype(np.float32)
    large = max_exact + (np.log(nf / np.float32(max_exact)) / np.float32(math.log(MAX_DISTANCE / max_exact))
                         * np.float32(NUM_BUCKETS - max_exact)).astype(np.int32)
    large = np.minimum(large, NUM_BUCKETS - 1)
    bucket = np.where(n < max_exact, n, large)
    assert np.all(np.diff(bucket) >= 0) and bucket[-1] == NUM_BUCKETS - 1
    return [int(np.argmax(bucket >= bb)) for bb in range(max_exact + 1, NUM_BUCKETS)]


def _sortable_key(v):
    v = jnp.where(v == 0.0, 0.0, v)
    bits = lax.bitcast_convert_type(v, I32)
    return bits ^ ((bits >> 31) & 0x7FFFFFFF)


def _dsa_kernel(rb_ref, q_ref, qidx_ref, lat_ref, kidx_ref, qg_ref, kg_ref, wuv_ref, tri_ref,
                o_ref, kn_scr, key_scr, qn_scr, acc_scr, m_scr, l_scr, eqc_scr, bias_scr,
                *, tq, topk, thresholds):
    bi = pl.program_id(0)
    qi = pl.program_id(1)
    nlane = tq // LANES
    row = lax.broadcasted_iota(I32, (tq, tq), 0)
    col = lax.broadcasted_iota(I32, (tq, tq), 1)
    row1 = lax.broadcasted_iota(I32, (tq, 1), 0)

    @pl.when((bi == 0) & (qi == 0))
    def _():
        max_exact = NUM_BUCKETS // 2
        for near in range(2):
            dist = row - col + near * tq
            large = jnp.full((tq, tq), max_exact, I32)
            for th in thresholds:
                large = large + jnp.where(dist >= th, 1, 0)
            bucket = jnp.where(dist < max_exact, jnp.maximum(dist, 0), large)

            def head_tile(h, _, near=near, bucket=bucket):
                far = rb_ref[NUM_BUCKETS - 1, h]
                tile = jnp.zeros((tq, tq), F32)
                for bb in range(NUM_BUCKETS - 1):
                    tile = jnp.where(bucket == bb, rb_ref[bb, h] - far, tile)
                bias_scr[near, h] = tile
                return 0

            lax.fori_loop(0, DSA_HEADS, head_tile, 0)

    @pl.when(qi == 0)
    def _():
        lat = lat_ref[0].astype(F32)
        ms = jnp.mean(lat * lat, axis=-1, keepdims=True)
        kn_scr[...] = (lat * lax.rsqrt(ms + RMS_EPS) * kg_ref[...]).astype(BF16)

    scale = DSA_LATENT ** -0.5
    for h in range(DSA_HEADS):
        qh = q_ref[0, :, h * DSA_LATENT:(h + 1) * DSA_LATENT].astype(F32)
        ms = jnp.mean(qh * qh, axis=-1, keepdims=True)
        qn_scr[h] = (qh * lax.rsqrt(ms + RMS_EPS) * (qg_ref[...] * scale)).astype(BF16)

    lane = lax.broadcasted_iota(I32, (1, LANES), 1)
    halves = (lane < IDX_DIM, lane >= IDX_DIM)
    n_qi = IDX_HEADS * IDX_DIM
    wi = qidx_ref[0, :, n_qi + LANES:n_qi + 2 * LANES].astype(F32) * (IDX_HEADS ** -0.5)
    qms = []
    for hh in range(IDX_HEADS):
        blk = qidx_ref[0, :, (hh // 2) * LANES:(hh // 2 + 1) * LANES]
        qms.append(jnp.where(halves[hh % 2], blk, jnp.zeros_like(blk)))

    def score_block(jb, _):
        start = pl.multiple_of(jb * tq, tq)
        kb = kidx_ref[0, pl.ds(start, tq), :]
        isc = jnp.zeros((tq, tq), F32)
        for hh in range(IDX_HEADS):
            isc = isc + jnp.maximum(_dot_nt(qms[hh], kb), 0.0) * wi[:, hh:hh + 1]
        key = _sortable_key(isc)
        key_scr[jb] = jnp.where(col + jb * tq <= row + qi * tq, key, INT_MIN)
        return 0

    lax.fori_loop(0, qi + 1, score_block, 0)

    def count(pred):
        def body(jb, part):
            c = jnp.where(pred(key_scr[jb]), 1.0, 0.0)
            for g in range(nlane):
                part = part + c[:, g * LANES:(g + 1) * LANES]
            return part
        part = lax.fori_loop(0, qi + 1, body, jnp.zeros((tq, LANES), F32))
        return jnp.sum(part, axis=1, keepdims=True)

    kf = float(topk)
    thr = jnp.where(count(lambda k: k >= 0) >= kf, 0, INT_MIN).astype(I32)

    def bisect(i, thr):
        cand = thr | lax.shift_left(jnp.int32(1), jnp.int32(30) - i)
        return jnp.where(count(lambda k: k >= cand) >= kf, cand, thr)

    thr = lax.fori_loop(0, 31, bisect, thr)
    thr = jnp.where(row1 + qi * tq + 1 <= topk, INT_MIN, thr)
    need = kf - count(lambda k: k > thr)

    m_scr[...] = jnp.full_like(m_scr, NEG)
    l_scr[...] = jnp.zeros_like(l_scr)
    acc_scr[...] = jnp.zeros_like(acc_scr)
    eqc_scr[...] = jnp.zeros_like(eqc_scr)
    tri = tri_ref[...]

    def attend_block(jb, near):
        start = pl.multiple_of(jb * tq, tq)
        key = key_scr[jb]
        eq = key == thr
        rank = _dot(jnp.where(eq, 1.0, 0.0).astype(BF16), tri) + eqc_scr[...]
        eqc_scr[...] = rank[:, tq - 1:tq]
        sel = (key > thr) | (eq & (rank <= need))
        sel = sel & (col + jb * tq <= row + qi * tq)
        madd = jnp.where(sel, 0.0, NEG)
        knb = kn_scr[pl.ds(start, tq), :]
        vb = lat_ref[0, pl.ds(start, tq), :]

        def head(h, _):
            lg = _dot_nt(qn_scr[h], knb) + madd
            if near is not None:
                lg = lg + bias_scr[near, h]
            m_old = m_scr[h]
            m_new = jnp.maximum(m_old, jnp.max(lg, axis=-1, keepdims=True))
            alpha = jnp.exp(m_old - m_new)
            p = jnp.exp(lg - m_new)
            l_scr[h] = alpha * l_scr[h] + jnp.sum(p, axis=-1, keepdims=True)
            acc_scr[h] = alpha * acc_scr[h] + _dot(p.astype(BF16), vb)
            m_scr[h] = m_new
            return 0

        lax.fori_loop(0, DSA_HEADS, head, 0, unroll=2)

    def far_block(jb, _):
        attend_block(jb, None)
        return 0

    lax.fori_loop(0, jnp.maximum(qi - 1, 0), far_block, 0)

    @pl.when(qi >= 1)
    def _():
        attend_block(qi - 1, 1)

    attend_block(qi, 0)

    for pr in range(DSA_HEADS // 2):
        out = jnp.zeros((tq, LANES), F32)
        for h in (2 * pr, 2 * pr + 1):
            ol = acc_scr[h] * (1.0 / l_scr[h])
            out = out + _dot(ol.astype(BF16), wuv_ref[h])
        o_ref[0, :, pr * LANES:(pr + 1) * LANES] = out.astype(o_ref.dtype)


def _dsa_attention(qp, idx, lat, q_gain, k_gain, wuv_ext, rel_bias, tq=256):
    b, s, nq = qp.shape
    topk = min(TOPK_MAX, s // 4)
    nkb = s // tq
    n_idx = idx.shape[-1]
    tri = (jnp.arange(tq)[:, None] <= jnp.arange(tq)[None, :]).astype(BF16)
    kern = functools.partial(_dsa_kernel, tq=tq, topk=topk, thresholds=tuple(_t5_bucket_thresholds()))
    return pl.pallas_call(
        kern,
        grid=(b, s // tq),
        in_specs=[
            pl.BlockSpec(memory_space=pltpu.SMEM),
            pl.BlockSpec((1, tq, nq), lambda i, j: (i, j, 0)),
            pl.BlockSpec((1, tq, n_idx), lambda i, j: (i, j, 0)),
            pl.BlockSpec((1, s, LANES), lambda i, j: (i, 0, 0)),
            pl.BlockSpec((1, s, LANES), lambda i, j: (i, 0, IDX_HEADS * IDX_DIM // LANES)),
            pl.BlockSpec((1, DSA_LATENT), lambda i, j: (0, 0)),
            pl.BlockSpec((1, DSA_LATENT), lambda i, j: (0, 0)),
            pl.BlockSpec((DSA_HEADS, DSA_LATENT, LANES), lambda i, j: (0, 0, 0)),
            pl.BlockSpec((tq, tq), lambda i, j: (0, 0)),
        ],
        out_specs=pl.BlockSpec((1, tq, DSA_HEADS * DSA_V_DIM), lambda i, j: (i, j, 0)),
        out_shape=jax.ShapeDtypeStruct((b, s, DSA_HEADS * DSA_V_DIM), BF16),
        scratch_shapes=[
            pltpu.VMEM((s, DSA_LATENT), BF16),
            pltpu.VMEM((nkb, tq, tq), I32),
            pltpu.VMEM((DSA_HEADS, tq, DSA_LATENT), BF16),
            pltpu.VMEM((DSA_HEADS, tq, DSA_LATENT), F32),
            pltpu.VMEM((DSA_HEADS, tq, 1), F32),
            pltpu.VMEM((DSA_HEADS, tq, 1), F32),
            pltpu.VMEM((tq, 1), F32),
            pltpu.VMEM((2, DSA_HEADS, tq, tq), F32),
        ],
        compiler_params=_cparams("arbitrary", "arbitrary"),
        name="dsa_attn",
    )(rel_bias, qp, idx, lat, idx, q_gain.reshape(1, -1), k_gain.reshape(1, -1), wuv_ext, tri)


def _dsa_weights(w_in, w_uv):
    o1 = DSA_HEADS * DSA_LATENT
    o2 = o1 + DSA_LATENT
    o3 = o2 + IDX_HEADS * IDX_DIM
    o4 = o3 + IDX_DIM
    d = w_in.shape[0]
    w_ki = w_in[:, o3:o4]
    w_wi = w_in[:, o4:]
    pad = jnp.zeros((d, LANES - IDX_HEADS), w_in.dtype)
    w = jnp.concatenate([w_in[:, :o3], w_ki, w_ki, w_wi, pad], axis=1).astype(BF16)
    zeros = jnp.zeros_like(w_uv)
    even = jnp.concatenate([w_uv, zeros], axis=-1)
    odd = jnp.concatenate([zeros, w_uv], axis=-1)
    is_even = (jnp.arange(DSA_HEADS) % 2 == 0)[:, None, None]
    wuv_ext = jnp.where(is_even, even, odd).astype(BF16)
    splits = (o1, DSA_LATENT, IDX_HEADS * IDX_DIM + 2 * LANES)
    return w, wuv_ext, splits


def kernel(x, c, ada_w, ada_b, norm_mix, norm_ffn, sb_w_in, sb_w_out, dsa_w_in, dsa_q_norm,
           dsa_k_norm, dsa_w_uv, dsa_w_out, rel_bias, ffn_w_up, ffn_conv_w, ffn_conv_b, ffn_w_down):
    depth = ada_w.shape[0]
    d = x.shape[-1]
    mod = _modulation(c, ada_w, ada_b)
    for i in range(depth):
        sh1, sc1, g1, sh2, sc2, g2 = [mod[i, :, k * d:(k + 1) * d] for k in range(6)]
        j = i // 2
        if i % 2 == 0:
            w_in = sb_w_in[j].astype(BF16)
            (qkv,) = _norm_proj(x, norm_mix[i], sc1, sh1, w_in, (w_in.shape[1],))
            o = _sb_attention(qkv)
            w_out = sb_w_out[j]
        else:
            w_in, wuv_ext, splits = _dsa_weights(dsa_w_in[j], dsa_w_uv[j])
            qp, lat, idx = _norm_proj(x, norm_mix[i], sc1, sh1, w_in, splits)
            o = _dsa_attention(qp, idx, lat, dsa_q_norm[j], dsa_k_norm[j], wuv_ext, rel_bias)
            w_out = dsa_w_out[j]
        x = _out_res(o, w_out.astype(BF16), x, g1)
        x = _ffn(x, norm_ffn[i], sc2, sh2, g2, ffn_w_up[i].astype(BF16), ffn_conv_w[i],
                 ffn_conv_b[i], ffn_w_down[i].astype(BF16))
    return x
```

```python
import functools
import math

import numpy as np
import jax
import jax.numpy as jnp
from jax import lax
from jax.experimental import pallas as pl
from jax.experimental.pallas import tpu as pltpu

F32 = jnp.float32
BF16 = jnp.bfloat16
I32 = jnp.int32

LANES = 128
SUBLANES = 8
VMEM_LIMIT = 56 * 1024 * 1024

RMS_EPS = 1e-6
NEG = -1e30
SB_EXIT = -104.0
INT_MIN = -(2 ** 31)

SB_HEADS = 16
SB_HEAD_DIM = 64
DSA_HEADS = 16
DSA_LATENT = 128
DSA_V_DIM = 64
DSA_GROUP = 4
IDX_HEADS = 8
IDX_DIM = 64
TOPK_MAX = 256
NUM_BUCKETS = 32
MAX_DISTANCE = 128
CONV_W = 3


def _cparams(*sem):
    return pltpu.CompilerParams(dimension_semantics=sem, vmem_limit_bytes=VMEM_LIMIT)


def _dot(a, b):
    return jnp.dot(a, b, preferred_element_type=F32)


def _dot_nt(a, b):
    return lax.dot_general(a, b, (((1,), (1,)), ((), ())), preferred_element_type=F32)


def _split_bf16(v):
    hi = v.astype(BF16)
    lo = (v - hi.astype(F32)).astype(BF16)
    return hi, lo


def _mod_kernel(c_ref, w_ref, b_ref, o_ref):
    c = c_ref[...]
    cond = c * (1.0 / (1.0 + jnp.exp(-c)))
    ch, cl = _split_bf16(cond)
    wh, wl = _split_bf16(w_ref[0])
    o_ref[0] = _dot(ch, wh) + _dot(ch, wl) + _dot(cl, wh) + b_ref[0]


def _modulation(c, ada_w, ada_b):
    depth, d, n = ada_w.shape
    b = c.shape[0]
    tn = 1024
    return pl.pallas_call(
        _mod_kernel,
        grid=(depth, n // tn),
        in_specs=[
            pl.BlockSpec((b, d), lambda i, j: (0, 0)),
            pl.BlockSpec((1, d, tn), lambda i, j: (i, 0, j)),
            pl.BlockSpec((1, 1, tn), lambda i, j: (i, 0, j)),
        ],
        out_specs=pl.BlockSpec((1, b, tn), lambda i, j: (i, 0, j)),
        out_shape=jax.ShapeDtypeStruct((depth, b, n), F32),
        compiler_params=_cparams("arbitrary", "arbitrary"),
        name="mod",
    )(c, ada_w, ada_b.reshape(depth, 1, n))


def _modulated_norm(x, g, sc, sh):
    ms = jnp.mean(x * x, axis=-1, keepdims=True)
    return (x * lax.rsqrt(ms + RMS_EPS) * g) * (1.0 + sc) + sh


def _norm_proj_kernel(x_ref, g_ref, sc_ref, sh_ref, w_ref, *o_refs, splits, chunk):
    hb = _modulated_norm(x_ref[0], g_ref[...], sc_ref[0], sh_ref[0]).astype(BF16)
    off = 0
    for o_ref, n in zip(o_refs, splits):
        for c0 in range(0, n, chunk):
            c1 = min(n, c0 + chunk)
            o_ref[0, :, c0:c1] = _dot(hb, w_ref[:, off + c0:off + c1]).astype(o_ref.dtype)
        off += n


def _norm_proj(x, g, sc, sh, w, splits, tm=512):
    b, s, d = x.shape
    n = w.shape[1]
    assert sum(splits) == n
    kern = functools.partial(_norm_proj_kernel, splits=tuple(splits), chunk=512)
    return pl.pallas_call(
        kern,
        grid=(b, s // tm),
        in_specs=[
            pl.BlockSpec((1, tm, d), lambda i, j: (i, j, 0)),
            pl.BlockSpec((1, d), lambda i, j: (0, 0)),
            pl.BlockSpec((1, 1, d), lambda i, j: (i, 0, 0)),
            pl.BlockSpec((1, 1, d), lambda i, j: (i, 0, 0)),
            pl.BlockSpec((d, n), lambda i, j: (0, 0)),
        ],
        out_specs=[pl.BlockSpec((1, tm, m), lambda i, j: (i, j, 0)) for m in splits],
        out_shape=[jax.ShapeDtypeStruct((b, s, m), BF16) for m in splits],
        compiler_params=_cparams("arbitrary", "arbitrary"),
        name="norm_proj",
    )(x, g.reshape(1, d), sc.reshape(b, 1, d), sh.reshape(b, 1, d), w)


def _out_res_kernel(o_ref, w_ref, x_ref, gate_ref, y_ref):
    y_ref[0] = x_ref[0] + gate_ref[0] * _dot(o_ref[0], w_ref[...])


def _out_res(o, w, x, gate, tm=512):
    b, s, d = x.shape
    k = o.shape[-1]
    return pl.pallas_call(
        _out_res_kernel,
        grid=(b, s // tm),
        in_specs=[
            pl.BlockSpec((1, tm, k), lambda i, j: (i, j, 0)),
            pl.BlockSpec((k, d), lambda i, j: (0, 0)),
            pl.BlockSpec((1, tm, d), lambda i, j: (i, j, 0)),
            pl.BlockSpec((1, 1, d), lambda i, j: (i, 0, 0)),
        ],
        out_specs=pl.BlockSpec((1, tm, d), lambda i, j: (i, j, 0)),
        out_shape=jax.ShapeDtypeStruct((b, s, d), F32),
        compiler_params=_cparams("arbitrary", "arbitrary"),
        name="out_res",
    )(o, w, x, gate.reshape(b, 1, d))


def _ffn_kernel(x_ref, g_ref, sc_ref, sh_ref, gate_ref, wug_ref, wuv_ref, cwg_ref, cwv_ref,
                cbg_ref, cbv_ref, wd_ref, y_ref, hb_scr, acc_scr, cg_scr, cv_scr, *, nf):
    si = pl.program_id(1)
    fi = pl.program_id(2)
    tm = hb_scr.shape[0]

    @pl.when(fi == 0)
    def _():
        hb_scr[...] = _modulated_norm(x_ref[0], g_ref[...], sc_ref[0], sh_ref[0]).astype(BF16)
        acc_scr[...] = jnp.zeros_like(acc_scr)

    @pl.when(si == 0)
    def _():
        cg_scr[fi] = jnp.zeros(cg_scr.shape[1:], F32)
        cv_scr[fi] = jnp.zeros(cv_scr.shape[1:], F32)

    hb = hb_scr[...]
    row = lax.broadcasted_iota(I32, (tm, 1), 0)

    def conv_branch(wu_ref, cw_ref, cb_ref, carry_scr):
        u = _dot(hb, wu_ref[...])
        prev = carry_scr[fi]
        carry_scr[fi] = u[tm - SUBLANES:, :]
        p1 = prev[SUBLANES - 1:SUBLANES, :]
        p2 = prev[SUBLANES - 2:SUBLANES - 1, :]
        u1 = jnp.where(row == 0, p1, pltpu.roll(u, 1, 0))
        u2 = jnp.where(row == 0, p2, jnp.where(row == 1, p1, pltpu.roll(u, 2, 0)))
        cw = cw_ref[...]
        return cb_ref[...] + u2 * cw[0:1, :] + u1 * cw[1:2, :] + u * cw[2:3, :]

    yg = conv_branch(wug_ref, cwg_ref, cbg_ref, cg_scr)
    yv = conv_branch(wuv_ref, cwv_ref, cbv_ref, cv_scr)
    act = yg * (1.0 / (1.0 + jnp.exp(-yg))) * yv
    acc_scr[...] += _dot(act.astype(BF16), wd_ref[...])

    @pl.when(fi == nf - 1)
    def _():
        y_ref[0] = x_ref[0] + gate_ref[0] * acc_scr[...]


def _ffn(x, g, sc, sh, gate, w_up, conv_w, conv_b, w_down, tm=1024, fc=256):
    b, s, d = x.shape
    f = w_down.shape[0]
    nf = f // fc
    kern = functools.partial(_ffn_kernel, nf=nf)
    vec = lambda a: a.reshape(b, 1, d)
    return pl.pallas_call(
        kern,
        grid=(b, s // tm, nf),
        in_specs=[
            pl.BlockSpec((1, tm, d), lambda i, j, k: (i, j, 0)),
            pl.BlockSpec((1, d), lambda i, j, k: (0, 0)),
            pl.BlockSpec((1, 1, d), lambda i, j, k: (i, 0, 0)),
            pl.BlockSpec((1, 1, d), lambda i, j, k: (i, 0, 0)),
            pl.BlockSpec((1, 1, d), lambda i, j, k: (i, 0, 0)),
            pl.BlockSpec((d, fc), lambda i, j, k: (0, k)),
            pl.BlockSpec((d, fc), lambda i, j, k: (0, nf + k)),
            pl.BlockSpec((CONV_W, fc), lambda i, j, k: (0, k)),
            pl.BlockSpec((CONV_W, fc), lambda i, j, k: (0, nf + k)),
            pl.BlockSpec((1, fc), lambda i, j, k: (0, k)),
            pl.BlockSpec((1, fc), lambda i, j, k: (0, nf + k)),
            pl.BlockSpec((fc, d), lambda i, j, k: (k, 0)),
        ],
        out_specs=pl.BlockSpec((1, tm, d), lambda i, j, k: (i, j, 0)),
        out_shape=jax.ShapeDtypeStruct((b, s, d), F32),
        scratch_shapes=[
            pltpu.VMEM((tm, d), BF16),
            pltpu.VMEM((tm, d), F32),
            pltpu.VMEM((nf, SUBLANES, fc), F32),
            pltpu.VMEM((nf, SUBLANES, fc), F32),
        ],
        compiler_params=_cparams("arbitrary", "arbitrary", "arbitrary"),
        name="ffn",
    )(x, g.reshape(1, d), vec(sc), vec(sh), vec(gate), w_up, w_up, conv_w, conv_w,
      conv_b.reshape(1, 2 * f), conv_b.reshape(1, 2 * f), w_down)


def _sb_kernel(q_ref, k_ref, v_ref, tri_ref, o_ref, *, tq):
    qi = pl.program_id(2)
    lane = lax.broadcasted_iota(I32, (1, LANES), 1)
    halves = (lane < SB_HEAD_DIM, lane >= SB_HEAD_DIM)
    q2 = q_ref[0] * (SB_HEAD_DIM ** -0.5)
    zero = jnp.zeros_like(q2)
    qs = tuple(jnp.where(hm, q2, zero) for hm in halves)
    tri = tri_ref[...]
    row = lax.broadcasted_iota(I32, (tq, tq), 0)
    col = lax.broadcasted_iota(I32, (tq, tq), 1)
    strict = col < row

    def block(jb, carry, masked):
        acc, r_a, r_b = carry
        start = pl.multiple_of(jb * tq, tq)
        kb = k_ref[0, pl.ds(start, tq), :]
        vb = v_ref[0, pl.ds(start, tq), :]
        r_new = []
        for hm, qh, r in zip(halves, qs, (r_a, r_b)):
            z = _dot_nt(qh, kb)
            lk = -(jnp.maximum(z, 0.0) + jnp.log(1.0 + jnp.exp(-jnp.abs(z))))
            if masked:
                lk = jnp.where(strict, lk, 0.0)
            hi, lo = _split_bf16(lk)
            incl = _dot(hi, tri) + _dot(lo, tri)
            a = jnp.exp(z + incl + r)
            if masked:
                a = jnp.where(strict, a, 0.0)
            vh = jnp.where(hm, vb, jnp.zeros_like(vb))
            acc = acc + _dot(a.astype(BF16), vh)
            r_new.append(r + incl[:, 0:1])
        return acc, r_new[0], r_new[1]

    def live(carry):
        return jnp.max(jnp.maximum(carry[1], carry[2])) > SB_EXIT

    def step(state):
        i, carry, _ = state
        carry = block(qi - 1 - i, carry, False)
        return i + 1, carry, live(carry)

    init = (jnp.zeros((tq, LANES), F32), jnp.zeros((tq, 1), F32), jnp.zeros((tq, 1), F32))
    carry = block(qi, init, True)
    _, carry, _ = lax.while_loop(lambda st: (st[0] < qi) & st[2], step, (jnp.int32(0), carry, live(carry)))
    o_ref[0] = carry[0].astype(o_ref.dtype)


def _sb_attention(qkv, tq=256):
    b, s, n3 = qkv.shape
    n = n3 // 3
    npair = n // LANES
    tri = (jnp.arange(tq)[:, None] >= jnp.arange(tq)[None, :]).astype(BF16)
    kern = functools.partial(_sb_kernel, tq=tq)
    return pl.pallas_call(
        kern,
        grid=(b, npair, s // tq),
        in_specs=[
            pl.BlockSpec((1, tq, LANES), lambda i, p, j: (i, j, p)),
            pl.BlockSpec((1, s, LANES), lambda i, p, j: (i, 0, npair + p)),
            pl.BlockSpec((1, s, LANES), lambda i, p, j: (i, 0, 2 * npair + p)),
            pl.BlockSpec((tq, tq), lambda i, p, j: (0, 0)),
        ],
        out_specs=pl.BlockSpec((1, tq, LANES), lambda i, p, j: (i, j, p)),
        out_shape=jax.ShapeDtypeStruct((b, s, n), BF16),
        compiler_params=_cparams("arbitrary", "arbitrary", "arbitrary"),
        name="sb_attn",
    )(qkv, qkv, qkv, tri)


def _t5_bucket_thresholds():
    max_exact = NUM_BUCKETS // 2
    n = np.arange(0, 4 * MAX_DISTANCE, dtype=np.int32)
    nf = np.maximum(n, 1).astype(np.float32)
    large = max_exact + (np.log(nf / np.float32(max_exact)) / np.float32(math.log(MAX_DISTANCE / max_exact))
                         * np.float32(NUM_BUCKETS - max_exact)).astype(np.int32)
    large = np.minimum(large, NUM_BUCKETS - 1)
    bucket = np.where(n < max_exact, n, large)
    assert np.all(np.diff(bucket) >= 0) and bucket[-1] == NUM_BUCKETS - 1
    return [int(np.argmax(bucket >= bb)) for bb in range(max_exact + 1, NUM_BUCKETS)]


def _sortable_key(v):
    v = jnp.where(v == 0.0, 0.0, v)
    bits = lax.bitcast_convert_type(v, I32)
    return bits ^ ((bits >> 31) & 0x7FFFFFFF)


def _tree_sum(parts):
    parts = list(parts)
    while len(parts) > 1:
        parts = [parts[i] + parts[i + 1] if i + 1 < len(parts) else parts[i] for i in range(0, len(parts), 2)]
    return parts[0]


def _dsa_kernel(rb_ref, q_ref, qidx_ref, lat_ref, kidx_ref, qg_ref, kg_ref, wuv_ref, tri_ref,
                o_ref, kn_scr, vt_scr, key_scr, madd_scr, qn_scr, olat_scr, bias_scr,
                *, tq, topk, thresholds):
    bi = pl.program_id(0)
    qi = pl.program_id(1)
    nkb = vt_scr.shape[0]
    krow = lax.broadcasted_iota(I32, (tq, tq), 0)
    qcol = lax.broadcasted_iota(I32, (tq, tq), 1)
    qcol1 = lax.broadcasted_iota(I32, (1, tq), 1)

    @pl.when((bi == 0) & (qi == 0))
    def _():
        max_exact = NUM_BUCKETS // 2
        for near in range(2):
            dist = qcol - krow + near * tq
            large = jnp.full((tq, tq), max_exact, I32)
            for th in thresholds:
                large = large + jnp.where(dist >= th, 1, 0)
            bucket = jnp.where(dist < max_exact, jnp.maximum(dist, 0), large)

            def head_tile(h, _, near=near, bucket=bucket):
                far = rb_ref[NUM_BUCKETS - 1, h]
                tile = jnp.zeros((tq, tq), F32)
                for bb in range(NUM_BUCKETS - 1):
                    tile = jnp.where(bucket == bb, rb_ref[bb, h] - far, tile)
                bias_scr[near, h] = tile
                return 0

            lax.fori_loop(0, DSA_HEADS, head_tile, 0)

    @pl.when(qi == 0)
    def _():
        for jb in range(nkb):
            lat = lat_ref[0, jb * tq:(jb + 1) * tq, :].astype(F32)
            ms = jnp.mean(lat * lat, axis=-1, keepdims=True)
            kn_scr[jb * tq:(jb + 1) * tq, :] = (lat * lax.rsqrt(ms + RMS_EPS) * kg_ref[...]).astype(BF16)
            vt_scr[jb] = lat.T.astype(BF16)

    scale = DSA_LATENT ** -0.5
    for h in range(DSA_HEADS):
        qh = q_ref[0, :, h * DSA_LATENT:(h + 1) * DSA_LATENT].astype(F32)
        ms = jnp.mean(qh * qh, axis=-1, keepdims=True)
        qn_scr[h * tq:(h + 1) * tq, :] = (qh * lax.rsqrt(ms + RMS_EPS) * (qg_ref[...] * scale)).astype(BF16)

    lane = lax.broadcasted_iota(I32, (1, LANES), 1)
    halves = (lane < IDX_DIM, lane >= IDX_DIM)
    n_qi = IDX_HEADS * IDX_DIM
    wi = qidx_ref[0, :, n_qi + LANES:n_qi + 2 * LANES].astype(F32) * (IDX_HEADS ** -0.5)
    wit = wi.T
    qms = []
    for hh in range(IDX_HEADS):
        blk = qidx_ref[0, :, (hh // 2) * LANES:(hh // 2 + 1) * LANES]
        qms.append(jnp.where(halves[hh % 2], blk, jnp.zeros_like(blk)))

    def causal(jb):
        return krow + jb * tq <= qcol + qi * tq

    def score_block(jb, _):
        start = pl.multiple_of(jb * tq, tq)
        kb = kidx_ref[0, pl.ds(start, tq), :]
        isc = jnp.zeros((tq, tq), F32)
        for hh in range(IDX_HEADS):
            isc = isc + jnp.maximum(_dot_nt(kb, qms[hh]), 0.0) * wit[hh:hh + 1, :]
        key_scr[jb] = jnp.where(causal(jb), _sortable_key(isc), INT_MIN)
        return 0

    lax.fori_loop(0, qi + 1, score_block, 0)

    def count(pred):
        def body(jb, part):
            c = jnp.where(pred(key_scr[jb]), 1.0, 0.0)
            return part + _tree_sum(c[g * SUBLANES:(g + 1) * SUBLANES, :] for g in range(tq // SUBLANES))
        part = lax.fori_loop(0, qi + 1, body, jnp.zeros((SUBLANES, tq), F32))
        return jnp.sum(part, axis=0, keepdims=True)

    kf = float(topk)
    thr = jnp.where(count(lambda k: k >= 0) >= kf, 0, INT_MIN).astype(I32)

    def bisect(i, thr):
        cand = thr | lax.shift_left(jnp.int32(1), jnp.int32(30) - i)
        return jnp.where(count(lambda k: k >= cand) >= kf, cand, thr)

    thr = lax.fori_loop(0, 31, bisect, thr)
    thr = jnp.where(qcol1 + qi * tq + 1 <= topk, INT_MIN, thr)
    need = kf - count(lambda k: k > thr)

    tri = tri_ref[...]

    def mask_block(jb, eq_before):
        key = key_scr[jb]
        eq = key == thr
        rank = _dot(tri, jnp.where(eq, 1.0, 0.0).astype(BF16)) + eq_before
        sel = ((key > thr) | (eq & (rank <= need))) & causal(jb)
        madd_scr[jb] = jnp.where(sel, 0.0, NEG)
        return rank[tq - 1:tq, :]

    lax.fori_loop(0, qi + 1, mask_block, jnp.zeros((1, tq), F32))

    prev = jnp.maximum(qi - 1, 0)
    prev_pen = jnp.where(qi >= 1, 0.0, NEG)

    def head_group(g, _):
        hs = [g * DSA_GROUP + k for k in range(DSA_GROUP)]
        row0 = pl.multiple_of(g * (DSA_GROUP * tq), DSA_GROUP * tq)
        qg = qn_scr[pl.ds(row0, DSA_GROUP * tq), :]

        def block(jb, carries, extra):
            start = pl.multiple_of(jb * tq, tq)
            lg = _dot_nt(kn_scr[pl.ds(start, tq), :], qg)
            madd = madd_scr[jb]
            stats, ps = [], []
            for k, (m, l, _) in enumerate(carries):
                add = madd if extra is None else madd + extra(hs[k])
                lk = lg[:, k * tq:(k + 1) * tq] + add
                m_new = jnp.maximum(m, jnp.max(lk, axis=0, keepdims=True))
                alpha = jnp.exp(m - m_new)
                p = jnp.exp(lk - m_new)
                stats.append((m_new, alpha, alpha * l + jnp.sum(p, axis=0, keepdims=True)))
                ps.append(p.astype(BF16))
            pv = _dot(vt_scr[jb], jnp.concatenate(ps, axis=1))
            return tuple((m_new, l_new, alpha * acc + pv[:, k * tq:(k + 1) * tq])
                         for k, ((m_new, alpha, l_new), (_, _, acc)) in enumerate(zip(stats, carries)))

        init = (jnp.full((1, tq), 0.5 * NEG, F32), jnp.zeros((1, tq), F32),
                jnp.zeros((DSA_LATENT, tq), F32))
        carries = lax.fori_loop(0, prev, lambda jb, cs: block(jb, cs, None), (init,) * DSA_GROUP)
        carries = block(prev, carries, lambda h: bias_scr[1, h] + prev_pen)
        carries = block(qi, carries, lambda h: bias_scr[0, h])
        for h, (_, l, acc) in zip(hs, carries):
            olat_scr[h] = (acc * (1.0 / l)).T.astype(BF16)
        return 0

    lax.fori_loop(0, DSA_HEADS // DSA_GROUP, head_group, 0)

    for pr in range(DSA_HEADS // 2):
        out = _dot(olat_scr[2 * pr], wuv_ref[2 * pr]) + _dot(olat_scr[2 * pr + 1], wuv_ref[2 * pr + 1])
        o_ref[0, :, pr * LANES:(pr + 1) * LANES] = out.astype(o_ref.dtype)


def _dsa_attention(qp, idx, lat, q_gain, k_gain, wuv_ext, rel_bias, tq=256):
    b, s, nq = qp.shape
    topk = min(TOPK_MAX, s // 4)
    nkb = s // tq
    n_idx = idx.shape[-1]
    tri = (jnp.arange(tq)[:, None] >= jnp.arange(tq)[None, :]).astype(BF16)
    kern = functools.partial(_dsa_kernel, tq=tq, topk=topk, thresholds=tuple(_t5_bucket_thresholds()))
    return pl.pallas_call(
        kern,
        grid=(b, s // tq),
        in_specs=[
            pl.BlockSpec(memory_space=pltpu.SMEM),
            pl.BlockSpec((1, tq, nq), lambda i, j: (i, j, 0)),
            pl.BlockSpec((1, tq, n_idx), lambda i, j: (i, j, 0)),
            pl.BlockSpec((1, s, LANES), lambda i, j: (i, 0, 0)),
            pl.BlockSpec((1, s, LANES), lambda i, j: (i, 0, IDX_HEADS * IDX_DIM // LANES)),
            pl.BlockSpec((1, DSA_LATENT), lambda i, j: (0, 0)),
            pl.BlockSpec((1, DSA_LATENT), lambda i, j: (0, 0)),
            pl.BlockSpec((DSA_HEADS, DSA_LATENT, LANES), lambda i, j: (0, 0, 0)),
            pl.BlockSpec((tq, tq), lambda i, j: (0, 0)),
        ],
        out_specs=pl.BlockSpec((1, tq, DSA_HEADS * DSA_V_DIM), lambda i, j: (i, j, 0)),
        out_shape=jax.ShapeDtypeStruct((b, s, DSA_HEADS * DSA_V_DIM), BF16),
        scratch_shapes=[
            pltpu.VMEM((s, DSA_LATENT), BF16),
            pltpu.VMEM((nkb, DSA_LATENT, tq), BF16),
            pltpu.VMEM((nkb, tq, tq), I32),
            pltpu.VMEM((nkb, tq, tq), F32),
            pltpu.VMEM((DSA_HEADS * tq, DSA_LATENT), BF16),
            pltpu.VMEM((DSA_HEADS, tq, DSA_LATENT), BF16),
            pltpu.VMEM((2, DSA_HEADS, tq, tq), F32),
        ],
        compiler_params=_cparams("arbitrary", "arbitrary"),
        name="dsa_attn",
    )(rel_bias, qp, idx, lat, idx, q_gain.reshape(1, -1), k_gain.reshape(1, -1), wuv_ext, tri)


def _dsa_weights(w_in, w_uv):
    o1 = DSA_HEADS * DSA_LATENT
    o2 = o1 + DSA_LATENT
    o3 = o2 + IDX_HEADS * IDX_DIM
    o4 = o3 + IDX_DIM
    d = w_in.shape[0]
    w_ki = w_in[:, o3:o4]
    w_wi = w_in[:, o4:]
    pad = jnp.zeros((d, LANES - IDX_HEADS), w_in.dtype)
    w = jnp.concatenate([w_in[:, :o3], w_ki, w_ki, w_wi, pad], axis=1).astype(BF16)
    zeros = jnp.zeros_like(w_uv)
    even = jnp.concatenate([w_uv, zeros], axis=-1)
    odd = jnp.concatenate([zeros, w_uv], axis=-1)
    is_even = (jnp.arange(DSA_HEADS) % 2 == 0)[:, None, None]
    wuv_ext = jnp.where(is_even, even, odd).astype(BF16)
    splits = (o1, DSA_LATENT, IDX_HEADS * IDX_DIM + 2 * LANES)
    return w, wuv_ext, splits


def kernel(x, c, ada_w, ada_b, norm_mix, norm_ffn, sb_w_in, sb_w_out, dsa_w_in, dsa_q_norm,
           dsa_k_norm, dsa_w_uv, dsa_w_out, rel_bias, ffn_w_up, ffn_conv_w, ffn_conv_b, ffn_w_down):
    depth = ada_w.shape[0]
    d = x.shape[-1]
    mod = _modulation(c, ada_w, ada_b)
    for i in range(depth):
        sh1, sc1, g1, sh2, sc2, g2 = [mod[i, :, k * d:(k + 1) * d] for k in range(6)]
        j = i // 2
        if i % 2 == 0:
            w_in = sb_w_in[j].astype(BF16)
            (qkv,) = _norm_proj(x, norm_mix[i], sc1, sh1, w_in, (w_in.shape[1],))
            o = _sb_attention(qkv)
            w_out = sb_w_out[j]
        else:
            w_in, wuv_ext, splits = _dsa_weights(dsa_w_in[j], dsa_w_uv[j])
            qp, lat, idx = _norm_proj(x, norm_mix[i], sc1, sh1, w_in, splits)
            o = _dsa_attention(qp, idx, lat, dsa_q_norm[j], dsa_k_norm[j], wuv_ext, rel_bias)
            w_out = dsa_w_out[j]
        x = _out_res(o, w_out.astype(BF16), x, g1)
        x = _ffn(x, norm_ffn[i], sc2, sh2, g2, ffn_w_up[i].astype(BF16), ffn_conv_w[i],
                 ffn_conv_b[i], ffn_w_down[i].astype(BF16))
    return x
```

```python
import functools
import math

import numpy as np
import jax
import jax.numpy as jnp
from jax import lax
from jax.experimental import pallas as pl
from jax.experimental.pallas import tpu as pltpu

F32 = jnp.float32
BF16 = jnp.bfloat16
I32 = jnp.int32

LANES = 128
SUBLANES = 8
VMEM_LIMIT = 56 * 1024 * 1024

RMS_EPS = 1e-6
NEG = -1e30
SB_EXIT = -104.0
INT_MIN = -(2 ** 31)

SB_HEADS = 16
SB_HEAD_DIM = 64
DSA_HEADS = 16
DSA_LATENT = 128
DSA_V_DIM = 64
DSA_GROUP = 4
IDX_HEADS = 8
IDX_DIM = 64
TOPK_MAX = 256
NUM_BUCKETS = 32
MAX_DISTANCE = 128
CONV_W = 3


def _cparams(*sem):
    return pltpu.CompilerParams(dimension_semantics=sem, vmem_limit_bytes=VMEM_LIMIT)


def _dot(a, b):
    return jnp.dot(a, b, preferred_element_type=F32)


def _dot_nt(a, b):
    return lax.dot_general(a, b, (((1,), (1,)), ((), ())), preferred_element_type=F32)


def _split_bf16(v):
    hi = v.astype(BF16)
    lo = (v - hi.astype(F32)).astype(BF16)
    return hi, lo


def _mod_kernel(c_ref, w_ref, b_ref, o_ref):
    c = c_ref[...]
    cond = c * (1.0 / (1.0 + jnp.exp(-c)))
    ch, cl = _split_bf16(cond)
    wh, wl = _split_bf16(w_ref[0])
    o_ref[0] = _dot(ch, wh) + _dot(ch, wl) + _dot(cl, wh) + b_ref[0]


def _modulation(c, ada_w, ada_b):
    depth, d, n = ada_w.shape
    b = c.shape[0]
    tn = 1024
    return pl.pallas_call(
        _mod_kernel,
        grid=(depth, n // tn),
        in_specs=[
            pl.BlockSpec((b, d), lambda i, j: (0, 0)),
            pl.BlockSpec((1, d, tn), lambda i, j: (i, 0, j)),
            pl.BlockSpec((1, 1, tn), lambda i, j: (i, 0, j)),
        ],
        out_specs=pl.BlockSpec((1, b, tn), lambda i, j: (i, 0, j)),
        out_shape=jax.ShapeDtypeStruct((depth, b, n), F32),
        compiler_params=_cparams("arbitrary", "arbitrary"),
        name="mod",
    )(c, ada_w, ada_b.reshape(depth, 1, n))


def _modulated_norm(x, g, sc, sh):
    ms = jnp.mean(x * x, axis=-1, keepdims=True)
    return (x * lax.rsqrt(ms + RMS_EPS) * g) * (1.0 + sc) + sh


def _norm_proj_kernel(x_ref, g_ref, sc_ref, sh_ref, w_ref, *o_refs, splits, chunk):
    hb = _modulated_norm(x_ref[0], g_ref[...], sc_ref[0], sh_ref[0]).astype(BF16)
    off = 0
    for o_ref, n in zip(o_refs, splits):
        for c0 in range(0, n, chunk):
            c1 = min(n, c0 + chunk)
            o_ref[0, :, c0:c1] = _dot(hb, w_ref[:, off + c0:off + c1]).astype(o_ref.dtype)
        off += n


def _norm_proj(x, g, sc, sh, w, splits, tm=512):
    b, s, d = x.shape
    n = w.shape[1]
    assert sum(splits) == n
    kern = functools.partial(_norm_proj_kernel, splits=tuple(splits), chunk=512)
    return pl.pallas_call(
        kern,
        grid=(b, s // tm),
        in_specs=[
            pl.BlockSpec((1, tm, d), lambda i, j: (i, j, 0)),
            pl.BlockSpec((1, d), lambda i, j: (0, 0)),
            pl.BlockSpec((1, 1, d), lambda i, j: (i, 0, 0)),
            pl.BlockSpec((1, 1, d), lambda i, j: (i, 0, 0)),
            pl.BlockSpec((d, n), lambda i, j: (0, 0)),
        ],
        out_specs=[pl.BlockSpec((1, tm, m), lambda i, j: (i, j, 0)) for m in splits],
        out_shape=[jax.ShapeDtypeStruct((b, s, m), BF16) for m in splits],
        compiler_params=_cparams("arbitrary", "arbitrary"),
        name="norm_proj",
    )(x, g.reshape(1, d), sc.reshape(b, 1, d), sh.reshape(b, 1, d), w)


def _out_res_kernel(o_ref, w_ref, x_ref, gate_ref, y_ref):
    y_ref[0] = x_ref[0] + gate_ref[0] * _dot(o_ref[0], w_ref[...])


def _out_res(o, w, x, gate, tm=512):
    b, s, d = x.shape
    k = o.shape[-1]
    return pl.pallas_call(
        _out_res_kernel,
        grid=(b, s // tm),
        in_specs=[
            pl.BlockSpec((1, tm, k), lambda i, j: (i, j, 0)),
            pl.BlockSpec((k, d), lambda i, j: (0, 0)),
            pl.BlockSpec((1, tm, d), lambda i, j: (i, j, 0)),
            pl.BlockSpec((1, 1, d), lambda i, j: (i, 0, 0)),
        ],
        out_specs=pl.BlockSpec((1, tm, d), lambda i, j: (i, j, 0)),
        out_shape=jax.ShapeDtypeStruct((b, s, d), F32),
        compiler_params=_cparams("arbitrary", "arbitrary"),
        name="out_res",
    )(o, w, x, gate.reshape(b, 1, d))


def _ffn_kernel(x_ref, g_ref, sc_ref, sh_ref, gate_ref, wu_ref, cw_ref, cb_ref, wd_ref, y_ref,
                act_scr, carry_scr, *, fc):
    si = pl.program_id(1)
    tm = act_scr.shape[0]
    f = act_scr.shape[1]
    nf = f // fc

    @pl.when(si == 0)
    def _():
        carry_scr[...] = jnp.zeros_like(carry_scr)

    hb = _modulated_norm(x_ref[0], g_ref[...], sc_ref[0], sh_ref[0]).astype(BF16)
    row = lax.broadcasted_iota(I32, (tm, 1), 0)

    def up(c):
        return (_dot(hb, wu_ref[:, c * fc:(c + 1) * fc]), _dot(hb, wu_ref[:, f + c * fc:f + (c + 1) * fc]))

    def conv(u, col0):
        prev = carry_scr[:, col0:col0 + fc]
        carry_scr[:, col0:col0 + fc] = u[tm - SUBLANES:, :]
        p1 = prev[SUBLANES - 1:SUBLANES, :]
        p2 = prev[SUBLANES - 2:SUBLANES - 1, :]
        u1 = jnp.where(row == 0, p1, pltpu.roll(u, 1, 0))
        u2 = jnp.where(row == 0, p2, jnp.where(row == 1, p1, pltpu.roll(u, 2, 0)))
        cw = cw_ref[:, col0:col0 + fc]
        return cb_ref[:, col0:col0 + fc] + u2 * cw[0:1, :] + u1 * cw[1:2, :] + u * cw[2:3, :]

    nxt = up(0)
    for c in range(nf):
        ug, uv = nxt
        if c + 1 < nf:
            nxt = up(c + 1)
        yg = conv(ug, c * fc)
        yv = conv(uv, f + c * fc)
        act_scr[:, c * fc:(c + 1) * fc] = (yg * (1.0 / (1.0 + jnp.exp(-yg))) * yv).astype(BF16)

    y_ref[0] = x_ref[0] + gate_ref[0] * _dot(act_scr[...], wd_ref[...])


def _ffn(x, g, sc, sh, gate, w_up, conv_w, conv_b, w_down, tm=256, fc=256):
    b, s, d = x.shape
    f = w_down.shape[0]
    kern = functools.partial(_ffn_kernel, fc=fc)
    vec = lambda a: a.reshape(b, 1, d)
    const = lambda i, j: (0, 0)
    return pl.pallas_call(
        kern,
        grid=(b, s // tm),
        in_specs=[
            pl.BlockSpec((1, tm, d), lambda i, j: (i, j, 0)),
            pl.BlockSpec((1, d), const),
            pl.BlockSpec((1, 1, d), lambda i, j: (i, 0, 0)),
            pl.BlockSpec((1, 1, d), lambda i, j: (i, 0, 0)),
            pl.BlockSpec((1, 1, d), lambda i, j: (i, 0, 0)),
            pl.BlockSpec((d, 2 * f), const),
            pl.BlockSpec((CONV_W, 2 * f), const),
            pl.BlockSpec((1, 2 * f), const),
            pl.BlockSpec((f, d), const),
        ],
        out_specs=pl.BlockSpec((1, tm, d), lambda i, j: (i, j, 0)),
        out_shape=jax.ShapeDtypeStruct((b, s, d), F32),
        scratch_shapes=[
            pltpu.VMEM((tm, f), BF16),
            pltpu.VMEM((SUBLANES, 2 * f), F32),
        ],
        compiler_params=_cparams("arbitrary", "arbitrary"),
        name="ffn",
    )(x, g.reshape(1, d), vec(sc), vec(sh), vec(gate), w_up, conv_w, conv_b.reshape(1, 2 * f), w_down)


def _sb_kernel(q_ref, k_ref, v_ref, tri_ref, o_ref, *, tq):
    qi = pl.program_id(2)
    lane = lax.broadcasted_iota(I32, (1, LANES), 1)
    halves = (lane < SB_HEAD_DIM, lane >= SB_HEAD_DIM)
    q2 = q_ref[0] * (SB_HEAD_DIM ** -0.5)
    zero = jnp.zeros_like(q2)
    qs = tuple(jnp.where(hm, q2, zero) for hm in halves)
    tri = tri_ref[...]
    row = lax.broadcasted_iota(I32, (tq, tq), 0)
    col = lax.broadcasted_iota(I32, (tq, tq), 1)
    strict = col < row

    def block(jb, carry, masked):
        acc, r_a, r_b = carry
        start = pl.multiple_of(jb * tq, tq)
        kb = k_ref[0, pl.ds(start, tq), :]
        vb = v_ref[0, pl.ds(start, tq), :]
        r_new = []
        for hm, qh, r in zip(halves, qs, (r_a, r_b)):
            z = _dot_nt(qh, kb)
            lk = -(jnp.maximum(z, 0.0) + jnp.log(1.0 + jnp.exp(-jnp.abs(z))))
            if masked:
                lk = jnp.where(strict, lk, 0.0)
            hi, lo = _split_bf16(lk)
            incl = _dot(hi, tri) + _dot(lo, tri)
            a = jnp.exp(z + incl + r)
            if masked:
                a = jnp.where(strict, a, 0.0)
            vh = jnp.where(hm, vb, jnp.zeros_like(vb))
            acc = acc + _dot(a.astype(BF16), vh)
            r_new.append(r + incl[:, 0:1])
        return acc, r_new[0], r_new[1]

    def live(carry):
        return jnp.max(jnp.maximum(carry[1], carry[2])) > SB_EXIT

    def step(state):
        i, carry, _ = state
        carry = block(qi - 1 - i, carry, False)
        return i + 1, carry, live(carry)

    init = (jnp.zeros((tq, LANES), F32), jnp.zeros((tq, 1), F32), jnp.zeros((tq, 1), F32))
    carry = block(qi, init, True)
    _, carry, _ = lax.while_loop(lambda st: (st[0] < qi) & st[2], step, (jnp.int32(0), carry, live(carry)))
    o_ref[0] = carry[0].astype(o_ref.dtype)


def _sb_attention(qkv, tq=256):
    b, s, n3 = qkv.shape
    n = n3 // 3
    npair = n // LANES
    tri = (jnp.arange(tq)[:, None] >= jnp.arange(tq)[None, :]).astype(BF16)
    kern = functools.partial(_sb_kernel, tq=tq)
    return pl.pallas_call(
        kern,
        grid=(b, npair, s // tq),
        in_specs=[
            pl.BlockSpec((1, tq, LANES), lambda i, p, j: (i, j, p)),
            pl.BlockSpec((1, s, LANES), lambda i, p, j: (i, 0, npair + p)),
            pl.BlockSpec((1, s, LANES), lambda i, p, j: (i, 0, 2 * npair + p)),
            pl.BlockSpec((tq, tq), lambda i, p, j: (0, 0)),
        ],
        out_specs=pl.BlockSpec((1, tq, LANES), lambda i, p, j: (i, j, p)),
        out_shape=jax.ShapeDtypeStruct((b, s, n), BF16),
        compiler_params=_cparams("arbitrary", "arbitrary", "arbitrary"),
        name="sb_attn",
    )(qkv, qkv, qkv, tri)


def _t5_bucket_thresholds():
    max_exact = NUM_BUCKETS // 2
    n = np.arange(0, 4 * MAX_DISTANCE, dtype=np.int32)
    nf = np.maximum(n, 1).astype(np.float32)
    large = max_exact + (np.log(nf / np.float32(max_exact)) / np.float32(math.log(MAX_DISTANCE / max_exact))
                         * np.float32(NUM_BUCKETS - max_exact)).astype(np.int32)
    large = np.minimum(large, NUM_BUCKETS - 1)
    bucket = np.where(n < max_exact, n, large)
    assert np.all(np.diff(bucket) >= 0) and bucket[-1] == NUM_BUCKETS - 1
    return [int(np.argmax(bucket >= bb)) for bb in range(max_exact + 1, NUM_BUCKETS)]


def _sortable_key(v):
    v = jnp.where(v == 0.0, 0.0, v)
    bits = lax.bitcast_convert_type(v, I32)
    return bits ^ ((bits >> 31) & 0x7FFFFFFF)


def _tree_sum(parts):
    parts = list(parts)
    while len(parts) > 1:
        parts = [parts[i] + parts[i + 1] if i + 1 < len(parts) else parts[i] for i in range(0, len(parts), 2)]
    return parts[0]


def _dsa_kernel(rb_ref, q_ref, qidx_ref, lat_ref, kidx_ref, qg_ref, kg_ref, wuv_ref, tri_ref,
                o_ref, kn_scr, vt_scr, key_scr, madd_scr, qn_scr, olat_scr, bias_scr,
                *, tq, topk, thresholds):
    bi = pl.program_id(0)
    qi = pl.program_id(1)
    nkb = vt_scr.shape[0]
    krow = lax.broadcasted_iota(I32, (tq, tq), 0)
    qcol = lax.broadcasted_iota(I32, (tq, tq), 1)
    qcol1 = lax.broadcasted_iota(I32, (1, tq), 1)

    @pl.when((bi == 0) & (qi == 0))
    def _():
        max_exact = NUM_BUCKETS // 2
        for near in range(2):
            dist = qcol - krow + near * tq
            large = jnp.full((tq, tq), max_exact, I32)
            for th in thresholds:
                large = large + jnp.where(dist >= th, 1, 0)
            bucket = jnp.where(dist < max_exact, jnp.maximum(dist, 0), large)

            def head_tile(h, _, near=near, bucket=bucket):
                far = rb_ref[NUM_BUCKETS - 1, h]
                tile = jnp.zeros((tq, tq), F32)
                for bb in range(NUM_BUCKETS - 1):
                    tile = jnp.where(bucket == bb, rb_ref[bb, h] - far, tile)
                bias_scr[near, h] = tile
                return 0

            lax.fori_loop(0, DSA_HEADS, head_tile, 0)

    @pl.when(qi == 0)
    def _():
        for jb in range(nkb):
            lat = lat_ref[0, jb * tq:(jb + 1) * tq, :].astype(F32)
            ms = jnp.mean(lat * lat, axis=-1, keepdims=True)
            kn_scr[jb * tq:(jb + 1) * tq, :] = (lat * lax.rsqrt(ms + RMS_EPS) * kg_ref[...]).astype(BF16)
            vt_scr[jb] = lat.T.astype(BF16)

    scale = DSA_LATENT ** -0.5
    for h in range(DSA_HEADS):
        qh = q_ref[0, :, h * DSA_LATENT:(h + 1) * DSA_LATENT].astype(F32)
        ms = jnp.mean(qh * qh, axis=-1, keepdims=True)
        qn_scr[h * tq:(h + 1) * tq, :] = (qh * lax.rsqrt(ms + RMS_EPS) * (qg_ref[...] * scale)).astype(BF16)

    lane = lax.broadcasted_iota(I32, (1, LANES), 1)
    halves = (lane < IDX_DIM, lane >= IDX_DIM)
    n_qi = IDX_HEADS * IDX_DIM
    wi = qidx_ref[0, :, n_qi + LANES:n_qi + 2 * LANES].astype(F32) * (IDX_HEADS ** -0.5)
    wit = wi.T
    qms = []
    for hh in range(IDX_HEADS):
        blk = qidx_ref[0, :, (hh // 2) * LANES:(hh // 2 + 1) * LANES]
        qms.append(jnp.where(halves[hh % 2], blk, jnp.zeros_like(blk)))

    def causal(jb):
        return krow + jb * tq <= qcol + qi * tq

    def score_block(jb, _):
        start = pl.multiple_of(jb * tq, tq)
        kb = kidx_ref[0, pl.ds(start, tq), :]
        isc = jnp.zeros((tq, tq), F32)
        for hh in range(IDX_HEADS):
            isc = isc + jnp.maximum(_dot_nt(kb, qms[hh]), 0.0) * wit[hh:hh + 1, :]
        key_scr[jb] = jnp.where(causal(jb), _sortable_key(isc), INT_MIN)
        return 0

    lax.fori_loop(0, qi + 1, score_block, 0)

    def count(pred):
        def body(jb, part):
            c = jnp.where(pred(key_scr[jb]), 1.0, 0.0)
            return part + _tree_sum(c[g * SUBLANES:(g + 1) * SUBLANES, :] for g in range(tq // SUBLANES))
        part = lax.fori_loop(0, qi + 1, body, jnp.zeros((SUBLANES, tq), F32))
        return jnp.sum(part, axis=0, keepdims=True)

    kf = float(topk)
    thr = jnp.where(count(lambda k: k >= 0) >= kf, 0, INT_MIN).astype(I32)

    def bisect(i, thr):
        cand = thr | lax.shift_left(jnp.int32(1), jnp.int32(30) - i)
        return jnp.where(count(lambda k: k >= cand) >= kf, cand, thr)

    thr = lax.fori_loop(0, 31, bisect, thr)
    thr = jnp.where(qcol1 + qi * tq + 1 <= topk, INT_MIN, thr)
    need = kf - count(lambda k: k > thr)

    tri = tri_ref[...]

    def mask_block(jb, eq_before):
        key = key_scr[jb]
        eq = key == thr
        rank = _dot(tri, jnp.where(eq, 1.0, 0.0).astype(BF16)) + eq_before
        sel = ((key > thr) | (eq & (rank <= need))) & causal(jb)
        madd_scr[jb] = jnp.where(sel, 0.0, NEG)
        return rank[tq - 1:tq, :]

    lax.fori_loop(0, qi + 1, mask_block, jnp.zeros((1, tq), F32))

    prev = jnp.maximum(qi - 1, 0)
    prev_pen = jnp.where(qi >= 1, 0.0, NEG)

    def head_group(g, _):
        hs = [g * DSA_GROUP + k for k in range(DSA_GROUP)]
        row0 = pl.multiple_of(g * (DSA_GROUP * tq), DSA_GROUP * tq)
        qg = qn_scr[pl.ds(row0, DSA_GROUP * tq), :]

        def block(jb, carries, extra):
            start = pl.multiple_of(jb * tq, tq)
            lg = _dot_nt(kn_scr[pl.ds(start, tq), :], qg)
            madd = madd_scr[jb]
            stats, ps = [], []
            for k, (m, l, _) in enumerate(carries):
                add = madd if extra is None else madd + extra(hs[k])
                lk = lg[:, k * tq:(k + 1) * tq] + add
                m_new = jnp.maximum(m, jnp.max(lk, axis=0, keepdims=True))
                alpha = jnp.exp(m - m_new)
                p = jnp.exp(lk - m_new)
                stats.append((m_new, alpha, alpha * l + jnp.sum(p, axis=0, keepdims=True)))
                ps.append(p.astype(BF16))
            pv = _dot(vt_scr[jb], jnp.concatenate(ps, axis=1))
            return tuple((m_new, l_new, alpha * acc + pv[:, k * tq:(k + 1) * tq])
                         for k, ((m_new, alpha, l_new), (_, _, acc)) in enumerate(zip(stats, carries)))

        init = (jnp.full((1, tq), 0.5 * NEG, F32), jnp.zeros((1, tq), F32),
                jnp.zeros((DSA_LATENT, tq), F32))
        carries = lax.fori_loop(0, prev, lambda jb, cs: block(jb, cs, None), (init,) * DSA_GROUP)
        carries = block(prev, carries, lambda h: bias_scr[1, h] + prev_pen)
        carries = block(qi, carries, lambda h: bias_scr[0, h])
        for h, (_, l, acc) in zip(hs, carries):
            olat_scr[h] = (acc * (1.0 / l)).T.astype(BF16)
        return 0

    lax.fori_loop(0, DSA_HEADS // DSA_GROUP, head_group, 0)

    for pr in range(DSA_HEADS // 2):
        out = _dot(olat_scr[2 * pr], wuv_ref[2 * pr]) + _dot(olat_scr[2 * pr + 1], wuv_ref[2 * pr + 1])
        o_ref[0, :, pr * LANES:(pr + 1) * LANES] = out.astype(o_ref.dtype)


def _dsa_attention(qp, idx, lat, q_gain, k_gain, wuv_ext, rel_bias, tq=256):
    b, s, nq = qp.shape
    topk = min(TOPK_MAX, s // 4)
    nkb = s // tq
    n_idx = idx.shape[-1]
    tri = (jnp.arange(tq)[:, None] >= jnp.arange(tq)[None, :]).astype(BF16)
    kern = functools.partial(_dsa_kernel, tq=tq, topk=topk, thresholds=tuple(_t5_bucket_thresholds()))
    return pl.pallas_call(
        kern,
        grid=(b, s // tq),
        in_specs=[
            pl.BlockSpec(memory_space=pltpu.SMEM),
            pl.BlockSpec((1, tq, nq), lambda i, j: (i, j, 0)),
            pl.BlockSpec((1, tq, n_idx), lambda i, j: (i, j, 0)),
            pl.BlockSpec((1, s, LANES), lambda i, j: (i, 0, 0)),
            pl.BlockSpec((1, s, LANES), lambda i, j: (i, 0, IDX_HEADS * IDX_DIM // LANES)),
            pl.BlockSpec((1, DSA_LATENT), lambda i, j: (0, 0)),
            pl.BlockSpec((1, DSA_LATENT), lambda i, j: (0, 0)),
            pl.BlockSpec((DSA_HEADS, DSA_LATENT, LANES), lambda i, j: (0, 0, 0)),
            pl.BlockSpec((tq, tq), lambda i, j: (0, 0)),
        ],
        out_specs=pl.BlockSpec((1, tq, DSA_HEADS * DSA_V_DIM), lambda i, j: (i, j, 0)),
        out_shape=jax.ShapeDtypeStruct((b, s, DSA_HEADS * DSA_V_DIM), BF16),
        scratch_shapes=[
            pltpu.VMEM((s, DSA_LATENT), BF16),
            pltpu.VMEM((nkb, DSA_LATENT, tq), BF16),
            pltpu.VMEM((nkb, tq, tq), I32),
            pltpu.VMEM((nkb, tq, tq), F32),
            pltpu.VMEM((DSA_HEADS * tq, DSA_LATENT), BF16),
            pltpu.VMEM((DSA_HEADS, tq, DSA_LATENT), BF16),
            pltpu.VMEM((2, DSA_HEADS, tq, tq), F32),
        ],
        compiler_params=_cparams("arbitrary", "arbitrary"),
        name="dsa_attn",
    )(rel_bias, qp, idx, lat, idx, q_gain.reshape(1, -1), k_gain.reshape(1, -1), wuv_ext, tri)


def _dsa_weights(w_in, w_uv):
    o1 = DSA_HEADS * DSA_LATENT
    o2 = o1 + DSA_LATENT
    o3 = o2 + IDX_HEADS * IDX_DIM
    o4 = o3 + IDX_DIM
    d = w_in.shape[0]
    w_ki = w_in[:, o3:o4]
    w_wi = w_in[:, o4:]
    pad = jnp.zeros((d, LANES - IDX_HEADS), w_in.dtype)
    w = jnp.concatenate([w_in[:, :o3], w_ki, w_ki, w_wi, pad], axis=1).astype(BF16)
    zeros = jnp.zeros_like(w_uv)
    even = jnp.concatenate([w_uv, zeros], axis=-1)
    odd = jnp.concatenate([zeros, w_uv], axis=-1)
    is_even = (jnp.arange(DSA_HEADS) % 2 == 0)[:, None, None]
    wuv_ext = jnp.where(is_even, even, odd).astype(BF16)
    splits = (o1, DSA_LATENT, IDX_HEADS * IDX_DIM + 2 * LANES)
    return w, wuv_ext, splits


def kernel(x, c, ada_w, ada_b, norm_mix, norm_ffn, sb_w_in, sb_w_out, dsa_w_in, dsa_q_norm,
           dsa_k_norm, dsa_w_uv, dsa_w_out, rel_bias, ffn_w_up, ffn_conv_w, ffn_conv_b, ffn_w_down):
    depth = ada_w.shape[0]
    d = x.shape[-1]
    mod = _modulation(c, ada_w, ada_b)
    for i in range(depth):
        sh1, sc1, g1, sh2, sc2, g2 = [mod[i, :, k * d:(k + 1) * d] for k in range(6)]
        j = i // 2
        if i % 2 == 0:
            w_in = sb_w_in[j].astype(BF16)
            (qkv,) = _norm_proj(x, norm_mix[i], sc1, sh1, w_in, (w_in.shape[1],))
            o = _sb_attention(qkv)
            w_out = sb_w_out[j]
        else:
            w_in, wuv_ext, splits = _dsa_weights(dsa_w_in[j], dsa_w_uv[j])
            qp, lat, idx = _norm_proj(x, norm_mix[i], sc1, sh1, w_in, splits)
            o = _dsa_attention(qp, idx, lat, dsa_q_norm[j], dsa_k_norm[j], wuv_ext, rel_bias)
            w_out = dsa_w_out[j]
        x = _out_res(o, w_out.astype(BF16), x, g1)
        x = _ffn(x, norm_ffn[i], sc2, sh2, g2, ffn_w_up[i].astype(BF16), ffn_conv_w[i],
                 ffn_conv_b[i], ffn_w_down[i].astype(BF16))
    return x
```

```python
import functools
import math

import numpy as np
import jax
import jax.numpy as jnp
from jax import lax
from jax.experimental import pallas as pl
from jax.experimental.pallas import tpu as pltpu

F32 = jnp.float32
BF16 = jnp.bfloat16
I32 = jnp.int32

LANES = 128
SUBLANES = 8
VMEM_LIMIT = 56 * 1024 * 1024

RMS_EPS = 1e-6
NEG = -1e30
SB_EXIT = -104.0
INT_MIN = -(2 ** 31)

SB_HEADS = 16
SB_HEAD_DIM = 64
DSA_HEADS = 16
DSA_LATENT = 128
DSA_V_DIM = 64
DSA_GROUP = 4
IDX_HEADS = 8
IDX_DIM = 64
TOPK_MAX = 256
NUM_BUCKETS = 32
MAX_DISTANCE = 128
CONV_W = 3


def _cparams(*sem):
    return pltpu.CompilerParams(dimension_semantics=sem, vmem_limit_bytes=VMEM_LIMIT)


def _dot(a, b):
    return jnp.dot(a, b, preferred_element_type=F32)


def _dot_nt(a, b):
    return lax.dot_general(a, b, (((1,), (1,)), ((), ())), preferred_element_type=F32)


def _split_bf16(v):
    hi = v.astype(BF16)
    lo = (v - hi.astype(F32)).astype(BF16)
    return hi, lo


def _mod_kernel(c_ref, w_ref, b_ref, o_ref):
    c = c_ref[...]
    cond = c * (1.0 / (1.0 + jnp.exp(-c)))
    ch, cl = _split_bf16(cond)
    wh, wl = _split_bf16(w_ref[0])
    o_ref[0] = _dot(ch, wh) + _dot(ch, wl) + _dot(cl, wh) + b_ref[0]


def _modulation(c, ada_w, ada_b):
    depth, d, n = ada_w.shape
    b = c.shape[0]
    tn = 1024
    return pl.pallas_call(
        _mod_kernel,
        grid=(depth, n // tn),
        in_specs=[
            pl.BlockSpec((b, d), lambda i, j: (0, 0)),
            pl.BlockSpec((1, d, tn), lambda i, j: (i, 0, j)),
            pl.BlockSpec((1, 1, tn), lambda i, j: (i, 0, j)),
        ],
        out_specs=pl.BlockSpec((1, b, tn), lambda i, j: (i, 0, j)),
        out_shape=jax.ShapeDtypeStruct((depth, b, n), F32),
        compiler_params=_cparams("arbitrary", "arbitrary"),
        name="mod",
    )(c, ada_w, ada_b.reshape(depth, 1, n))


def _modulated_norm(x, g, sc, sh):
    ms = jnp.mean(x * x, axis=-1, keepdims=True)
    return (x * lax.rsqrt(ms + RMS_EPS) * g) * (1.0 + sc) + sh


def _norm_proj_kernel(x_ref, g_ref, sc_ref, sh_ref, w_ref, *o_refs, splits, chunk):
    hb = _modulated_norm(x_ref[0], g_ref[...], sc_ref[0], sh_ref[0]).astype(BF16)
    off = 0
    for o_ref, n in zip(o_refs, splits):
        for c0 in range(0, n, chunk):
            c1 = min(n, c0 + chunk)
            o_ref[0, :, c0:c1] = _dot(hb, w_ref[:, off + c0:off + c1]).astype(o_ref.dtype)
        off += n


def _norm_proj(x, g, sc, sh, w, splits, tm=512):
    b, s, d = x.shape
    n = w.shape[1]
    assert sum(splits) == n
    kern = functools.partial(_norm_proj_kernel, splits=tuple(splits), chunk=512)
    return pl.pallas_call(
        kern,
        grid=(b, s // tm),
        in_specs=[
            pl.BlockSpec((1, tm, d), lambda i, j: (i, j, 0)),
            pl.BlockSpec((1, d), lambda i, j: (0, 0)),
            pl.BlockSpec((1, 1, d), lambda i, j: (i, 0, 0)),
            pl.BlockSpec((1, 1, d), lambda i, j: (i, 0, 0)),
            pl.BlockSpec((d, n), lambda i, j: (0, 0)),
        ],
        out_specs=[pl.BlockSpec((1, tm, m), lambda i, j: (i, j, 0)) for m in splits],
        out_shape=[jax.ShapeDtypeStruct((b, s, m), BF16) for m in splits],
        compiler_params=_cparams("arbitrary", "arbitrary"),
        name="norm_proj",
    )(x, g.reshape(1, d), sc.reshape(b, 1, d), sh.reshape(b, 1, d), w)


def _out_res_kernel(o_ref, w_ref, x_ref, gate_ref, y_ref):
    y_ref[0] = x_ref[0] + gate_ref[0] * _dot(o_ref[0], w_ref[...])


def _out_res(o, w, x, gate, tm=512):
    b, s, d = x.shape
    k = o.shape[-1]
    return pl.pallas_call(
        _out_res_kernel,
        grid=(b, s // tm),
        in_specs=[
            pl.BlockSpec((1, tm, k), lambda i, j: (i, j, 0)),
            pl.BlockSpec((k, d), lambda i, j: (0, 0)),
            pl.BlockSpec((1, tm, d), lambda i, j: (i, j, 0)),
            pl.BlockSpec((1, 1, d), lambda i, j: (i, 0, 0)),
        ],
        out_specs=pl.BlockSpec((1, tm, d), lambda i, j: (i, j, 0)),
        out_shape=jax.ShapeDtypeStruct((b, s, d), F32),
        compiler_params=_cparams("arbitrary", "arbitrary"),
        name="out_res",
    )(o, w, x, gate.reshape(b, 1, d))


def _ffn_kernel(x_ref, g_ref, sc_ref, sh_ref, gate_ref, wu_ref, cw_ref, cb_ref, wd_ref, y_ref,
                act_scr, carry_scr, *, fc):
    si = pl.program_id(1)
    tm = act_scr.shape[0]
    f = act_scr.shape[1]
    nf = f // fc

    @pl.when(si == 0)
    def _():
        carry_scr[...] = jnp.zeros_like(carry_scr)

    hb = _modulated_norm(x_ref[0], g_ref[...], sc_ref[0], sh_ref[0]).astype(BF16)
    row = lax.broadcasted_iota(I32, (tm, 1), 0)

    def up(c):
        return (_dot(hb, wu_ref[:, c * fc:(c + 1) * fc]), _dot(hb, wu_ref[:, f + c * fc:f + (c + 1) * fc]))

    def conv(u, col0):
        prev = carry_scr[:, col0:col0 + fc]
        carry_scr[:, col0:col0 + fc] = u[tm - SUBLANES:, :]
        p1 = prev[SUBLANES - 1:SUBLANES, :]
        p2 = prev[SUBLANES - 2:SUBLANES - 1, :]
        u1 = jnp.where(row == 0, p1, pltpu.roll(u, 1, 0))
        u2 = jnp.where(row == 0, p2, jnp.where(row == 1, p1, pltpu.roll(u, 2, 0)))
        cw = cw_ref[:, col0:col0 + fc]
        return cb_ref[:, col0:col0 + fc] + u2 * cw[0:1, :] + u1 * cw[1:2, :] + u * cw[2:3, :]

    nxt = up(0)
    for c in range(nf):
        ug, uv = nxt
        if c + 1 < nf:
            nxt = up(c + 1)
        yg = conv(ug, c * fc)
        yv = conv(uv, f + c * fc)
        act_scr[:, c * fc:(c + 1) * fc] = (yg * (1.0 / (1.0 + jnp.exp(-yg))) * yv).astype(BF16)

    y_ref[0] = x_ref[0] + gate_ref[0] * _dot(act_scr[...], wd_ref[...])


def _ffn(x, g, sc, sh, gate, w_up, conv_w, conv_b, w_down, tm=256, fc=256):
    b, s, d = x.shape
    f = w_down.shape[0]
    kern = functools.partial(_ffn_kernel, fc=fc)
    vec = lambda a: a.reshape(b, 1, d)
    const = lambda i, j: (0, 0)
    return pl.pallas_call(
        kern,
        grid=(b, s // tm),
        in_specs=[
            pl.BlockSpec((1, tm, d), lambda i, j: (i, j, 0)),
            pl.BlockSpec((1, d), const),
            pl.BlockSpec((1, 1, d), lambda i, j: (i, 0, 0)),
            pl.BlockSpec((1, 1, d), lambda i, j: (i, 0, 0)),
            pl.BlockSpec((1, 1, d), lambda i, j: (i, 0, 0)),
            pl.BlockSpec((d, 2 * f), const),
            pl.BlockSpec((CONV_W, 2 * f), const),
            pl.BlockSpec((1, 2 * f), const),
            pl.BlockSpec((f, d), const),
        ],
        out_specs=pl.BlockSpec((1, tm, d), lambda i, j: (i, j, 0)),
        out_shape=jax.ShapeDtypeStruct((b, s, d), F32),
        scratch_shapes=[
            pltpu.VMEM((tm, f), BF16),
            pltpu.VMEM((SUBLANES, 2 * f), F32),
        ],
        compiler_params=_cparams("arbitrary", "arbitrary"),
        name="ffn",
    )(x, g.reshape(1, d), vec(sc), vec(sh), vec(gate), w_up, conv_w, conv_b.reshape(1, 2 * f), w_down)


def _sb_kernel(q_ref, k_ref, v_ref, tri_ref, o_ref, *, tq):
    qi = pl.program_id(2)
    lane = lax.broadcasted_iota(I32, (1, LANES), 1)
    halves = (lane < SB_HEAD_DIM, lane >= SB_HEAD_DIM)
    q2 = q_ref[0] * (SB_HEAD_DIM ** -0.5)
    zero = jnp.zeros_like(q2)
    qstack = jnp.concatenate([jnp.where(hm, q2, zero) for hm in halves], axis=0)
    tri2 = tri_ref[...]
    row = lax.broadcasted_iota(I32, (2 * tq, tq), 0)
    col = lax.broadcasted_iota(I32, (2 * tq, tq), 1)
    strict = col < jnp.where(row >= tq, row - tq, row)

    def qk(jb):
        start = pl.multiple_of(jb * tq, tq)
        return _dot_nt(qstack, k_ref[0, pl.ds(start, tq), :])

    def log_keep(z, masked):
        lk = -(jnp.maximum(z, 0.0) + jnp.log(1.0 + jnp.exp(-jnp.abs(z))))
        return jnp.where(strict, lk, 0.0) if masked else lk

    def suffix(lk):
        hi, lo = _split_bf16(lk)
        return _dot(jnp.concatenate([hi, lo], axis=1), tri2)

    def weights(z, incl, r, masked):
        a = jnp.exp(z + incl + r)
        return jnp.where(strict, a, 0.0) if masked else a

    def av(a, jb):
        start = pl.multiple_of(jb * tq, tq)
        vb = v_ref[0, pl.ds(start, tq), :]
        vstack = jnp.concatenate([jnp.where(hm, vb, jnp.zeros_like(vb)) for hm in halves], axis=0)
        a2 = jnp.concatenate([a[:tq], a[tq:]], axis=1).astype(BF16)
        return _dot(a2, vstack)

    has_prev = qi >= 1
    jp = jnp.maximum(qi - 1, 0)
    z_d = qk(qi)
    z_p = qk(jp)
    incl_d = suffix(log_keep(z_d, True))
    incl_p = suffix(log_keep(z_p, False))
    r0 = jnp.zeros((2 * tq, 1), F32)
    acc = av(weights(z_d, incl_d, r0, True), qi)
    r1 = incl_d[:, 0:1]
    acc = acc + av(jnp.where(has_prev, weights(z_p, incl_p, r1, False), 0.0), jp)
    r2 = r1 + jnp.where(has_prev, incl_p[:, 0:1], 0.0)

    def live(r):
        return jnp.max(r) > SB_EXIT

    def step(state):
        i, acc, r, _ = state
        jb = qi - 2 - i
        z = qk(jb)
        incl = suffix(log_keep(z, False))
        acc = acc + av(weights(z, incl, r, False), jb)
        r = r + incl[:, 0:1]
        return i + 1, acc, r, live(r)

    _, acc, _, _ = lax.while_loop(lambda st: (st[0] < qi - 1) & st[3], step,
                                  (jnp.int32(0), acc, r2, live(r2)))
    o_ref[0] = acc.astype(o_ref.dtype)


def _sb_attention(qkv, tq=256):
    b, s, n3 = qkv.shape
    n = n3 // 3
    npair = n // LANES
    tri = (jnp.arange(tq)[:, None] >= jnp.arange(tq)[None, :]).astype(BF16)
    tri = jnp.concatenate([tri, tri], axis=0)
    kern = functools.partial(_sb_kernel, tq=tq)
    return pl.pallas_call(
        kern,
        grid=(b, npair, s // tq),
        in_specs=[
            pl.BlockSpec((1, tq, LANES), lambda i, p, j: (i, j, p)),
            pl.BlockSpec((1, s, LANES), lambda i, p, j: (i, 0, npair + p)),
            pl.BlockSpec((1, s, LANES), lambda i, p, j: (i, 0, 2 * npair + p)),
            pl.BlockSpec((2 * tq, tq), lambda i, p, j: (0, 0)),
        ],
        out_specs=pl.BlockSpec((1, tq, LANES), lambda i, p, j: (i, j, p)),
        out_shape=jax.ShapeDtypeStruct((b, s, n), BF16),
        compiler_params=_cparams("arbitrary", "arbitrary", "arbitrary"),
        name="sb_attn",
    )(qkv, qkv, qkv, tri)


def _t5_bucket_thresholds():
    max_exact = NUM_BUCKETS // 2
    n = np.arange(0, 4 * MAX_DISTANCE, dtype=np.int32)
    nf = np.maximum(n, 1).astype(np.float32)
    large = max_exact + (np.log(nf / np.float32(max_exact)) / np.float32(math.log(MAX_DISTANCE / max_exact))
                         * np.float32(NUM_BUCKETS - max_exact)).astype(np.int32)
    large = np.minimum(large, NUM_BUCKETS - 1)
    bucket = np.where(n < max_exact, n, large)
    assert np.all(np.diff(bucket) >= 0) and bucket[-1] == NUM_BUCKETS - 1
    return [int(np.argmax(bucket >= bb)) for bb in range(max_exact + 1, NUM_BUCKETS)]


def _sortable_key(v):
    v = jnp.where(v == 0.0, 0.0, v)
    bits = lax.bitcast_convert_type(v, I32)
    return bits ^ ((bits >> 31) & 0x7FFFFFFF)


def _tree_sum(parts):
    parts = list(parts)
    while len(parts) > 1:
        parts = [parts[i] + parts[i + 1] if i + 1 < len(parts) else parts[i] for i in range(0, len(parts), 2)]
    return parts[0]


def _dsa_kernel(rb_ref, q_ref, qidx_ref, lat_ref, kidx_ref, qg_ref, kg_ref, wuv_ref, tri_ref,
                o_ref, kn_scr, vt_scr, key_scr, madd_scr, qn_scr, olat_scr, bias_scr,
                *, tq, topk, thresholds):
    bi = pl.program_id(0)
    qi = pl.program_id(1)
    nkb = vt_scr.shape[0]
    krow = lax.broadcasted_iota(I32, (tq, tq), 0)
    qcol = lax.broadcasted_iota(I32, (tq, tq), 1)
    qcol1 = lax.broadcasted_iota(I32, (1, tq), 1)

    @pl.when((bi == 0) & (qi == 0))
    def _():
        max_exact = NUM_BUCKETS // 2
        for near in range(2):
            dist = qcol - krow + near * tq
            large = jnp.full((tq, tq), max_exact, I32)
            for th in thresholds:
                large = large + jnp.where(dist >= th, 1, 0)
            bucket = jnp.where(dist < max_exact, jnp.maximum(dist, 0), large)

            def head_tile(h, _, near=near, bucket=bucket):
                far = rb_ref[NUM_BUCKETS - 1, h]
                tile = jnp.zeros((tq, tq), F32)
                for bb in range(NUM_BUCKETS - 1):
                    tile = jnp.where(bucket == bb, rb_ref[bb, h] - far, tile)
                bias_scr[near, h] = tile
                return 0

            lax.fori_loop(0, DSA_HEADS, head_tile, 0)

    @pl.when(qi == 0)
    def _():
        for jb in range(nkb):
            lat = lat_ref[0, jb * tq:(jb + 1) * tq, :].astype(F32)
            ms = jnp.mean(lat * lat, axis=-1, keepdims=True)
            kn_scr[jb * tq:(jb + 1) * tq, :] = (lat * lax.rsqrt(ms + RMS_EPS) * kg_ref[...]).astype(BF16)
            vt_scr[jb] = lat.T.astype(BF16)

    scale = DSA_LATENT ** -0.5
    for h in range(DSA_HEADS):
        qh = q_ref[0, :, h * DSA_LATENT:(h + 1) * DSA_LATENT].astype(F32)
        ms = jnp.mean(qh * qh, axis=-1, keepdims=True)
        qn_scr[h * tq:(h + 1) * tq, :] = (qh * lax.rsqrt(ms + RMS_EPS) * (qg_ref[...] * scale)).astype(BF16)

    lane = lax.broadcasted_iota(I32, (1, LANES), 1)
    halves = (lane < IDX_DIM, lane >= IDX_DIM)
    n_qi = IDX_HEADS * IDX_DIM
    wi = qidx_ref[0, :, n_qi + LANES:n_qi + 2 * LANES].astype(F32) * (IDX_HEADS ** -0.5)
    wit = wi.T
    qms = []
    for hh in range(IDX_HEADS):
        blk = qidx_ref[0, :, (hh // 2) * LANES:(hh // 2 + 1) * LANES]
        qms.append(jnp.where(halves[hh % 2], blk, jnp.zeros_like(blk)))

    def causal(jb):
        return krow + jb * tq <= qcol + qi * tq

    def score_block(jb, _):
        start = pl.multiple_of(jb * tq, tq)
        kb = kidx_ref[0, pl.ds(start, tq), :]
        isc = jnp.zeros((tq, tq), F32)
        for hh in range(IDX_HEADS):
            isc = isc + jnp.maximum(_dot_nt(kb, qms[hh]), 0.0) * wit[hh:hh + 1, :]
        key_scr[jb] = jnp.where(causal(jb), _sortable_key(isc), INT_MIN)
        return 0

    lax.fori_loop(0, qi + 1, score_block, 0)

    def count(pred):
        def body(jb, part):
            c = jnp.where(pred(key_scr[jb]), 1.0, 0.0)
            return part + _tree_sum(c[g * SUBLANES:(g + 1) * SUBLANES, :] for g in range(tq // SUBLANES))
        part = lax.fori_loop(0, qi + 1, body, jnp.zeros((SUBLANES, tq), F32))
        return jnp.sum(part, axis=0, keepdims=True)

    kf = float(topk)
    thr = jnp.where(count(lambda k: k >= 0) >= kf, 0, INT_MIN).astype(I32)

    def bisect(i, thr):
        cand = thr | lax.shift_left(jnp.int32(1), jnp.int32(30) - i)
        return jnp.where(count(lambda k: k >= cand) >= kf, cand, thr)

    thr = lax.fori_loop(0, 31, bisect, thr)
    thr = jnp.where(qcol1 + qi * tq + 1 <= topk, INT_MIN, thr)
    need = kf - count(lambda k: k > thr)

    tri = tri_ref[...]

    def mask_block(jb, eq_before):
        key = key_scr[jb]
        eq = key == thr
        rank = _dot(tri, jnp.where(eq, 1.0, 0.0).astype(BF16)) + eq_before
        sel = ((key > thr) | (eq & (rank <= need))) & causal(jb)
        madd_scr[jb] = jnp.where(sel, 0.0, NEG)
        return rank[tq - 1:tq, :]

    lax.fori_loop(0, qi + 1, mask_block, jnp.zeros((1, tq), F32))

    prev = jnp.maximum(qi - 1, 0)
    prev_pen = jnp.where(qi >= 1, 0.0, NEG)

    def head_group(g, _):
        hs = [g * DSA_GROUP + k for k in range(DSA_GROUP)]
        row0 = pl.multiple_of(g * (DSA_GROUP * tq), DSA_GROUP * tq)
        qg = qn_scr[pl.ds(row0, DSA_GROUP * tq), :]

        def block(jb, carries, extra):
            start = pl.multiple_of(jb * tq, tq)
            lg = _dot_nt(kn_scr[pl.ds(start, tq), :], qg)
            madd = madd_scr[jb]
            stats, ps = [], []
            for k, (m, l, _) in enumerate(carries):
                add = madd if extra is None else madd + extra(hs[k])
                lk = lg[:, k * tq:(k + 1) * tq] + add
                m_new = jnp.maximum(m, jnp.max(lk, axis=0, keepdims=True))
                alpha = jnp.exp(m - m_new)
                p = jnp.exp(lk - m_new)
                stats.append((m_new, alpha, alpha * l + jnp.sum(p, axis=0, keepdims=True)))
                ps.append(p.astype(BF16))
            pv = _dot(vt_scr[jb], jnp.concatenate(ps, axis=1))
            return tuple((m_new, l_new, alpha * acc + pv[:, k * tq:(k + 1) * tq])
                         for k, ((m_new, alpha, l_new), (_, _, acc)) in enumerate(zip(stats, carries)))

        init = (jnp.full((1, tq), 0.5 * NEG, F32), jnp.zeros((1, tq), F32),
                jnp.zeros((DSA_LATENT, tq), F32))
        carries = lax.fori_loop(0, prev, lambda jb, cs: block(jb, cs, None), (init,) * DSA_GROUP)
        carries = block(prev, carries, lambda h: bias_scr[1, h] + prev_pen)
        carries = block(qi, carries, lambda h: bias_scr[0, h])
        for h, (_, l, acc) in zip(hs, carries):
            olat_scr[h] = (acc * (1.0 / l)).T.astype(BF16)
        return 0

    lax.fori_loop(0, DSA_HEADS // DSA_GROUP, head_group, 0)

    for pr in range(DSA_HEADS // 2):
        out = _dot(olat_scr[2 * pr], wuv_ref[2 * pr]) + _dot(olat_scr[2 * pr + 1], wuv_ref[2 * pr + 1])
        o_ref[0, :, pr * LANES:(pr + 1) * LANES] = out.astype(o_ref.dtype)


def _dsa_attention(qp, idx, lat, q_gain, k_gain, wuv_ext, rel_bias, tq=256):
    b, s, nq = qp.shape
    topk = min(TOPK_MAX, s // 4)
    nkb = s // tq
    n_idx = idx.shape[-1]
    tri = (jnp.arange(tq)[:, None] >= jnp.arange(tq)[None, :]).astype(BF16)
    kern = functools.partial(_dsa_kernel, tq=tq, topk=topk, thresholds=tuple(_t5_bucket_thresholds()))
    return pl.pallas_call(
        kern,
        grid=(b, s // tq),
        in_specs=[
            pl.BlockSpec(memory_space=pltpu.SMEM),
            pl.BlockSpec((1, tq, nq), lambda i, j: (i, j, 0)),
            pl.BlockSpec((1, tq, n_idx), lambda i, j: (i, j, 0)),
            pl.BlockSpec((1, s, LANES), lambda i, j: (i, 0, 0)),
            pl.BlockSpec((1, s, LANES), lambda i, j: (i, 0, IDX_HEADS * IDX_DIM // LANES)),
            pl.BlockSpec((1, DSA_LATENT), lambda i, j: (0, 0)),
            pl.BlockSpec((1, DSA_LATENT), lambda i, j: (0, 0)),
            pl.BlockSpec((DSA_HEADS, DSA_LATENT, LANES), lambda i, j: (0, 0, 0)),
            pl.BlockSpec((tq, tq), lambda i, j: (0, 0)),
        ],
        out_specs=pl.BlockSpec((1, tq, DSA_HEADS * DSA_V_DIM), lambda i, j: (i, j, 0)),
        out_shape=jax.ShapeDtypeStruct((b, s, DSA_HEADS * DSA_V_DIM), BF16),
        scratch_shapes=[
            pltpu.VMEM((s, DSA_LATENT), BF16),
            pltpu.VMEM((nkb, DSA_LATENT, tq), BF16),
            pltpu.VMEM((nkb, tq, tq), I32),
            pltpu.VMEM((nkb, tq, tq), F32),
            pltpu.VMEM((DSA_HEADS * tq, DSA_LATENT), BF16),
            pltpu.VMEM((DSA_HEADS, tq, DSA_LATENT), BF16),
            pltpu.VMEM((2, DSA_HEADS, tq, tq), F32),
        ],
        compiler_params=_cparams("arbitrary", "arbitrary"),
        name="dsa_attn",
    )(rel_bias, qp, idx, lat, idx, q_gain.reshape(1, -1), k_gain.reshape(1, -1), wuv_ext, tri)


def _dsa_weights(w_in, w_uv):
    o1 = DSA_HEADS * DSA_LATENT
    o2 = o1 + DSA_LATENT
    o3 = o2 + IDX_HEADS * IDX_DIM
    o4 = o3 + IDX_DIM
    d = w_in.shape[0]
    w_ki = w_in[:, o3:o4]
    w_wi = w_in[:, o4:]
    pad = jnp.zeros((d, LANES - IDX_HEADS), w_in.dtype)
    w = jnp.concatenate([w_in[:, :o3], w_ki, w_ki, w_wi, pad], axis=1).astype(BF16)
    zeros = jnp.zeros_like(w_uv)
    even = jnp.concatenate([w_uv, zeros], axis=-1)
    odd = jnp.concatenate([zeros, w_uv], axis=-1)
    is_even = (jnp.arange(DSA_HEADS) % 2 == 0)[:, None, None]
    wuv_ext = jnp.where(is_even, even, odd).astype(BF16)
    splits = (o1, DSA_LATENT, IDX_HEADS * IDX_DIM + 2 * LANES)
    return w, wuv_ext, splits


def kernel(x, c, ada_w, ada_b, norm_mix, norm_ffn, sb_w_in, sb_w_out, dsa_w_in, dsa_q_norm,
           dsa_k_norm, dsa_w_uv, dsa_w_out, rel_bias, ffn_w_up, ffn_conv_w, ffn_conv_b, ffn_w_down):
    depth = ada_w.shape[0]
    d = x.shape[-1]
    mod = _modulation(c, ada_w, ada_b)
    for i in range(depth):
        sh1, sc1, g1, sh2, sc2, g2 = [mod[i, :, k * d:(k + 1) * d] for k in range(6)]
        j = i // 2
        if i % 2 == 0:
            w_in = sb_w_in[j].astype(BF16)
            (qkv,) = _norm_proj(x, norm_mix[i], sc1, sh1, w_in, (w_in.shape[1],))
            o = _sb_attention(qkv)
            w_out = sb_w_out[j]
        else:
            w_in, wuv_ext, splits = _dsa_weights(dsa_w_in[j], dsa_w_uv[j])
            qp, lat, idx = _norm_proj(x, norm_mix[i], sc1, sh1, w_in, splits)
            o = _dsa_attention(qp, idx, lat, dsa_q_norm[j], dsa_k_norm[j], wuv_ext, rel_bias)
            w_out = dsa_w_out[j]
        x = _out_res(o, w_out.astype(BF16), x, g1)
        x = _ffn(x, norm_ffn[i], sc2, sh2, g2, ffn_w_up[i].astype(BF16), ffn_conv_w[i],
                 ffn_conv_b[i], ffn_w_down[i].astype(BF16))
    return x
```

```python
import functools
import math

import numpy as np
import jax
import jax.numpy as jnp
from jax import lax
from jax.experimental import pallas as pl
from jax.experimental.pallas import tpu as pltpu

F32 = jnp.float32
BF16 = jnp.bfloat16
I32 = jnp.int32

LANES = 128
SUBLANES = 8
VMEM_LIMIT = 56 * 1024 * 1024

RMS_EPS = 1e-6
NEG = -1e30
LOG2E = math.log2(math.e)
SB_EXIT = -104.0
INT_MIN = -(2 ** 31)

SB_HEADS = 16
SB_HEAD_DIM = 64
DSA_HEADS = 16
DSA_LATENT = 128
DSA_V_DIM = 64
V_PAD = 16
DSA_GROUP = 8
IDX_HEADS = 8
IDX_DIM = 64
TOPK_MAX = 256
NUM_BUCKETS = 32
MAX_DISTANCE = 128
CONV_W = 3


def _cparams(*sem):
    return pltpu.CompilerParams(dimension_semantics=sem, vmem_limit_bytes=VMEM_LIMIT)


def _dot(a, b):
    return jnp.dot(a, b, preferred_element_type=F32)


def _dot_nt(a, b):
    return lax.dot_general(a, b, (((1,), (1,)), ((), ())), preferred_element_type=F32)


def _split_bf16(v):
    hi = v.astype(BF16)
    lo = (v - hi.astype(F32)).astype(BF16)
    return hi, lo


def _mod_kernel(c_ref, w_ref, b_ref, o_ref):
    c = c_ref[...]
    cond = c * (1.0 / (1.0 + jnp.exp(-c)))
    ch, cl = _split_bf16(cond)
    wh, wl = _split_bf16(w_ref[0])
    o_ref[0] = _dot(ch, wh) + _dot(ch, wl) + _dot(cl, wh) + b_ref[0]


def _modulation(c, ada_w, ada_b):
    depth, d, n = ada_w.shape
    b = c.shape[0]
    tn = 1024
    return pl.pallas_call(
        _mod_kernel,
        grid=(depth, n // tn),
        in_specs=[
            pl.BlockSpec((b, d), lambda i, j: (0, 0)),
            pl.BlockSpec((1, d, tn), lambda i, j: (i, 0, j)),
            pl.BlockSpec((1, 1, tn), lambda i, j: (i, 0, j)),
        ],
        out_specs=pl.BlockSpec((1, b, tn), lambda i, j: (i, 0, j)),
        out_shape=jax.ShapeDtypeStruct((depth, b, n), F32),
        compiler_params=_cparams("arbitrary", "arbitrary"),
        name="mod",
    )(c, ada_w, ada_b.reshape(depth, 1, n))


def _modulated_norm(x, g, sc, sh):
    ms = jnp.mean(x * x, axis=-1, keepdims=True)
    return (x * lax.rsqrt(ms + RMS_EPS) * g) * (1.0 + sc) + sh


def _norm_proj_kernel(x_ref, g_ref, sc_ref, sh_ref, w_ref, *o_refs, splits, chunk):
    hb = _modulated_norm(x_ref[0], g_ref[...], sc_ref[0], sh_ref[0]).astype(BF16)
    off = 0
    for o_ref, n in zip(o_refs, splits):
        for c0 in range(0, n, chunk):
            c1 = min(n, c0 + chunk)
            o_ref[0, :, c0:c1] = _dot(hb, w_ref[:, off + c0:off + c1]).astype(o_ref.dtype)
        off += n


def _norm_proj(x, g, sc, sh, w, splits, tm=512):
    b, s, d = x.shape
    n = w.shape[1]
    assert sum(splits) == n
    kern = functools.partial(_norm_proj_kernel, splits=tuple(splits), chunk=512)
    return pl.pallas_call(
        kern,
        grid=(b, s // tm),
        in_specs=[
            pl.BlockSpec((1, tm, d), lambda i, j: (i, j, 0)),
            pl.BlockSpec((1, d), lambda i, j: (0, 0)),
            pl.BlockSpec((1, 1, d), lambda i, j: (i, 0, 0)),
            pl.BlockSpec((1, 1, d), lambda i, j: (i, 0, 0)),
            pl.BlockSpec((d, n), lambda i, j: (0, 0)),
        ],
        out_specs=[pl.BlockSpec((1, tm, m), lambda i, j: (i, j, 0)) for m in splits],
        out_shape=[jax.ShapeDtypeStruct((b, s, m), BF16) for m in splits],
        compiler_params=_cparams("arbitrary", "arbitrary"),
        name="norm_proj",
    )(x, g.reshape(1, d), sc.reshape(b, 1, d), sh.reshape(b, 1, d), w)


def _out_res_kernel(o_ref, w_ref, x_ref, gate_ref, y_ref):
    y_ref[0] = x_ref[0] + gate_ref[0] * _dot(o_ref[0], w_ref[...])


def _out_res(o, w, x, gate, tm=512):
    b, s, d = x.shape
    k = o.shape[-1]
    return pl.pallas_call(
        _out_res_kernel,
        grid=(b, s // tm),
        in_specs=[
            pl.BlockSpec((1, tm, k), lambda i, j: (i, j, 0)),
            pl.BlockSpec((k, d), lambda i, j: (0, 0)),
            pl.BlockSpec((1, tm, d), lambda i, j: (i, j, 0)),
            pl.BlockSpec((1, 1, d), lambda i, j: (i, 0, 0)),
        ],
        out_specs=pl.BlockSpec((1, tm, d), lambda i, j: (i, j, 0)),
        out_shape=jax.ShapeDtypeStruct((b, s, d), F32),
        compiler_params=_cparams("arbitrary", "arbitrary"),
        name="out_res",
    )(o, w, x, gate.reshape(b, 1, d))


def _ffn_kernel(x_ref, g_ref, sc_ref, sh_ref, gate_ref, wu_ref, cw_ref, cb_ref, wd_ref, y_ref,
                act_scr, carry_scr, *, fc):
    si = pl.program_id(1)
    tm = act_scr.shape[0]
    f = act_scr.shape[1]
    nf = f // fc

    @pl.when(si == 0)
    def _():
        carry_scr[...] = jnp.zeros_like(carry_scr)

    hb = _modulated_norm(x_ref[0], g_ref[...], sc_ref[0], sh_ref[0]).astype(BF16)
    row = lax.broadcasted_iota(I32, (tm, 1), 0)

    def up(c):
        return (_dot(hb, wu_ref[:, c * fc:(c + 1) * fc]), _dot(hb, wu_ref[:, f + c * fc:f + (c + 1) * fc]))

    def conv(u, col0):
        prev = carry_scr[:, col0:col0 + fc]
        carry_scr[:, col0:col0 + fc] = u[tm - SUBLANES:, :]
        p1 = prev[SUBLANES - 1:SUBLANES, :]
        p2 = prev[SUBLANES - 2:SUBLANES - 1, :]
        u1 = jnp.where(row == 0, p1, pltpu.roll(u, 1, 0))
        u2 = jnp.where(row == 0, p2, jnp.where(row == 1, p1, pltpu.roll(u, 2, 0)))
        cw = cw_ref[:, col0:col0 + fc]
        return cb_ref[:, col0:col0 + fc] + u2 * cw[0:1, :] + u1 * cw[1:2, :] + u * cw[2:3, :]

    nxt = up(0)
    for c in range(nf):
        ug, uv = nxt
        if c + 1 < nf:
            nxt = up(c + 1)
        yg = conv(ug, c * fc)
        yv = conv(uv, f + c * fc)
        act_scr[:, c * fc:(c + 1) * fc] = (yg * (1.0 / (1.0 + jnp.exp(-yg))) * yv).astype(BF16)

    y_ref[0] = x_ref[0] + gate_ref[0] * _dot(act_scr[...], wd_ref[...])


def _ffn(x, g, sc, sh, gate, w_up, conv_w, conv_b, w_down, tm=256, fc=256):
    b, s, d = x.shape
    f = w_down.shape[0]
    kern = functools.partial(_ffn_kernel, fc=fc)
    vec = lambda a: a.reshape(b, 1, d)
    const = lambda i, j: (0, 0)
    return pl.pallas_call(
        kern,
        grid=(b, s // tm),
        in_specs=[
            pl.BlockSpec((1, tm, d), lambda i, j: (i, j, 0)),
            pl.BlockSpec((1, d), const),
            pl.BlockSpec((1, 1, d), lambda i, j: (i, 0, 0)),
            pl.BlockSpec((1, 1, d), lambda i, j: (i, 0, 0)),
            pl.BlockSpec((1, 1, d), lambda i, j: (i, 0, 0)),
            pl.BlockSpec((d, 2 * f), const),
            pl.BlockSpec((CONV_W, 2 * f), const),
            pl.BlockSpec((1, 2 * f), const),
            pl.BlockSpec((f, d), const),
        ],
        out_specs=pl.BlockSpec((1, tm, d), lambda i, j: (i, j, 0)),
        out_shape=jax.ShapeDtypeStruct((b, s, d), F32),
        scratch_shapes=[
            pltpu.VMEM((tm, f), BF16),
            pltpu.VMEM((SUBLANES, 2 * f), F32),
        ],
        compiler_params=_cparams("arbitrary", "arbitrary"),
        name="ffn",
    )(x, g.reshape(1, d), vec(sc), vec(sh), vec(gate), w_up, conv_w, conv_b.reshape(1, 2 * f), w_down)


def _sb_kernel(q_ref, k_ref, v_ref, tri_ref, o_ref, *, tq):
    qi = pl.program_id(2)
    lane = lax.broadcasted_iota(I32, (1, LANES), 1)
    halves = (lane < SB_HEAD_DIM, lane >= SB_HEAD_DIM)
    q2 = q_ref[0] * (SB_HEAD_DIM ** -0.5)
    zero = jnp.zeros_like(q2)
    qstack = jnp.concatenate([jnp.where(hm, q2, zero) for hm in halves], axis=0)
    tri2 = tri_ref[...]
    row = lax.broadcasted_iota(I32, (2 * tq, tq), 0)
    col = lax.broadcasted_iota(I32, (2 * tq, tq), 1)
    strict = col < jnp.where(row >= tq, row - tq, row)

    def qk(jb):
        start = pl.multiple_of(jb * tq, tq)
        return _dot_nt(qstack, k_ref[0, pl.ds(start, tq), :])

    def log_keep(z, masked):
        lk = -(jnp.maximum(z, 0.0) + jnp.log(1.0 + jnp.exp(-jnp.abs(z))))
        return jnp.where(strict, lk, 0.0) if masked else lk

    def suffix(lk):
        hi, lo = _split_bf16(lk)
        return _dot(jnp.concatenate([hi, lo], axis=1), tri2)

    def weights(z, incl, r, masked):
        a = jnp.exp(z + incl + r)
        return jnp.where(strict, a, 0.0) if masked else a

    def av(a, jb):
        start = pl.multiple_of(jb * tq, tq)
        vb = v_ref[0, pl.ds(start, tq), :]
        vstack = jnp.concatenate([jnp.where(hm, vb, jnp.zeros_like(vb)) for hm in halves], axis=0)
        a2 = jnp.concatenate([a[:tq], a[tq:]], axis=1).astype(BF16)
        return _dot(a2, vstack)

    has_prev = qi >= 1
    jp = jnp.maximum(qi - 1, 0)
    z_d = qk(qi)
    z_p = qk(jp)
    incl_d = suffix(log_keep(z_d, True))
    incl_p = suffix(log_keep(z_p, False))
    r0 = jnp.zeros((2 * tq, 1), F32)
    acc = av(weights(z_d, incl_d, r0, True), qi)
    r1 = incl_d[:, 0:1]
    acc = acc + av(jnp.where(has_prev, weights(z_p, incl_p, r1, False), 0.0), jp)
    r2 = r1 + jnp.where(has_prev, incl_p[:, 0:1], 0.0)

    def live(r):
        return jnp.max(r) > SB_EXIT

    def step(state):
        i, acc, r, _ = state
        jb = qi - 2 - i
        z = qk(jb)
        incl = suffix(log_keep(z, False))
        acc = acc + av(weights(z, incl, r, False), jb)
        r = r + incl[:, 0:1]
        return i + 1, acc, r, live(r)

    _, acc, _, _ = lax.while_loop(lambda st: (st[0] < qi - 1) & st[3], step,
                                  (jnp.int32(0), acc, r2, live(r2)))
    o_ref[0] = acc.astype(o_ref.dtype)


def _sb_attention(qkv, tq=256):
    b, s, n3 = qkv.shape
    n = n3 // 3
    npair = n // LANES
    tri = (jnp.arange(tq)[:, None] >= jnp.arange(tq)[None, :]).astype(BF16)
    tri = jnp.concatenate([tri, tri], axis=0)
    kern = functools.partial(_sb_kernel, tq=tq)
    return pl.pallas_call(
        kern,
        grid=(b, npair, s // tq),
        in_specs=[
            pl.BlockSpec((1, tq, LANES), lambda i, p, j: (i, j, p)),
            pl.BlockSpec((1, s, LANES), lambda i, p, j: (i, 0, npair + p)),
            pl.BlockSpec((1, s, LANES), lambda i, p, j: (i, 0, 2 * npair + p)),
            pl.BlockSpec((2 * tq, tq), lambda i, p, j: (0, 0)),
        ],
        out_specs=pl.BlockSpec((1, tq, LANES), lambda i, p, j: (i, j, p)),
        out_shape=jax.ShapeDtypeStruct((b, s, n), BF16),
        compiler_params=_cparams("arbitrary", "arbitrary", "arbitrary"),
        name="sb_attn",
    )(qkv, qkv, qkv, tri)


def _t5_bucket_thresholds():
    max_exact = NUM_BUCKETS // 2
    n = np.arange(0, 4 * MAX_DISTANCE, dtype=np.int32)
    nf = np.maximum(n, 1).astype(np.float32)
    large = max_exact + (np.log(nf / np.float32(max_exact)) / np.float32(math.log(MAX_DISTANCE / max_exact))
                         * np.float32(NUM_BUCKETS - max_exact)).astype(np.int32)
    large = np.minimum(large, NUM_BUCKETS - 1)
    bucket = np.where(n < max_exact, n, large)
    assert np.all(np.diff(bucket) >= 0) and bucket[-1] == NUM_BUCKETS - 1
    return [int(np.argmax(bucket >= bb)) for bb in range(max_exact + 1, NUM_BUCKETS)]


def _sortable_key(v):
    v = jnp.where(v == 0.0, 0.0, v)
    bits = lax.bitcast_convert_type(v, I32)
    return bits ^ ((bits >> 31) & 0x7FFFFFFF)


def _tree_sum(parts):
    parts = list(parts)
    while len(parts) > 1:
        parts = [parts[i] + parts[i + 1] if i + 1 < len(parts) else parts[i] for i in range(0, len(parts), 2)]
    return parts[0]


def _dsa_kernel(rb_ref, q_ref, qidx_ref, lat_ref, kidx_ref, qg_ref, kg_ref, wuv_ref, tri_ref,
                o_ref, kn_scr, vt_scr, key_scr, add_scr, qn_scr, olat_scr, bias_scr,
                lg_a, lg_b, p_a, p_b, acc_scr, *, tq, topk, thresholds):
    bi = pl.program_id(0)
    qi = pl.program_id(1)
    nkb = vt_scr.shape[0]
    krow = lax.broadcasted_iota(I32, (tq, tq), 0)
    qcol = lax.broadcasted_iota(I32, (tq, tq), 1)
    qcol1 = lax.broadcasted_iota(I32, (1, tq), 1)

    @pl.when((bi == 0) & (qi == 0))
    def _():
        add_scr[nkb + 2 * DSA_HEADS] = jnp.full((tq, tq), NEG, F32)
        max_exact = NUM_BUCKETS // 2
        for near in range(2):
            dist = qcol - krow + near * tq
            large = jnp.full((tq, tq), max_exact, I32)
            for th in thresholds:
                large = large + jnp.where(dist >= th, 1, 0)
            bucket = jnp.where(dist < max_exact, jnp.maximum(dist, 0), large)

            def head_tile(h, _, near=near, bucket=bucket):
                far = rb_ref[NUM_BUCKETS - 1, h]
                tile = jnp.zeros((tq, tq), F32)
                for bb in range(NUM_BUCKETS - 1):
                    tile = jnp.where(bucket == bb, (rb_ref[bb, h] - far) * LOG2E, tile)
                bias_scr[near, h] = tile
                return 0

            lax.fori_loop(0, DSA_HEADS, head_tile, 0)

    @pl.when(qi == 0)
    def _():
        ext_row = lax.broadcasted_iota(I32, (V_PAD, tq), 0)
        for jb in range(nkb):
            lat = lat_ref[0, jb * tq:(jb + 1) * tq, :].astype(F32)
            ms = jnp.mean(lat * lat, axis=-1, keepdims=True)
            kn_scr[jb * tq:(jb + 1) * tq, :] = (lat * lax.rsqrt(ms + RMS_EPS) * kg_ref[...]).astype(BF16)
            vt_scr[jb, :DSA_LATENT, :] = lat.T.astype(BF16)
            vt_scr[jb, DSA_LATENT:, :] = jnp.where(ext_row == 0, 1.0, 0.0).astype(BF16)

    scale = DSA_LATENT ** -0.5 * LOG2E
    for h in range(DSA_HEADS):
        qh = q_ref[0, :, h * DSA_LATENT:(h + 1) * DSA_LATENT].astype(F32)
        ms = jnp.mean(qh * qh, axis=-1, keepdims=True)
        qn_scr[h * tq:(h + 1) * tq, :] = (qh * lax.rsqrt(ms + RMS_EPS) * (qg_ref[...] * scale)).astype(BF16)

    lane = lax.broadcasted_iota(I32, (1, LANES), 1)
    halves = (lane < IDX_DIM, lane >= IDX_DIM)
    n_qi = IDX_HEADS * IDX_DIM
    wi = qidx_ref[0, :, n_qi + LANES:n_qi + 2 * LANES].astype(F32) * (IDX_HEADS ** -0.5)
    wit = wi.T
    qms = []
    for hh in range(IDX_HEADS):
        blk = qidx_ref[0, :, (hh // 2) * LANES:(hh // 2 + 1) * LANES]
        qms.append(jnp.where(halves[hh % 2], blk, jnp.zeros_like(blk)))

    def causal(jb):
        return krow + jb * tq <= qcol + qi * tq

    def score_block(jb, _):
        start = pl.multiple_of(jb * tq, tq)
        kb = kidx_ref[0, pl.ds(start, tq), :]
        isc = jnp.zeros((tq, tq), F32)
        for hh in range(IDX_HEADS):
            isc = isc + jnp.maximum(_dot_nt(kb, qms[hh]), 0.0) * wit[hh:hh + 1, :]
        key_scr[jb] = jnp.where(causal(jb), _sortable_key(isc), INT_MIN)
        return 0

    lax.fori_loop(0, qi + 1, score_block, 0)

    def count(pred):
        def body(jb, part):
            c = jnp.where(pred(key_scr[jb]), 1.0, 0.0)
            return part + _tree_sum(c[g * SUBLANES:(g + 1) * SUBLANES, :] for g in range(tq // SUBLANES))
        part = lax.fori_loop(0, qi + 1, body, jnp.zeros((SUBLANES, tq), F32))
        return jnp.sum(part, axis=0, keepdims=True)

    kf = float(topk)
    thr = jnp.where(count(lambda k: k >= 0) >= kf, 0, INT_MIN).astype(I32)

    def bisect(i, thr):
        cand = thr | lax.shift_left(jnp.int32(1), jnp.int32(30) - i)
        return jnp.where(count(lambda k: k >= cand) >= kf, cand, thr)

    thr = lax.fori_loop(0, 31, bisect, thr)
    thr = jnp.where(qcol1 + qi * tq + 1 <= topk, INT_MIN, thr)
    need = kf - count(lambda k: k > thr)

    tri = tri_ref[...]

    def mask_block(jb, eq_before):
        key = key_scr[jb]
        eq = key == thr
        rank = _dot(tri, jnp.where(eq, 1.0, 0.0).astype(BF16)) + eq_before
        sel = ((key > thr) | (eq & (rank <= need))) & causal(jb)
        add_scr[jb] = jnp.where(sel, 0.0, NEG)
        return rank[tq - 1:tq, :]

    lax.fori_loop(0, qi + 1, mask_block, jnp.zeros((1, tq), F32))

    prev = jnp.maximum(qi - 1, 0)
    prev_pen = jnp.where(qi >= 1, 0.0, NEG)
    nblk = prev + 2

    def near_tiles(h, _):
        add_scr[nkb + h] = add_scr[prev] + bias_scr[1, h] + prev_pen
        add_scr[nkb + DSA_HEADS + h] = add_scr[qi] + bias_scr[0, h]
        return 0

    lax.fori_loop(0, DSA_HEADS, near_tiles, 0)

    def kv_block(t):
        t = jnp.maximum(t, 0)
        return jnp.minimum(jnp.where(t < prev, t, jnp.where(t == prev, prev, qi)), qi)

    def add_index(t, h):
        return jnp.where(t < prev, t,
                         jnp.where(t == prev, nkb + h,
                                   jnp.where(t == prev + 1, nkb + DSA_HEADS + h, nkb + 2 * DSA_HEADS)))

    gw = DSA_GROUP * tq

    def head_group(g, _):
        row0 = pl.multiple_of(g * gw, gw)
        qg = qn_scr[pl.ds(row0, gw), :]

        def stage_qk(t, lg_ref):
            start = pl.multiple_of(kv_block(t) * tq, tq)
            lg_ref[...] = _dot_nt(kn_scr[pl.ds(start, tq), :], qg)

        def stage_softmax(t, lg_ref, p_ref, stats):
            new_stats, alphas = [], []
            for k, m in enumerate(stats):
                lk = lg_ref[:, k * tq:(k + 1) * tq] + add_scr[add_index(t, g * DSA_GROUP + k)]
                m_new = jnp.maximum(m, jnp.max(lk, axis=0, keepdims=True))
                alphas.append(jnp.exp2(m - m_new))
                p_ref[:, k * tq:(k + 1) * tq] = jnp.exp2(lk - m_new).astype(BF16)
                new_stats.append(m_new)
            return tuple(new_stats), tuple(alphas)

        def stage_pv(t, p_ref, alphas):
            pv = _dot(vt_scr[kv_block(t)], p_ref[...])
            for k, alpha in enumerate(alphas):
                acc_scr[k] = alpha * acc_scr[k] + pv[:, k * tq:(k + 1) * tq]

        stats = (jnp.full((1, tq), 0.5 * NEG, F32),) * DSA_GROUP
        ones = (jnp.ones((1, tq), F32),) * DSA_GROUP
        acc_scr[...] = jnp.zeros_like(acc_scr)
        p_b[...] = jnp.zeros_like(p_b)
        stage_qk(0, lg_a)

        def pair(u, carry):
            stats, alphas = carry
            t0 = 2 * u
            stage_qk(t0 + 1, lg_b)
            stage_pv(t0 - 1, p_b, alphas)
            stats, alphas = stage_softmax(t0, lg_a, p_a, stats)
            stage_qk(t0 + 2, lg_a)
            stage_pv(t0, p_a, alphas)
            stats, alphas = stage_softmax(t0 + 1, lg_b, p_b, stats)
            return stats, alphas

        npair = (nblk + 1) // 2
        stats, alphas = lax.fori_loop(0, npair, pair, (stats, ones))
        stage_pv(2 * npair - 1, p_b, alphas)
        for k in range(DSA_GROUP):
            acc = acc_scr[k]
            l = acc[DSA_LATENT:DSA_LATENT + 1, :]
            olat_scr[g * DSA_GROUP + k] = (acc[:DSA_LATENT, :] * (1.0 / l)).T.astype(BF16)
        return 0

    lax.fori_loop(0, DSA_HEADS // DSA_GROUP, head_group, 0)

    for pr in range(DSA_HEADS // 2):
        out = _dot(olat_scr[2 * pr], wuv_ref[2 * pr]) + _dot(olat_scr[2 * pr + 1], wuv_ref[2 * pr + 1])
        o_ref[0, :, pr * LANES:(pr + 1) * LANES] = out.astype(o_ref.dtype)


def _dsa_attention(qp, idx, lat, q_gain, k_gain, wuv_ext, rel_bias, tq=256):
    b, s, nq = qp.shape
    topk = min(TOPK_MAX, s // 4)
    nkb = s // tq
    n_idx = idx.shape[-1]
    tri = (jnp.arange(tq)[:, None] >= jnp.arange(tq)[None, :]).astype(BF16)
    kern = functools.partial(_dsa_kernel, tq=tq, topk=topk, thresholds=tuple(_t5_bucket_thresholds()))
    return pl.pallas_call(
        kern,
        grid=(b, s // tq),
        in_specs=[
            pl.BlockSpec(memory_space=pltpu.SMEM),
            pl.BlockSpec((1, tq, nq), lambda i, j: (i, j, 0)),
            pl.BlockSpec((1, tq, n_idx), lambda i, j: (i, j, 0)),
            pl.BlockSpec((1, s, LANES), lambda i, j: (i, 0, 0)),
            pl.BlockSpec((1, s, LANES), lambda i, j: (i, 0, IDX_HEADS * IDX_DIM // LANES)),
            pl.BlockSpec((1, DSA_LATENT), lambda i, j: (0, 0)),
            pl.BlockSpec((1, DSA_LATENT), lambda i, j: (0, 0)),
            pl.BlockSpec((DSA_HEADS, DSA_LATENT, LANES), lambda i, j: (0, 0, 0)),
            pl.BlockSpec((tq, tq), lambda i, j: (0, 0)),
        ],
        out_specs=pl.BlockSpec((1, tq, DSA_HEADS * DSA_V_DIM), lambda i, j: (i, j, 0)),
        out_shape=jax.ShapeDtypeStruct((b, s, DSA_HEADS * DSA_V_DIM), BF16),
        scratch_shapes=[
            pltpu.VMEM((s, DSA_LATENT), BF16),
            pltpu.VMEM((nkb, DSA_LATENT + V_PAD, tq), BF16),
            pltpu.VMEM((nkb, tq, tq), I32),
            pltpu.VMEM((nkb + 2 * DSA_HEADS + 1, tq, tq), F32),
            pltpu.VMEM((DSA_HEADS * tq, DSA_LATENT), BF16),
            pltpu.VMEM((DSA_HEADS, tq, DSA_LATENT), BF16),
            pltpu.VMEM((2, DSA_HEADS, tq, tq), F32),
            pltpu.VMEM((tq, DSA_GROUP * tq), F32),
            pltpu.VMEM((tq, DSA_GROUP * tq), F32),
            pltpu.VMEM((tq, DSA_GROUP * tq), BF16),
            pltpu.VMEM((tq, DSA_GROUP * tq), BF16),
            pltpu.VMEM((DSA_GROUP, DSA_LATENT + V_PAD, tq), F32),
        ],
        compiler_params=_cparams("arbitrary", "arbitrary"),
        name="dsa_attn",
    )(rel_bias, qp, idx, lat, idx, q_gain.reshape(1, -1), k_gain.reshape(1, -1), wuv_ext, tri)


def _dsa_weights(w_in, w_uv):
    o1 = DSA_HEADS * DSA_LATENT
    o2 = o1 + DSA_LATENT
    o3 = o2 + IDX_HEADS * IDX_DIM
    o4 = o3 + IDX_DIM
    d = w_in.shape[0]
    w_ki = w_in[:, o3:o4]
    w_wi = w_in[:, o4:]
    pad = jnp.zeros((d, LANES - IDX_HEADS), w_in.dtype)
    w = jnp.concatenate([w_in[:, :o3], w_ki, w_ki, w_wi, pad], axis=1).astype(BF16)
    zeros = jnp.zeros_like(w_uv)
    even = jnp.concatenate([w_uv, zeros], axis=-1)
    odd = jnp.concatenate([zeros, w_uv], axis=-1)
    is_even = (jnp.arange(DSA_HEADS) % 2 == 0)[:, None, None]
    wuv_ext = jnp.where(is_even, even, odd).astype(BF16)
    splits = (o1, DSA_LATENT, IDX_HEADS * IDX_DIM + 2 * LANES)
    return w, wuv_ext, splits


def kernel(x, c, ada_w, ada_b, norm_mix, norm_ffn, sb_w_in, sb_w_out, dsa_w_in, dsa_q_norm,
           dsa_k_norm, dsa_w_uv, dsa_w_out, rel_bias, ffn_w_up, ffn_conv_w, ffn_conv_b, ffn_w_down):
    depth = ada_w.shape[0]
    d = x.shape[-1]
    mod = _modulation(c, ada_w, ada_b)
    for i in range(depth):
        sh1, sc1, g1, sh2, sc2, g2 = [mod[i, :, k * d:(k + 1) * d] for k in range(6)]
        j = i // 2
        if i % 2 == 0:
            w_in = sb_w_in[j].astype(BF16)
            (qkv,) = _norm_proj(x, norm_mix[i], sc1, sh1, w_in, (w_in.shape[1],))
            o = _sb_attention(qkv)
            w_out = sb_w_out[j]
        else:
            w_in, wuv_ext, splits = _dsa_weights(dsa_w_in[j], dsa_w_uv[j])
            qp, lat, idx = _norm_proj(x, norm_mix[i], sc1, sh1, w_in, splits)
            o = _dsa_attention(qp, idx, lat, dsa_q_norm[j], dsa_k_norm[j], wuv_ext, rel_bias)
            w_out = dsa_w_out[j]
        x = _out_res(o, w_out.astype(BF16), x, g1)
        x = _ffn(x, norm_ffn[i], sc2, sh2, g2, ffn_w_up[i].astype(BF16), ffn_conv_w[i],
                 ffn_conv_b[i], ffn_w_down[i].astype(BF16))
    return x
```

```python
import functools
import math

import numpy as np
import jax
import jax.numpy as jnp
from jax import lax
from jax.experimental import pallas as pl
from jax.experimental.pallas import tpu as pltpu

F32 = jnp.float32
BF16 = jnp.bfloat16
I32 = jnp.int32

LANES = 128
SUBLANES = 8
VMEM_LIMIT = 56 * 1024 * 1024

RMS_EPS = 1e-6
NEG = -1e30
LOG2E = math.log2(math.e)
SB_EXIT = -104.0
INT_MIN = -(2 ** 31)

SB_HEADS = 16
SB_HEAD_DIM = 64
DSA_HEADS = 16
DSA_LATENT = 128
DSA_V_DIM = 64
V_PAD = 16
DSA_GROUP = 8
IDX_HEADS = 8
IDX_DIM = 64
TOPK_MAX = 256
NUM_BUCKETS = 32
MAX_DISTANCE = 128
CONV_W = 3


def _cparams(*sem):
    return pltpu.CompilerParams(dimension_semantics=sem, vmem_limit_bytes=VMEM_LIMIT)


def _dot(a, b):
    return jnp.dot(a, b, preferred_element_type=F32)


def _dot_nt(a, b):
    return lax.dot_general(a, b, (((1,), (1,)), ((), ())), preferred_element_type=F32)


def _split_bf16(v):
    hi = v.astype(BF16)
    lo = (v - hi.astype(F32)).astype(BF16)
    return hi, lo


def _mod_kernel(c_ref, w_ref, b_ref, o_ref):
    c = c_ref[...]
    cond = c * (1.0 / (1.0 + jnp.exp(-c)))
    ch, cl = _split_bf16(cond)
    wh, wl = _split_bf16(w_ref[0])
    o_ref[0] = _dot(ch, wh) + _dot(ch, wl) + _dot(cl, wh) + b_ref[0]


def _modulation(c, ada_w, ada_b):
    depth, d, n = ada_w.shape
    b = c.shape[0]
    tn = 1024
    return pl.pallas_call(
        _mod_kernel,
        grid=(depth, n // tn),
        in_specs=[
            pl.BlockSpec((b, d), lambda i, j: (0, 0)),
            pl.BlockSpec((1, d, tn), lambda i, j: (i, 0, j)),
            pl.BlockSpec((1, 1, tn), lambda i, j: (i, 0, j)),
        ],
        out_specs=pl.BlockSpec((1, b, tn), lambda i, j: (i, 0, j)),
        out_shape=jax.ShapeDtypeStruct((depth, b, n), F32),
        compiler_params=_cparams("arbitrary", "arbitrary"),
        name="mod",
    )(c, ada_w, ada_b.reshape(depth, 1, n))


def _modulated_norm(x, g, sc, sh):
    ms = jnp.mean(x * x, axis=-1, keepdims=True)
    return (x * lax.rsqrt(ms + RMS_EPS) * g) * (1.0 + sc) + sh


def _norm_proj_kernel(x_ref, g_ref, sc_ref, sh_ref, w_ref, *o_refs, splits, chunk):
    hb = _modulated_norm(x_ref[0], g_ref[...], sc_ref[0], sh_ref[0]).astype(BF16)
    off = 0
    for o_ref, n in zip(o_refs, splits):
        for c0 in range(0, n, chunk):
            c1 = min(n, c0 + chunk)
            o_ref[0, :, c0:c1] = _dot(hb, w_ref[:, off + c0:off + c1]).astype(o_ref.dtype)
        off += n


def _norm_proj(x, g, sc, sh, w, splits, tm=512):
    b, s, d = x.shape
    n = w.shape[1]
    assert sum(splits) == n
    kern = functools.partial(_norm_proj_kernel, splits=tuple(splits), chunk=512)
    return pl.pallas_call(
        kern,
        grid=(b, s // tm),
        in_specs=[
            pl.BlockSpec((1, tm, d), lambda i, j: (i, j, 0)),
            pl.BlockSpec((1, d), lambda i, j: (0, 0)),
            pl.BlockSpec((1, 1, d), lambda i, j: (i, 0, 0)),
            pl.BlockSpec((1, 1, d), lambda i, j: (i, 0, 0)),
            pl.BlockSpec((d, n), lambda i, j: (0, 0)),
        ],
        out_specs=[pl.BlockSpec((1, tm, m), lambda i, j: (i, j, 0)) for m in splits],
        out_shape=[jax.ShapeDtypeStruct((b, s, m), BF16) for m in splits],
        compiler_params=_cparams("arbitrary", "arbitrary"),
        name="norm_proj",
    )(x, g.reshape(1, d), sc.reshape(b, 1, d), sh.reshape(b, 1, d), w)


def _out_res_kernel(o_ref, w_ref, x_ref, gate_ref, y_ref):
    y_ref[0] = x_ref[0] + gate_ref[0] * _dot(o_ref[0], w_ref[...])


def _out_res(o, w, x, gate, tm=512):
    b, s, d = x.shape
    k = o.shape[-1]
    return pl.pallas_call(
        _out_res_kernel,
        grid=(b, s // tm),
        in_specs=[
            pl.BlockSpec((1, tm, k), lambda i, j: (i, j, 0)),
            pl.BlockSpec((k, d), lambda i, j: (0, 0)),
            pl.BlockSpec((1, tm, d), lambda i, j: (i, j, 0)),
            pl.BlockSpec((1, 1, d), lambda i, j: (i, 0, 0)),
        ],
        out_specs=pl.BlockSpec((1, tm, d), lambda i, j: (i, j, 0)),
        out_shape=jax.ShapeDtypeStruct((b, s, d), F32),
        compiler_params=_cparams("arbitrary", "arbitrary"),
        name="out_res",
    )(o, w, x, gate.reshape(b, 1, d))


def _ffn_kernel(x_ref, g_ref, sc_ref, sh_ref, gate_ref, wu_ref, cw_ref, cb_ref, wd_ref, y_ref,
                act_scr, carry_scr, *, fc):
    si = pl.program_id(1)
    tm = act_scr.shape[0]
    f = act_scr.shape[1]
    nf = f // fc

    @pl.when(si == 0)
    def _():
        carry_scr[...] = jnp.zeros_like(carry_scr)

    hb = _modulated_norm(x_ref[0], g_ref[...], sc_ref[0], sh_ref[0]).astype(BF16)
    row = lax.broadcasted_iota(I32, (tm, 1), 0)

    def up(c):
        return (_dot(hb, wu_ref[:, c * fc:(c + 1) * fc]), _dot(hb, wu_ref[:, f + c * fc:f + (c + 1) * fc]))

    def conv(u, col0):
        prev = carry_scr[:, col0:col0 + fc]
        carry_scr[:, col0:col0 + fc] = u[tm - SUBLANES:, :]
        p1 = prev[SUBLANES - 1:SUBLANES, :]
        p2 = prev[SUBLANES - 2:SUBLANES - 1, :]
        u1 = jnp.where(row == 0, p1, pltpu.roll(u, 1, 0))
        u2 = jnp.where(row == 0, p2, jnp.where(row == 1, p1, pltpu.roll(u, 2, 0)))
        cw = cw_ref[:, col0:col0 + fc]
        return cb_ref[:, col0:col0 + fc] + u2 * cw[0:1, :] + u1 * cw[1:2, :] + u * cw[2:3, :]

    nxt = up(0)
    for c in range(nf):
        ug, uv = nxt
        if c + 1 < nf:
            nxt = up(c + 1)
        yg = conv(ug, c * fc)
        yv = conv(uv, f + c * fc)
        act_scr[:, c * fc:(c + 1) * fc] = (yg * (1.0 / (1.0 + jnp.exp(-yg))) * yv).astype(BF16)

    y_ref[0] = x_ref[0] + gate_ref[0] * _dot(act_scr[...], wd_ref[...])


def _ffn(x, g, sc, sh, gate, w_up, conv_w, conv_b, w_down, tm=256, fc=256):
    b, s, d = x.shape
    f = w_down.shape[0]
    kern = functools.partial(_ffn_kernel, fc=fc)
    vec = lambda a: a.reshape(b, 1, d)
    const = lambda i, j: (0, 0)
    return pl.pallas_call(
        kern,
        grid=(b, s // tm),
        in_specs=[
            pl.BlockSpec((1, tm, d), lambda i, j: (i, j, 0)),
            pl.BlockSpec((1, d), const),
            pl.BlockSpec((1, 1, d), lambda i, j: (i, 0, 0)),
            pl.BlockSpec((1, 1, d), lambda i, j: (i, 0, 0)),
            pl.BlockSpec((1, 1, d), lambda i, j: (i, 0, 0)),
            pl.BlockSpec((d, 2 * f), const),
            pl.BlockSpec((CONV_W, 2 * f), const),
            pl.BlockSpec((1, 2 * f), const),
            pl.BlockSpec((f, d), const),
        ],
        out_specs=pl.BlockSpec((1, tm, d), lambda i, j: (i, j, 0)),
        out_shape=jax.ShapeDtypeStruct((b, s, d), F32),
        scratch_shapes=[
            pltpu.VMEM((tm, f), BF16),
            pltpu.VMEM((SUBLANES, 2 * f), F32),
        ],
        compiler_params=_cparams("arbitrary", "arbitrary"),
        name="ffn",
    )(x, g.reshape(1, d), vec(sc), vec(sh), vec(gate), w_up, conv_w, conv_b.reshape(1, 2 * f), w_down)


def _sb_kernel(q_ref, k_ref, v_ref, tri_ref, o_ref, *, tq):
    qi = pl.program_id(2)
    lane = lax.broadcasted_iota(I32, (1, LANES), 1)
    halves = (lane < SB_HEAD_DIM, lane >= SB_HEAD_DIM)
    q2 = q_ref[0] * (SB_HEAD_DIM ** -0.5)
    zero = jnp.zeros_like(q2)
    qstack = jnp.concatenate([jnp.where(hm, q2, zero) for hm in halves], axis=0)
    tri2 = tri_ref[...]
    row = lax.broadcasted_iota(I32, (2 * tq, tq), 0)
    col = lax.broadcasted_iota(I32, (2 * tq, tq), 1)
    strict = col < jnp.where(row >= tq, row - tq, row)

    def qk(jb):
        start = pl.multiple_of(jb * tq, tq)
        return _dot_nt(qstack, k_ref[0, pl.ds(start, tq), :])

    def log_keep(z, masked):
        lk = -(jnp.maximum(z, 0.0) + jnp.log(1.0 + jnp.exp(-jnp.abs(z))))
        return jnp.where(strict, lk, 0.0) if masked else lk

    def suffix(lk):
        hi, lo = _split_bf16(lk)
        return _dot(jnp.concatenate([hi, lo], axis=1), tri2)

    def weights(z, incl, r, masked):
        a = jnp.exp(z + incl + r)
        return jnp.where(strict, a, 0.0) if masked else a

    def av(a, jb):
        start = pl.multiple_of(jb * tq, tq)
        vb = v_ref[0, pl.ds(start, tq), :]
        vstack = jnp.concatenate([jnp.where(hm, vb, jnp.zeros_like(vb)) for hm in halves], axis=0)
        a2 = jnp.concatenate([a[:tq], a[tq:]], axis=1).astype(BF16)
        return _dot(a2, vstack)

    has_prev = qi >= 1
    jp = jnp.maximum(qi - 1, 0)
    z_d = qk(qi)
    z_p = qk(jp)
    incl_d = suffix(log_keep(z_d, True))
    incl_p = suffix(log_keep(z_p, False))
    r0 = jnp.zeros((2 * tq, 1), F32)
    acc = av(weights(z_d, incl_d, r0, True), qi)
    r1 = incl_d[:, 0:1]
    acc = acc + av(jnp.where(has_prev, weights(z_p, incl_p, r1, False), 0.0), jp)
    r2 = r1 + jnp.where(has_prev, incl_p[:, 0:1], 0.0)

    def live(r):
        return jnp.max(r) > SB_EXIT

    def step(state):
        i, acc, r, _ = state
        jb = qi - 2 - i
        z = qk(jb)
        incl = suffix(log_keep(z, False))
        acc = acc + av(weights(z, incl, r, False), jb)
        r = r + incl[:, 0:1]
        return i + 1, acc, r, live(r)

    _, acc, _, _ = lax.while_loop(lambda st: (st[0] < qi - 1) & st[3], step,
                                  (jnp.int32(0), acc, r2, live(r2)))
    o_ref[0] = acc.astype(o_ref.dtype)


def _sb_attention(qkv, tq=256):
    b, s, n3 = qkv.shape
    n = n3 // 3
    npair = n // LANES
    tri = (jnp.arange(tq)[:, None] >= jnp.arange(tq)[None, :]).astype(BF16)
    tri = jnp.concatenate([tri, tri], axis=0)
    kern = functools.partial(_sb_kernel, tq=tq)
    return pl.pallas_call(
        kern,
        grid=(b, npair, s // tq),
        in_specs=[
            pl.BlockSpec((1, tq, LANES), lambda i, p, j: (i, j, p)),
            pl.BlockSpec((1, s, LANES), lambda i, p, j: (i, 0, npair + p)),
            pl.BlockSpec((1, s, LANES), lambda i, p, j: (i, 0, 2 * npair + p)),
            pl.BlockSpec((2 * tq, tq), lambda i, p, j: (0, 0)),
        ],
        out_specs=pl.BlockSpec((1, tq, LANES), lambda i, p, j: (i, j, p)),
        out_shape=jax.ShapeDtypeStruct((b, s, n), BF16),
        compiler_params=_cparams("arbitrary", "arbitrary", "arbitrary"),
        name="sb_attn",
    )(qkv, qkv, qkv, tri)


def _t5_bucket_thresholds():
    max_exact = NUM_BUCKETS // 2
    n = np.arange(0, 4 * MAX_DISTANCE, dtype=np.int32)
    nf = np.maximum(n, 1).astype(np.float32)
    large = max_exact + (np.log(nf / np.float32(max_exact)) / np.float32(math.log(MAX_DISTANCE / max_exact))
                         * np.float32(NUM_BUCKETS - max_exact)).astype(np.int32)
    large = np.minimum(large, NUM_BUCKETS - 1)
    bucket = np.where(n < max_exact, n, large)
    assert np.all(np.diff(bucket) >= 0) and bucket[-1] == NUM_BUCKETS - 1
    return [int(np.argmax(bucket >= bb)) for bb in range(max_exact + 1, NUM_BUCKETS)]


def _sortable_key(v):
    v = jnp.where(v == 0.0, 0.0, v)
    bits = lax.bitcast_convert_type(v, I32)
    return bits ^ ((bits >> 31) & 0x7FFFFFFF)


def _tree_sum(parts):
    parts = list(parts)
    while len(parts) > 1:
        parts = [parts[i] + parts[i + 1] if i + 1 < len(parts) else parts[i] for i in range(0, len(parts), 2)]
    return parts[0]


def _dsa_kernel(rb_ref, q_ref, qidx_ref, lat_ref, kidx_ref, qg_ref, kg_ref, wuv_ref, tri_ref,
                o_ref, kn_scr, vt_scr, key_scr, add_scr, qn_scr, qis_scr, olat_scr, bias_scr,
                lg_a, lg_b, p_a, p_b, acc_scr, *, tq, topk, thresholds):
    bi = pl.program_id(0)
    qi = pl.program_id(1)
    nkb = vt_scr.shape[0]
    krow = lax.broadcasted_iota(I32, (tq, tq), 0)
    qcol = lax.broadcasted_iota(I32, (tq, tq), 1)
    qcol1 = lax.broadcasted_iota(I32, (1, tq), 1)

    @pl.when((bi == 0) & (qi == 0))
    def _():
        add_scr[nkb + 2 * DSA_HEADS] = jnp.full((tq, tq), NEG, F32)
        max_exact = NUM_BUCKETS // 2
        for near in range(2):
            dist = qcol - krow + near * tq
            large = jnp.full((tq, tq), max_exact, I32)
            for th in thresholds:
                large = large + jnp.where(dist >= th, 1, 0)
            bucket = jnp.where(dist < max_exact, jnp.maximum(dist, 0), large)

            def head_tile(h, _, near=near, bucket=bucket):
                far = rb_ref[NUM_BUCKETS - 1, h]
                tile = jnp.zeros((tq, tq), F32)
                for bb in range(NUM_BUCKETS - 1):
                    tile = jnp.where(bucket == bb, (rb_ref[bb, h] - far) * LOG2E, tile)
                bias_scr[near, h] = tile
                return 0

            lax.fori_loop(0, DSA_HEADS, head_tile, 0)

    @pl.when(qi == 0)
    def _():
        ext_row = lax.broadcasted_iota(I32, (V_PAD, tq), 0)
        for jb in range(nkb):
            lat = lat_ref[0, jb * tq:(jb + 1) * tq, :].astype(F32)
            ms = jnp.mean(lat * lat, axis=-1, keepdims=True)
            kn_scr[jb * tq:(jb + 1) * tq, :] = (lat * lax.rsqrt(ms + RMS_EPS) * kg_ref[...]).astype(BF16)
            vt_scr[jb, :DSA_LATENT, :] = lat.T.astype(BF16)
            vt_scr[jb, DSA_LATENT:, :] = jnp.where(ext_row == 0, 1.0, 0.0).astype(BF16)

    scale = DSA_LATENT ** -0.5 * LOG2E
    for h in range(DSA_HEADS):
        qh = q_ref[0, :, h * DSA_LATENT:(h + 1) * DSA_LATENT].astype(F32)
        ms = jnp.mean(qh * qh, axis=-1, keepdims=True)
        qn_scr[h * tq:(h + 1) * tq, :] = (qh * lax.rsqrt(ms + RMS_EPS) * (qg_ref[...] * scale)).astype(BF16)

    lane = lax.broadcasted_iota(I32, (1, LANES), 1)
    halves = (lane < IDX_DIM, lane >= IDX_DIM)
    n_qi = IDX_HEADS * IDX_DIM
    wi = qidx_ref[0, :, n_qi + LANES:n_qi + 2 * LANES].astype(F32) * (IDX_HEADS ** -0.5)
    wit = wi.T
    for hh in range(IDX_HEADS):
        blk = qidx_ref[0, :, (hh // 2) * LANES:(hh // 2 + 1) * LANES]
        qis_scr[hh * tq:(hh + 1) * tq, :] = jnp.where(halves[hh % 2], blk, jnp.zeros_like(blk))

    def causal(jb):
        return krow + jb * tq <= qcol + qi * tq

    def score_matmul(t, sc_ref):
        start = pl.multiple_of(jnp.minimum(t, qi) * tq, tq)
        sc_ref[...] = _dot_nt(kidx_ref[0, pl.ds(start, tq), :], qis_scr[...])

    def score_keys(t, sc_ref):
        jb = jnp.minimum(t, qi)
        isc = jnp.zeros((tq, tq), F32)
        for hh in range(IDX_HEADS):
            isc = isc + jnp.maximum(sc_ref[:, hh * tq:(hh + 1) * tq], 0.0) * wit[hh:hh + 1, :]
        key_scr[jb] = jnp.where(causal(jb), _sortable_key(isc), INT_MIN)

    score_matmul(0, lg_a)

    def score_pair(u, _):
        t0 = 2 * u
        score_matmul(t0 + 1, lg_b)
        score_keys(t0, lg_a)
        score_matmul(t0 + 2, lg_a)
        score_keys(t0 + 1, lg_b)
        return 0

    lax.fori_loop(0, (qi + 2) // 2, score_pair, 0)

    def count(pred):
        def body(jb, part):
            c = jnp.where(pred(key_scr[jb]), 1.0, 0.0)
            return part + _tree_sum(c[g * SUBLANES:(g + 1) * SUBLANES, :] for g in range(tq // SUBLANES))
        part = lax.fori_loop(0, qi + 1, body, jnp.zeros((SUBLANES, tq), F32))
        return jnp.sum(part, axis=0, keepdims=True)

    kf = float(topk)
    c0 = count(lambda k: k >= 0)
    thr = jnp.where(c0 >= kf, 0, INT_MIN).astype(I32)
    cnt = jnp.where(c0 >= kf, c0, ((qi + 1) * tq).astype(F32))

    def bisect(i, state):
        thr, cnt = state
        cand = thr | lax.shift_left(jnp.int32(1), jnp.int32(30) - i)
        c = count(lambda k: k >= cand)
        return jnp.where(c >= kf, cand, thr), jnp.where(c >= kf, c, cnt)

    thr, cnt = lax.fori_loop(0, 31, bisect, (thr, cnt))
    short = qcol1 + qi * tq + 1 <= topk
    thr = jnp.where(short, INT_MIN, thr)
    straddle = jnp.max(jnp.where(short, 0.0, cnt - kf)) > 0.0

    @pl.when(jnp.logical_not(straddle))
    def _():
        def far_mask(jb, _):
            add_scr[jb] = jnp.where(key_scr[jb] >= thr, 0.0, NEG)
            return 0
        lax.fori_loop(0, qi, far_mask, 0)
        add_scr[qi] = jnp.where((key_scr[qi] >= thr) & causal(qi), 0.0, NEG)

    @pl.when(straddle)
    def _():
        need = kf - count(lambda k: k > thr)
        tri = tri_ref[...]

        def mask_block(jb, eq_before):
            key = key_scr[jb]
            eq = key == thr
            rank = _dot(tri, jnp.where(eq, 1.0, 0.0).astype(BF16)) + eq_before
            sel = ((key > thr) | (eq & (rank <= need))) & causal(jb)
            add_scr[jb] = jnp.where(sel, 0.0, NEG)
            return rank[tq - 1:tq, :]

        lax.fori_loop(0, qi + 1, mask_block, jnp.zeros((1, tq), F32))

    prev = jnp.maximum(qi - 1, 0)
    prev_pen = jnp.where(qi >= 1, 0.0, NEG)
    nblk = prev + 2

    def near_tiles(h, _):
        add_scr[nkb + h] = add_scr[prev] + bias_scr[1, h] + prev_pen
        add_scr[nkb + DSA_HEADS + h] = add_scr[qi] + bias_scr[0, h]
        return 0

    lax.fori_loop(0, DSA_HEADS, near_tiles, 0)

    def kv_block(t):
        t = jnp.maximum(t, 0)
        return jnp.minimum(jnp.where(t < prev, t, jnp.where(t == prev, prev, qi)), qi)

    def add_index(t, h):
        return jnp.where(t < prev, t,
                         jnp.where(t == prev, nkb + h,
                                   jnp.where(t == prev + 1, nkb + DSA_HEADS + h, nkb + 2 * DSA_HEADS)))

    gw = DSA_GROUP * tq

    def head_group(g, _):
        row0 = pl.multiple_of(g * gw, gw)
        qg = qn_scr[pl.ds(row0, gw), :]

        def stage_qk(t, lk_ref):
            start = pl.multiple_of(kv_block(t) * tq, tq)
            lg = _dot_nt(kn_scr[pl.ds(start, tq), :], qg)
            tops = []
            for k in range(DSA_GROUP):
                lk = lg[:, k * tq:(k + 1) * tq] + add_scr[add_index(t, g * DSA_GROUP + k)]
                lk_ref[:, k * tq:(k + 1) * tq] = lk
                tops.append(jnp.max(lk, axis=0, keepdims=True))
            return tuple(tops)

        def stage_softmax(lk_ref, p_ref, stats, tops):
            new_stats, alphas = [], []
            for k, (m, top) in enumerate(zip(stats, tops)):
                m_new = jnp.maximum(m, top)
                alphas.append(jnp.exp2(m - m_new))
                p_ref[:, k * tq:(k + 1) * tq] = jnp.exp2(lk_ref[:, k * tq:(k + 1) * tq] - m_new).astype(BF16)
                new_stats.append(m_new)
            return tuple(new_stats), tuple(alphas)

        def stage_pv(t, p_ref, alphas):
            pv = _dot(vt_scr[kv_block(t)], p_ref[...])
            for k, alpha in enumerate(alphas):
                acc_scr[k] = alpha * acc_scr[k] + pv[:, k * tq:(k + 1) * tq]

        stats = (jnp.full((1, tq), 0.5 * NEG, F32),) * DSA_GROUP
        ones = (jnp.ones((1, tq), F32),) * DSA_GROUP
        acc_scr[...] = jnp.zeros_like(acc_scr)
        p_b[...] = jnp.zeros_like(p_b)
        tops_a = stage_qk(0, lg_a)

        def pair(u, carry):
            stats, alphas, tops_a = carry
            t0 = 2 * u
            tops_b = stage_qk(t0 + 1, lg_b)
            stage_pv(t0 - 1, p_b, alphas)
            stats, alphas = stage_softmax(lg_a, p_a, stats, tops_a)
            tops_a = stage_qk(t0 + 2, lg_a)
            stage_pv(t0, p_a, alphas)
            stats, alphas = stage_softmax(lg_b, p_b, stats, tops_b)
            return stats, alphas, tops_a

        npair = (nblk + 1) // 2
        stats, alphas, _ = lax.fori_loop(0, npair, pair, (stats, ones, tops_a))
        stage_pv(2 * npair - 1, p_b, alphas)
        for k in range(DSA_GROUP):
            acc = acc_scr[k]
            l = acc[DSA_LATENT:DSA_LATENT + 1, :]
            olat_scr[g * DSA_GROUP + k] = (acc[:DSA_LATENT, :] * (1.0 / l)).T.astype(BF16)
        return 0

    lax.fori_loop(0, DSA_HEADS // DSA_GROUP, head_group, 0)

    for pr in range(DSA_HEADS // 2):
        out = _dot(olat_scr[2 * pr], wuv_ref[2 * pr]) + _dot(olat_scr[2 * pr + 1], wuv_ref[2 * pr + 1])
        o_ref[0, :, pr * LANES:(pr + 1) * LANES] = out.astype(o_ref.dtype)


def _dsa_attention(qp, idx, lat, q_gain, k_gain, wuv_ext, rel_bias, tq=256):
    b, s, nq = qp.shape
    assert DSA_GROUP == IDX_HEADS
    topk = min(TOPK_MAX, s // 4)
    nkb = s // tq
    n_idx = idx.shape[-1]
    tri = (jnp.arange(tq)[:, None] >= jnp.arange(tq)[None, :]).astype(BF16)
    kern = functools.partial(_dsa_kernel, tq=tq, topk=topk, thresholds=tuple(_t5_bucket_thresholds()))
    return pl.pallas_call(
        kern,
        grid=(b, s // tq),
        in_specs=[
            pl.BlockSpec(memory_space=pltpu.SMEM),
            pl.BlockSpec((1, tq, nq), lambda i, j: (i, j, 0)),
            pl.BlockSpec((1, tq, n_idx), lambda i, j: (i, j, 0)),
            pl.BlockSpec((1, s, LANES), lambda i, j: (i, 0, 0)),
            pl.BlockSpec((1, s, LANES), lambda i, j: (i, 0, IDX_HEADS * IDX_DIM // LANES)),
            pl.BlockSpec((1, DSA_LATENT), lambda i, j: (0, 0)),
            pl.BlockSpec((1, DSA_LATENT), lambda i, j: (0, 0)),
            pl.BlockSpec((DSA_HEADS, DSA_LATENT, LANES), lambda i, j: (0, 0, 0)),
            pl.BlockSpec((tq, tq), lambda i, j: (0, 0)),
        ],
        out_specs=pl.BlockSpec((1, tq, DSA_HEADS * DSA_V_DIM), lambda i, j: (i, j, 0)),
        out_shape=jax.ShapeDtypeStruct((b, s, DSA_HEADS * DSA_V_DIM), BF16),
        scratch_shapes=[
            pltpu.VMEM((s, DSA_LATENT), BF16),
            pltpu.VMEM((nkb, DSA_LATENT + V_PAD, tq), BF16),
            pltpu.VMEM((nkb, tq, tq), I32),
            pltpu.VMEM((nkb + 2 * DSA_HEADS + 1, tq, tq), F32),
            pltpu.VMEM((DSA_HEADS * tq, DSA_LATENT), BF16),
            pltpu.VMEM((IDX_HEADS * tq, LANES), BF16),
            pltpu.VMEM((DSA_HEADS, tq, DSA_LATENT), BF16),
            pltpu.VMEM((2, DSA_HEADS, tq, tq), F32),
            pltpu.VMEM((tq, DSA_GROUP * tq), F32),
            pltpu.VMEM((tq, DSA_GROUP * tq), F32),
            pltpu.VMEM((tq, DSA_GROUP * tq), BF16),
            pltpu.VMEM((tq, DSA_GROUP * tq), BF16),
            pltpu.VMEM((DSA_GROUP, DSA_LATENT + V_PAD, tq), F32),
        ],
        compiler_params=_cparams("arbitrary", "arbitrary"),
        name="dsa_attn",
    )(rel_bias, qp, idx, lat, idx, q_gain.reshape(1, -1), k_gain.reshape(1, -1), wuv_ext, tri)


def _dsa_weights(w_in, w_uv):
    o1 = DSA_HEADS * DSA_LATENT
    o2 = o1 + DSA_LATENT
    o3 = o2 + IDX_HEADS * IDX_DIM
    o4 = o3 + IDX_DIM
    d = w_in.shape[0]
    w_ki = w_in[:, o3:o4]
    w_wi = w_in[:, o4:]
    pad = jnp.zeros((d, LANES - IDX_HEADS), w_in.dtype)
    w = jnp.concatenate([w_in[:, :o3], w_ki, w_ki, w_wi, pad], axis=1).astype(BF16)
    zeros = jnp.zeros_like(w_uv)
    even = jnp.concatenate([w_uv, zeros], axis=-1)
    odd = jnp.concatenate([zeros, w_uv], axis=-1)
    is_even = (jnp.arange(DSA_HEADS) % 2 == 0)[:, None, None]
    wuv_ext = jnp.where(is_even, even, odd).astype(BF16)
    splits = (o1, DSA_LATENT, IDX_HEADS * IDX_DIM + 2 * LANES)
    return w, wuv_ext, splits


def kernel(x, c, ada_w, ada_b, norm_mix, norm_ffn, sb_w_in, sb_w_out, dsa_w_in, dsa_q_norm,
           dsa_k_norm, dsa_w_uv, dsa_w_out, rel_bias, ffn_w_up, ffn_conv_w, ffn_conv_b, ffn_w_down):
    depth = ada_w.shape[0]
    d = x.shape[-1]
    mod = _modulation(c, ada_w, ada_b)
    for i in range(depth):
        sh1, sc1, g1, sh2, sc2, g2 = [mod[i, :, k * d:(k + 1) * d] for k in range(6)]
        j = i // 2
        if i % 2 == 0:
            w_in = sb_w_in[j].astype(BF16)
            (qkv,) = _norm_proj(x, norm_mix[i], sc1, sh1, w_in, (w_in.shape[1],))
            o = _sb_attention(qkv)
            w_out = sb_w_out[j]
        else:
            w_in, wuv_ext, splits = _dsa_weights(dsa_w_in[j], dsa_w_uv[j])
            qp, lat, idx = _norm_proj(x, norm_mix[i], sc1, sh1, w_in, splits)
            o = _dsa_attention(qp, idx, lat, dsa_q_norm[j], dsa_k_norm[j], wuv_ext, rel_bias)
            w_out = dsa_w_out[j]
        x = _out_res(o, w_out.astype(BF16), x, g1)
        x = _ffn(x, norm_ffn[i], sc2, sh2, g2, ffn_w_up[i].astype(BF16), ffn_conv_w[i],
                 ffn_conv_b[i], ffn_w_down[i].astype(BF16))
    return x
```

```python
import functools
import math

import numpy as np
import jax
import jax.numpy as jnp
from jax import lax
from jax.experimental import pallas as pl
from jax.experimental.pallas import tpu as pltpu

F32 = jnp.float32
BF16 = jnp.bfloat16
I32 = jnp.int32

LANES = 128
SUBLANES = 8
VMEM_LIMIT = 56 * 1024 * 1024

RMS_EPS = 1e-6
NEG = -1e30
LOG2E = math.log2(math.e)
SB_EXIT = -104.0
INT_MIN = -(2 ** 31)

SB_HEADS = 16
SB_HEAD_DIM = 64
DSA_HEADS = 16
DSA_LATENT = 128
DSA_V_DIM = 64
V_PAD = 16
DSA_GROUP = 8
IDX_HEADS = 8
IDX_DIM = 64
TOPK_MAX = 256
NUM_BUCKETS = 32
MAX_DISTANCE = 128
CONV_W = 3


def _cparams(*sem):
    return pltpu.CompilerParams(dimension_semantics=sem, vmem_limit_bytes=VMEM_LIMIT)


def _dot(a, b):
    return jnp.dot(a, b, preferred_element_type=F32)


def _dot_nt(a, b):
    return lax.dot_general(a, b, (((1,), (1,)), ((), ())), preferred_element_type=F32)


def _split_bf16(v):
    hi = v.astype(BF16)
    lo = (v - hi.astype(F32)).astype(BF16)
    return hi, lo


def _mod_kernel(c_ref, w_ref, b_ref, o_ref):
    c = c_ref[...]
    cond = c * (1.0 / (1.0 + jnp.exp(-c)))
    ch, cl = _split_bf16(cond)
    wh, wl = _split_bf16(w_ref[0])
    o_ref[0] = _dot(ch, wh) + _dot(ch, wl) + _dot(cl, wh) + b_ref[0]


def _modulation(c, ada_w, ada_b):
    depth, d, n = ada_w.shape
    b = c.shape[0]
    tn = 1024
    return pl.pallas_call(
        _mod_kernel,
        grid=(depth, n // tn),
        in_specs=[
            pl.BlockSpec((b, d), lambda i, j: (0, 0)),
            pl.BlockSpec((1, d, tn), lambda i, j: (i, 0, j)),
            pl.BlockSpec((1, 1, tn), lambda i, j: (i, 0, j)),
        ],
        out_specs=pl.BlockSpec((1, b, tn), lambda i, j: (i, 0, j)),
        out_shape=jax.ShapeDtypeStruct((depth, b, n), F32),
        compiler_params=_cparams("arbitrary", "arbitrary"),
        name="mod",
    )(c, ada_w, ada_b.reshape(depth, 1, n))


def _modulated_norm(x, g, sc, sh):
    ms = jnp.mean(x * x, axis=-1, keepdims=True)
    return (x * lax.rsqrt(ms + RMS_EPS) * g) * (1.0 + sc) + sh


def _norm_proj_kernel(x_ref, g_ref, sc_ref, sh_ref, w_ref, *o_refs, splits, chunk):
    hb = _modulated_norm(x_ref[0], g_ref[...], sc_ref[0], sh_ref[0]).astype(BF16)
    off = 0
    for o_ref, n in zip(o_refs, splits):
        for c0 in range(0, n, chunk):
            c1 = min(n, c0 + chunk)
            o_ref[0, :, c0:c1] = _dot(hb, w_ref[:, off + c0:off + c1]).astype(o_ref.dtype)
        off += n


def _norm_proj(x, g, sc, sh, w, splits, tm=512):
    b, s, d = x.shape
    n = w.shape[1]
    assert sum(splits) == n
    kern = functools.partial(_norm_proj_kernel, splits=tuple(splits), chunk=512)
    return pl.pallas_call(
        kern,
        grid=(b, s // tm),
        in_specs=[
            pl.BlockSpec((1, tm, d), lambda i, j: (i, j, 0)),
            pl.BlockSpec((1, d), lambda i, j: (0, 0)),
            pl.BlockSpec((1, 1, d), lambda i, j: (i, 0, 0)),
            pl.BlockSpec((1, 1, d), lambda i, j: (i, 0, 0)),
            pl.BlockSpec((d, n), lambda i, j: (0, 0)),
        ],
        out_specs=[pl.BlockSpec((1, tm, m), lambda i, j: (i, j, 0)) for m in splits],
        out_shape=[jax.ShapeDtypeStruct((b, s, m), BF16) for m in splits],
        compiler_params=_cparams("arbitrary", "arbitrary"),
        name="norm_proj",
    )(x, g.reshape(1, d), sc.reshape(b, 1, d), sh.reshape(b, 1, d), w)


def _ffn_kernel(x_ref, o_ref, wo_ref, mgate_ref, g_ref, sc_ref, sh_ref, gate_ref, wu_ref, cw_ref, cb_ref,
                wd_ref, y_ref, act_scr, carry_scr, *, fc):
    si = pl.program_id(1)
    tm = act_scr.shape[0]
    f = act_scr.shape[1]
    nf = f // fc

    @pl.when(si == 0)
    def _():
        carry_scr[...] = jnp.zeros_like(carry_scr)

    x1 = x_ref[0] + mgate_ref[0] * _dot(o_ref[0], wo_ref[...])
    hb = _modulated_norm(x1, g_ref[...], sc_ref[0], sh_ref[0]).astype(BF16)
    row = lax.broadcasted_iota(I32, (tm, 1), 0)

    def up(c):
        return (_dot(hb, wu_ref[:, c * fc:(c + 1) * fc]), _dot(hb, wu_ref[:, f + c * fc:f + (c + 1) * fc]))

    def conv(u, col0):
        prev = carry_scr[:, col0:col0 + fc]
        carry_scr[:, col0:col0 + fc] = u[tm - SUBLANES:, :]
        p1 = prev[SUBLANES - 1:SUBLANES, :]
        p2 = prev[SUBLANES - 2:SUBLANES - 1, :]
        u1 = jnp.where(row == 0, p1, pltpu.roll(u, 1, 0))
        u2 = jnp.where(row == 0, p2, jnp.where(row == 1, p1, pltpu.roll(u, 2, 0)))
        cw = cw_ref[:, col0:col0 + fc]
        return cb_ref[:, col0:col0 + fc] + u2 * cw[0:1, :] + u1 * cw[1:2, :] + u * cw[2:3, :]

    nxt = up(0)
    for c in range(nf):
        ug, uv = nxt
        if c + 1 < nf:
            nxt = up(c + 1)
        yg = conv(ug, c * fc)
        yv = conv(uv, f + c * fc)
        act_scr[:, c * fc:(c + 1) * fc] = (yg * (1.0 / (1.0 + jnp.exp(-yg))) * yv).astype(BF16)

    y_ref[0] = x1 + gate_ref[0] * _dot(act_scr[...], wd_ref[...])


def _mix_ffn(x, o, w_out, mgate, g, sc, sh, gate, w_up, conv_w, conv_b, w_down, tm=256, fc=256):
    b, s, d = x.shape
    f = w_down.shape[0]
    k = o.shape[-1]
    kern = functools.partial(_ffn_kernel, fc=fc)
    vec = lambda a: a.reshape(b, 1, d)
    const = lambda i, j: (0, 0)
    return pl.pallas_call(
        kern,
        grid=(b, s // tm),
        in_specs=[
            pl.BlockSpec((1, tm, d), lambda i, j: (i, j, 0)),
            pl.BlockSpec((1, tm, k), lambda i, j: (i, j, 0)),
            pl.BlockSpec((k, d), const),
            pl.BlockSpec((1, 1, d), lambda i, j: (i, 0, 0)),
            pl.BlockSpec((1, d), const),
            pl.BlockSpec((1, 1, d), lambda i, j: (i, 0, 0)),
            pl.BlockSpec((1, 1, d), lambda i, j: (i, 0, 0)),
            pl.BlockSpec((1, 1, d), lambda i, j: (i, 0, 0)),
            pl.BlockSpec((d, 2 * f), const),
            pl.BlockSpec((CONV_W, 2 * f), const),
            pl.BlockSpec((1, 2 * f), const),
            pl.BlockSpec((f, d), const),
        ],
        out_specs=pl.BlockSpec((1, tm, d), lambda i, j: (i, j, 0)),
        out_shape=jax.ShapeDtypeStruct((b, s, d), F32),
        scratch_shapes=[
            pltpu.VMEM((tm, f), BF16),
            pltpu.VMEM((SUBLANES, 2 * f), F32),
        ],
        compiler_params=_cparams("arbitrary", "arbitrary"),
        name="ffn",
    )(x, o, w_out, vec(mgate), g.reshape(1, d), vec(sc), vec(sh), vec(gate), w_up, conv_w,
      conv_b.reshape(1, 2 * f), w_down)


def _sb_kernel(q_ref, k_ref, v_ref, tri_ref, o_ref, acc_scr, r_scr, *, tq):
    qi = pl.program_id(2)
    npair = acc_scr.shape[0]
    lane = lax.broadcasted_iota(I32, (1, LANES), 1)
    halves = (lane < SB_HEAD_DIM, lane >= SB_HEAD_DIM)
    tri2 = tri_ref[...]
    row = lax.broadcasted_iota(I32, (2 * tq, tq), 0)
    col = lax.broadcasted_iota(I32, (2 * tq, tq), 1)
    strict = col < jnp.where(row >= tq, row - tq, row)
    has_prev = qi >= 1
    jp = jnp.maximum(qi - 1, 0)

    def log_keep(z, masked):
        lk = -(jnp.maximum(z, 0.0) + jnp.log(1.0 + jnp.exp(-jnp.abs(z))))
        return jnp.where(strict, lk, 0.0) if masked else lk

    def suffix(lk):
        hi, lo = _split_bf16(lk)
        return _dot(jnp.concatenate([hi, lo], axis=1), tri2)

    def weights(z, incl, r, masked):
        a = jnp.exp(z + incl + r)
        return jnp.where(strict, a, 0.0) if masked else a

    def pair_fns(pp):
        lanes = slice(pp * LANES, (pp + 1) * LANES)
        q2 = q_ref[0, :, lanes] * (SB_HEAD_DIM ** -0.5)
        zero = jnp.zeros_like(q2)
        qstack = jnp.concatenate([jnp.where(hm, q2, zero) for hm in halves], axis=0)

        def qk(jb):
            start = pl.multiple_of(jb * tq, tq)
            return _dot_nt(qstack, k_ref[0, pl.ds(start, tq), lanes])

        def av(a, jb):
            start = pl.multiple_of(jb * tq, tq)
            vb = v_ref[0, pl.ds(start, tq), lanes]
            vstack = jnp.concatenate([jnp.where(hm, vb, jnp.zeros_like(vb)) for hm in halves], axis=0)
            a2 = jnp.concatenate([a[:tq], a[tq:]], axis=1).astype(BF16)
            return _dot(a2, vstack)

        return qk, av

    fns = [pair_fns(pp) for pp in range(npair)]
    zs = {0: (fns[0][0](qi), fns[0][0](jp))}
    for pp in range(npair):
        qk, av = fns[pp]
        if pp + 1 < npair:
            zs[pp + 1] = (fns[pp + 1][0](qi), fns[pp + 1][0](jp))
        z_d, z_p = zs.pop(pp)
        incl_d = suffix(log_keep(z_d, True))
        incl_p = suffix(log_keep(z_p, False))
        r0 = jnp.zeros((2 * tq, 1), F32)
        acc = av(weights(z_d, incl_d, r0, True), qi)
        r1 = incl_d[:, 0:1]
        acc = acc + av(jnp.where(has_prev, weights(z_p, incl_p, r1, False), 0.0), jp)
        acc_scr[pp] = acc
        r_scr[pp] = r1 + jnp.where(has_prev, incl_p[:, 0:1], 0.0)

    def live(r):
        return jnp.max(r) > SB_EXIT

    for pp in range(npair):
        qk, av = pair_fns(pp)

        def step(state, qk=qk, av=av):
            i, acc, r, _ = state
            jb = qi - 2 - i
            z = qk(jb)
            incl = suffix(log_keep(z, False))
            acc = acc + av(weights(z, incl, r, False), jb)
            r = r + incl[:, 0:1]
            return i + 1, acc, r, live(r)

        r2 = r_scr[pp]
        _, acc, _, _ = lax.while_loop(lambda st: (st[0] < qi - 1) & st[3], step,
                                      (jnp.int32(0), acc_scr[pp], r2, live(r2)))
        o_ref[0, :, pp * LANES:(pp + 1) * LANES] = acc.astype(o_ref.dtype)


def _sb_attention(qkv, tq=256, pairs_per_step=4):
    b, s, n3 = qkv.shape
    n = n3 // 3
    gw = pairs_per_step * LANES
    ngrp = n // gw
    tri = (jnp.arange(tq)[:, None] >= jnp.arange(tq)[None, :]).astype(BF16)
    tri = jnp.concatenate([tri, tri], axis=0)
    kern = functools.partial(_sb_kernel, tq=tq)
    return pl.pallas_call(
        kern,
        grid=(b, ngrp, s // tq),
        in_specs=[
            pl.BlockSpec((1, tq, gw), lambda i, p, j: (i, j, p)),
            pl.BlockSpec((1, s, gw), lambda i, p, j: (i, 0, ngrp + p)),
            pl.BlockSpec((1, s, gw), lambda i, p, j: (i, 0, 2 * ngrp + p)),
            pl.BlockSpec((2 * tq, tq), lambda i, p, j: (0, 0)),
        ],
        out_specs=pl.BlockSpec((1, tq, gw), lambda i, p, j: (i, j, p)),
        out_shape=jax.ShapeDtypeStruct((b, s, n), BF16),
        scratch_shapes=[
            pltpu.VMEM((pairs_per_step, tq, LANES), F32),
            pltpu.VMEM((pairs_per_step, 2 * tq, 1), F32),
        ],
        compiler_params=_cparams("arbitrary", "arbitrary", "arbitrary"),
        name="sb_attn",
    )(qkv, qkv, qkv, tri)


def _t5_bucket_thresholds():
    max_exact = NUM_BUCKETS // 2
    n = np.arange(0, 4 * MAX_DISTANCE, dtype=np.int32)
    nf = np.maximum(n, 1).astype(np.float32)
    large = max_exact + (np.log(nf / np.float32(max_exact)) / np.float32(math.log(MAX_DISTANCE / max_exact))
                         * np.float32(NUM_BUCKETS - max_exact)).astype(np.int32)
    large = np.minimum(large, NUM_BUCKETS - 1)
    bucket = np.where(n < max_exact, n, large)
    assert np.all(np.diff(bucket) >= 0) and bucket[-1] == NUM_BUCKETS - 1
    return [int(np.argmax(bucket >= bb)) for bb in range(max_exact + 1, NUM_BUCKETS)]


def _sortable_key(v):
    v = jnp.where(v == 0.0, 0.0, v)
    bits = lax.bitcast_convert_type(v, I32)
    return bits ^ ((bits >> 31) & 0x7FFFFFFF)


def _tree_sum(parts):
    parts = list(parts)
    while len(parts) > 1:
        parts = [parts[i] + parts[i + 1] if i + 1 < len(parts) else parts[i] for i in range(0, len(parts), 2)]
    return parts[0]


def _dsa_kernel(rb_ref, q_ref, qidx_ref, lat_ref, kidx_ref, qg_ref, kg_ref, wuv_ref, tri_ref,
                o_ref, kn_scr, vt_scr, key_scr, add_scr, qn_scr, qis_scr, olat_scr, bias_scr,
                lg_a, lg_b, p_a, p_b, acc_scr, *, tq, topk, thresholds):
    bi = pl.program_id(0)
    qi = pl.program_id(1)
    nkb = vt_scr.shape[0]
    krow = lax.broadcasted_iota(I32, (tq, tq), 0)
    qcol = lax.broadcasted_iota(I32, (tq, tq), 1)
    qcol1 = lax.broadcasted_iota(I32, (1, tq), 1)

    @pl.when((bi == 0) & (qi == 0))
    def _():
        add_scr[nkb + 2 * DSA_HEADS] = jnp.full((tq, tq), NEG, F32)
        max_exact = NUM_BUCKETS // 2
        for near in range(2):
            dist = qcol - krow + near * tq
            large = jnp.full((tq, tq), max_exact, I32)
            for th in thresholds:
                large = large + jnp.where(dist >= th, 1, 0)
            bucket = jnp.where(dist < max_exact, jnp.maximum(dist, 0), large)

            def head_tile(h, _, near=near, bucket=bucket):
                far = rb_ref[NUM_BUCKETS - 1, h]
                tile = jnp.zeros((tq, tq), F32)
                for bb in range(NUM_BUCKETS - 1):
                    tile = jnp.where(bucket == bb, (rb_ref[bb, h] - far) * LOG2E, tile)
                bias_scr[near, h] = tile
                return 0

            lax.fori_loop(0, DSA_HEADS, head_tile, 0)

    @pl.when(qi == 0)
    def _():
        ext_row = lax.broadcasted_iota(I32, (V_PAD, tq), 0)
        for jb in range(nkb):
            lat = lat_ref[0, jb * tq:(jb + 1) * tq, :].astype(F32)
            ms = jnp.mean(lat * lat, axis=-1, keepdims=True)
            kn_scr[jb * tq:(jb + 1) * tq, :] = (lat * lax.rsqrt(ms + RMS_EPS) * kg_ref[...]).astype(BF16)
            vt_scr[jb, :DSA_LATENT, :] = lat.T.astype(BF16)
            vt_scr[jb, DSA_LATENT:, :] = jnp.where(ext_row == 0, 1.0, 0.0).astype(BF16)

    scale = DSA_LATENT ** -0.5 * LOG2E
    for h in range(DSA_HEADS):
        qh = q_ref[0, :, h * DSA_LATENT:(h + 1) * DSA_LATENT].astype(F32)
        ms = jnp.mean(qh * qh, axis=-1, keepdims=True)
        qn_scr[h * tq:(h + 1) * tq, :] = (qh * lax.rsqrt(ms + RMS_EPS) * (qg_ref[...] * scale)).astype(BF16)

    lane = lax.broadcasted_iota(I32, (1, LANES), 1)
    halves = (lane < IDX_DIM, lane >= IDX_DIM)
    n_qi = IDX_HEADS * IDX_DIM
    wi = qidx_ref[0, :, n_qi + LANES:n_qi + 2 * LANES].astype(F32) * (IDX_HEADS ** -0.5)
    wit = wi.T
    for hh in range(IDX_HEADS):
        blk = qidx_ref[0, :, (hh // 2) * LANES:(hh // 2 + 1) * LANES]
        qis_scr[hh * tq:(hh + 1) * tq, :] = jnp.where(halves[hh % 2], blk, jnp.zeros_like(blk))

    def causal(jb):
        return krow + jb * tq <= qcol + qi * tq

    def score_matmul(t, sc_ref):
        start = pl.multiple_of(jnp.minimum(t, qi) * tq, tq)
        sc_ref[...] = _dot_nt(kidx_ref[0, pl.ds(start, tq), :], qis_scr[...])

    def score_keys(t, sc_ref):
        jb = jnp.minimum(t, qi)
        isc = jnp.zeros((tq, tq), F32)
        for hh in range(IDX_HEADS):
            isc = isc + jnp.maximum(sc_ref[:, hh * tq:(hh + 1) * tq], 0.0) * wit[hh:hh + 1, :]
        key_scr[jb] = jnp.where(causal(jb), _sortable_key(isc), INT_MIN)

    score_matmul(0, lg_a)

    def score_pair(u, _):
        t0 = 2 * u
        score_matmul(t0 + 1, lg_b)
        score_keys(t0, lg_a)
        score_matmul(t0 + 2, lg_a)
        score_keys(t0 + 1, lg_b)
        return 0

    lax.fori_loop(0, (qi + 2) // 2, score_pair, 0)

    def count(pred):
        def body(jb, part):
            c = jnp.where(pred(key_scr[jb]), 1.0, 0.0)
            return part + _tree_sum(c[g * SUBLANES:(g + 1) * SUBLANES, :] for g in range(tq // SUBLANES))
        part = lax.fori_loop(0, qi + 1, body, jnp.zeros((SUBLANES, tq), F32))
        return jnp.sum(part, axis=0, keepdims=True)

    kf = float(topk)
    c0 = count(lambda k: k >= 0)
    thr = jnp.where(c0 >= kf, 0, INT_MIN).astype(I32)
    cnt = jnp.where(c0 >= kf, c0, ((qi + 1) * tq).astype(F32))

    def bisect(i, state):
        thr, cnt = state
        cand = thr | lax.shift_left(jnp.int32(1), jnp.int32(30) - i)
        c = count(lambda k: k >= cand)
        return jnp.where(c >= kf, cand, thr), jnp.where(c >= kf, c, cnt)

    thr, cnt = lax.fori_loop(0, 31, bisect, (thr, cnt))
    short = qcol1 + qi * tq + 1 <= topk
    thr = jnp.where(short, INT_MIN, thr)
    straddle = jnp.max(jnp.where(short, 0.0, cnt - kf)) > 0.0

    @pl.when(jnp.logical_not(straddle))
    def _():
        def far_mask(jb, _):
            add_scr[jb] = jnp.where(key_scr[jb] >= thr, 0.0, NEG)
            return 0
        lax.fori_loop(0, qi, far_mask, 0)
        add_scr[qi] = jnp.where((key_scr[qi] >= thr) & causal(qi), 0.0, NEG)

    @pl.when(straddle)
    def _():
        need = kf - count(lambda k: k > thr)
        tri = tri_ref[...]

        def mask_block(jb, eq_before):
            key = key_scr[jb]
            eq = key == thr
            rank = _dot(tri, jnp.where(eq, 1.0, 0.0).astype(BF16)) + eq_before
            sel = ((key > thr) | (eq & (rank <= need))) & causal(jb)
            add_scr[jb] = jnp.where(sel, 0.0, NEG)
            return rank[tq - 1:tq, :]

        lax.fori_loop(0, qi + 1, mask_block, jnp.zeros((1, tq), F32))

    prev = jnp.maximum(qi - 1, 0)
    prev_pen = jnp.where(qi >= 1, 0.0, NEG)
    nblk = prev + 2

    def near_tiles(h, _):
        add_scr[nkb + h] = add_scr[prev] + bias_scr[1, h] + prev_pen
        add_scr[nkb + DSA_HEADS + h] = add_scr[qi] + bias_scr[0, h]
        return 0

    lax.fori_loop(0, DSA_HEADS, near_tiles, 0)

    def kv_block(t):
        t = jnp.maximum(t, 0)
        return jnp.minimum(jnp.where(t < prev, t, jnp.where(t == prev, prev, qi)), qi)

    def add_index(t, h):
        return jnp.where(t < prev, t,
                         jnp.where(t == prev, nkb + h,
                                   jnp.where(t == prev + 1, nkb + DSA_HEADS + h, nkb + 2 * DSA_HEADS)))

    gw = DSA_GROUP * tq

    def head_group(g, _):
        row0 = pl.multiple_of(g * gw, gw)
        qg = qn_scr[pl.ds(row0, gw), :]

        def stage_qk(t, lk_ref):
            start = pl.multiple_of(kv_block(t) * tq, tq)
            lg = _dot_nt(kn_scr[pl.ds(start, tq), :], qg)
            tops = []
            for k in range(DSA_GROUP):
                lk = lg[:, k * tq:(k + 1) * tq] + add_scr[add_index(t, g * DSA_GROUP + k)]
                lk_ref[:, k * tq:(k + 1) * tq] = lk
                tops.append(jnp.max(lk, axis=0, keepdims=True))
            return tuple(tops)

        def stage_softmax(lk_ref, p_ref, stats, tops):
            new_stats, alphas = [], []
            for k, (m, top) in enumerate(zip(stats, tops)):
                m_new = jnp.maximum(m, top)
                alphas.append(jnp.exp2(m - m_new))
                p_ref[:, k * tq:(k + 1) * tq] = jnp.exp2(lk_ref[:, k * tq:(k + 1) * tq] - m_new).astype(BF16)
                new_stats.append(m_new)
            return tuple(new_stats), tuple(alphas)

        def stage_pv(t, p_ref, alphas):
            pv = _dot(vt_scr[kv_block(t)], p_ref[...])
            for k, alpha in enumerate(alphas):
                acc_scr[k] = alpha * acc_scr[k] + pv[:, k * tq:(k + 1) * tq]

        stats = (jnp.full((1, tq), 0.5 * NEG, F32),) * DSA_GROUP
        ones = (jnp.ones((1, tq), F32),) * DSA_GROUP
        acc_scr[...] = jnp.zeros_like(acc_scr)
        p_b[...] = jnp.zeros_like(p_b)
        tops_a = stage_qk(0, lg_a)

        def pair(u, carry):
            stats, alphas, tops_a = carry
            t0 = 2 * u
            tops_b = stage_qk(t0 + 1, lg_b)
            stage_pv(t0 - 1, p_b, alphas)
            stats, alphas = stage_softmax(lg_a, p_a, stats, tops_a)
            tops_a = stage_qk(t0 + 2, lg_a)
            stage_pv(t0, p_a, alphas)
            stats, alphas = stage_softmax(lg_b, p_b, stats, tops_b)
            return stats, alphas, tops_a

        npair = (nblk + 1) // 2
        stats, alphas, _ = lax.fori_loop(0, npair, pair, (stats, ones, tops_a))
        stage_pv(2 * npair - 1, p_b, alphas)
        for k in range(DSA_GROUP):
            acc = acc_scr[k]
            l = acc[DSA_LATENT:DSA_LATENT + 1, :]
            olat_scr[g * DSA_GROUP + k] = (acc[:DSA_LATENT, :] * (1.0 / l)).T.astype(BF16)
        return 0

    lax.fori_loop(0, DSA_HEADS // DSA_GROUP, head_group, 0)

    for pr in range(DSA_HEADS // 2):
        out = _dot(olat_scr[2 * pr], wuv_ref[2 * pr]) + _dot(olat_scr[2 * pr + 1], wuv_ref[2 * pr + 1])
        o_ref[0, :, pr * LANES:(pr + 1) * LANES] = out.astype(o_ref.dtype)


def _dsa_attention(qp, idx, lat, q_gain, k_gain, wuv_ext, rel_bias, tq=256):
    b, s, nq = qp.shape
    assert DSA_GROUP == IDX_HEADS
    topk = min(TOPK_MAX, s // 4)
    nkb = s // tq
    n_idx = idx.shape[-1]
    tri = (jnp.arange(tq)[:, None] >= jnp.arange(tq)[None, :]).astype(BF16)
    kern = functools.partial(_dsa_kernel, tq=tq, topk=topk, thresholds=tuple(_t5_bucket_thresholds()))
    return pl.pallas_call(
        kern,
        grid=(b, s // tq),
        in_specs=[
            pl.BlockSpec(memory_space=pltpu.SMEM),
            pl.BlockSpec((1, tq, nq), lambda i, j: (i, j, 0)),
            pl.BlockSpec((1, tq, n_idx), lambda i, j: (i, j, 0)),
            pl.BlockSpec((1, s, LANES), lambda i, j: (i, 0, 0)),
            pl.BlockSpec((1, s, LANES), lambda i, j: (i, 0, IDX_HEADS * IDX_DIM // LANES)),
            pl.BlockSpec((1, DSA_LATENT), lambda i, j: (0, 0)),
            pl.BlockSpec((1, DSA_LATENT), lambda i, j: (0, 0)),
            pl.BlockSpec((DSA_HEADS, DSA_LATENT, LANES), lambda i, j: (0, 0, 0)),
            pl.BlockSpec((tq, tq), lambda i, j: (0, 0)),
        ],
        out_specs=pl.BlockSpec((1, tq, DSA_HEADS * DSA_V_DIM), lambda i, j: (i, j, 0)),
        out_shape=jax.ShapeDtypeStruct((b, s, DSA_HEADS * DSA_V_DIM), BF16),
        scratch_shapes=[
            pltpu.VMEM((s, DSA_LATENT), BF16),
            pltpu.VMEM((nkb, DSA_LATENT + V_PAD, tq), BF16),
            pltpu.VMEM((nkb, tq, tq), I32),
            pltpu.VMEM((nkb + 2 * DSA_HEADS + 1, tq, tq), F32),
            pltpu.VMEM((DSA_HEADS * tq, DSA_LATENT), BF16),
            pltpu.VMEM((IDX_HEADS * tq, LANES), BF16),
            pltpu.VMEM((DSA_HEADS, tq, DSA_LATENT), BF16),
            pltpu.VMEM((2, DSA_HEADS, tq, tq), F32),
            pltpu.VMEM((tq, DSA_GROUP * tq), F32),
            pltpu.VMEM((tq, DSA_GROUP * tq), F32),
            pltpu.VMEM((tq, DSA_GROUP * tq), BF16),
            pltpu.VMEM((tq, DSA_GROUP * tq), BF16),
            pltpu.VMEM((DSA_GROUP, DSA_LATENT + V_PAD, tq), F32),
        ],
        compiler_params=_cparams("arbitrary", "arbitrary"),
        name="dsa_attn",
    )(rel_bias, qp, idx, lat, idx, q_gain.reshape(1, -1), k_gain.reshape(1, -1), wuv_ext, tri)


def _dsa_weights(w_in, w_uv):
    o1 = DSA_HEADS * DSA_LATENT
    o2 = o1 + DSA_LATENT
    o3 = o2 + IDX_HEADS * IDX_DIM
    o4 = o3 + IDX_DIM
    d = w_in.shape[0]
    w_ki = w_in[:, o3:o4]
    w_wi = w_in[:, o4:]
    pad = jnp.zeros((d, LANES - IDX_HEADS), w_in.dtype)
    w = jnp.concatenate([w_in[:, :o3], w_ki, w_ki, w_wi, pad], axis=1).astype(BF16)
    zeros = jnp.zeros_like(w_uv)
    even = jnp.concatenate([w_uv, zeros], axis=-1)
    odd = jnp.concatenate([zeros, w_uv], axis=-1)
    is_even = (jnp.arange(DSA_HEADS) % 2 == 0)[:, None, None]
    wuv_ext = jnp.where(is_even, even, odd).astype(BF16)
    splits = (o1, DSA_LATENT, IDX_HEADS * IDX_DIM + 2 * LANES)
    return w, wuv_ext, splits


def kernel(x, c, ada_w, ada_b, norm_mix, norm_ffn, sb_w_in, sb_w_out, dsa_w_in, dsa_q_norm,
           dsa_k_norm, dsa_w_uv, dsa_w_out, rel_bias, ffn_w_up, ffn_conv_w, ffn_conv_b, ffn_w_down):
    depth = ada_w.shape[0]
    d = x.shape[-1]
    mod = _modulation(c, ada_w, ada_b)
    for i in range(depth):
        sh1, sc1, g1, sh2, sc2, g2 = [mod[i, :, k * d:(k + 1) * d] for k in range(6)]
        j = i // 2
        if i % 2 == 0:
            w_in = sb_w_in[j].astype(BF16)
            (qkv,) = _norm_proj(x, norm_mix[i], sc1, sh1, w_in, (w_in.shape[1],))
            o = _sb_attention(qkv)
            w_out = sb_w_out[j]
        else:
            w_in, wuv_ext, splits = _dsa_weights(dsa_w_in[j], dsa_w_uv[j])
            qp, lat, idx = _norm_proj(x, norm_mix[i], sc1, sh1, w_in, splits)
            o = _dsa_attention(qp, idx, lat, dsa_q_norm[j], dsa_k_norm[j], wuv_ext, rel_bias)
            w_out = dsa_w_out[j]
        x = _mix_ffn(x, o, w_out.astype(BF16), g1, norm_ffn[i], sc2, sh2, g2, ffn_w_up[i].astype(BF16),
                     ffn_conv_w[i], ffn_conv_b[i], ffn_w_down[i].astype(BF16))
    return x
```

```python
import functools
import math

import numpy as np
import jax
import jax.numpy as jnp
from jax import lax
from jax.experimental import pallas as pl
from jax.experimental.pallas import tpu as pltpu

F32 = jnp.float32
BF16 = jnp.bfloat16
I32 = jnp.int32

LANES = 128
SUBLANES = 8
MXU_DIM = 256
VMEM_LIMIT = 56 * 1024 * 1024

RMS_EPS = 1e-6
NEG = -1e30
LOG2E = math.log2(math.e)
SB_EXIT = -151.0
INT_MIN = -(2 ** 31)

SB_HEADS = 16
SB_HEAD_DIM = 64
DSA_HEADS = 16
DSA_LATENT = 128
DSA_V_DIM = 64
V_PAD = 16
DSA_GROUP = 8
IDX_HEADS = 8
IDX_DIM = 64
TOPK_MAX = 256
NUM_BUCKETS = 32
MAX_DISTANCE = 128
CONV_W = 3


def _cparams(*sem):
    return pltpu.CompilerParams(dimension_semantics=sem, vmem_limit_bytes=VMEM_LIMIT)


def _dot(a, b):
    return jnp.dot(a, b, preferred_element_type=F32)


def _dot_nt(a, b):
    return lax.dot_general(a, b, (((1,), (1,)), ((), ())), preferred_element_type=F32)


def _split_bf16(v):
    hi = v.astype(BF16)
    lo = (v - hi.astype(F32)).astype(BF16)
    return hi, lo


def _mod_kernel(c_ref, w_ref, b_ref, o_ref):
    c = c_ref[...]
    cond = c * (1.0 / (1.0 + jnp.exp(-c)))
    ch, cl = _split_bf16(cond)
    wh, wl = _split_bf16(w_ref[0])
    o_ref[0] = _dot(ch, wh) + _dot(ch, wl) + _dot(cl, wh) + b_ref[0]


def _modulation(c, ada_w, ada_b):
    depth, d, n = ada_w.shape
    b = c.shape[0]
    tn = 1024
    return pl.pallas_call(
        _mod_kernel,
        grid=(depth, n // tn),
        in_specs=[
            pl.BlockSpec((b, d), lambda i, j: (0, 0)),
            pl.BlockSpec((1, d, tn), lambda i, j: (i, 0, j)),
            pl.BlockSpec((1, 1, tn), lambda i, j: (i, 0, j)),
        ],
        out_specs=pl.BlockSpec((1, b, tn), lambda i, j: (i, 0, j)),
        out_shape=jax.ShapeDtypeStruct((depth, b, n), F32),
        compiler_params=_cparams("arbitrary", "arbitrary"),
        name="mod",
    )(c, ada_w, ada_b.reshape(depth, 1, n))


def _modulated_norm(x, g, sc, sh):
    ms = jnp.mean(x * x, axis=-1, keepdims=True)
    return (x * lax.rsqrt(ms + RMS_EPS) * g) * (1.0 + sc) + sh


def _norm_proj_kernel(x_ref, g_ref, sc_ref, sh_ref, w_ref, *o_refs, splits, chunk):
    hb = _modulated_norm(x_ref[0], g_ref[...], sc_ref[0], sh_ref[0]).astype(BF16)
    off = 0
    for o_ref, n in zip(o_refs, splits):
        for c0 in range(0, n, chunk):
            c1 = min(n, c0 + chunk)
            o_ref[0, :, c0:c1] = _dot(hb, w_ref[:, off + c0:off + c1]).astype(o_ref.dtype)
        off += n


def _norm_proj(x, g, sc, sh, w, splits, tm=2 * MXU_DIM):
    b, s, d = x.shape
    n = w.shape[1]
    assert sum(splits) == n
    kern = functools.partial(_norm_proj_kernel, splits=tuple(splits), chunk=512)
    return pl.pallas_call(
        kern,
        grid=(b, s // tm),
        in_specs=[
            pl.BlockSpec((1, tm, d), lambda i, j: (i, j, 0)),
            pl.BlockSpec((1, d), lambda i, j: (0, 0)),
            pl.BlockSpec((1, 1, d), lambda i, j: (i, 0, 0)),
            pl.BlockSpec((1, 1, d), lambda i, j: (i, 0, 0)),
            pl.BlockSpec((d, n), lambda i, j: (0, 0)),
        ],
        out_specs=[pl.BlockSpec((1, tm, m), lambda i, j: (i, j, 0)) for m in splits],
        out_shape=[jax.ShapeDtypeStruct((b, s, m), BF16) for m in splits],
        compiler_params=_cparams("arbitrary", "arbitrary"),
        name="norm_proj",
    )(x, g.reshape(1, d), sc.reshape(b, 1, d), sh.reshape(b, 1, d), w)


def _ffn_kernel(x_ref, o_ref, wo_ref, mgate_ref, g_ref, sc_ref, sh_ref, gate_ref, wu_ref, cw_ref, cb_ref,
                wd_ref, y_ref, act_scr, carry_scr, *, fc):
    si = pl.program_id(1)
    tm = act_scr.shape[0]
    f = act_scr.shape[1]
    nf = f // fc

    @pl.when(si == 0)
    def _():
        carry_scr[...] = jnp.zeros_like(carry_scr)

    x1 = x_ref[0] + mgate_ref[0] * _dot(o_ref[0], wo_ref[...])
    hb = _modulated_norm(x1, g_ref[...], sc_ref[0], sh_ref[0]).astype(BF16)
    row = lax.broadcasted_iota(I32, (tm, 1), 0)

    def up(c):
        return (_dot(hb, wu_ref[:, c * fc:(c + 1) * fc]), _dot(hb, wu_ref[:, f + c * fc:f + (c + 1) * fc]))

    def conv(u, col0):
        prev = carry_scr[:, col0:col0 + fc]
        carry_scr[:, col0:col0 + fc] = u[tm - SUBLANES:, :]
        p1 = prev[SUBLANES - 1:SUBLANES, :]
        p2 = prev[SUBLANES - 2:SUBLANES - 1, :]
        u1 = jnp.where(row == 0, p1, pltpu.roll(u, 1, 0))
        u2 = jnp.where(row == 0, p2, jnp.where(row == 1, p1, pltpu.roll(u, 2, 0)))
        cw = cw_ref[:, col0:col0 + fc]
        return cb_ref[:, col0:col0 + fc] + u2 * cw[0:1, :] + u1 * cw[1:2, :] + u * cw[2:3, :]

    nxt = up(0)
    for c in range(nf):
        ug, uv = nxt
        if c + 1 < nf:
            nxt = up(c + 1)
        yg = conv(ug, c * fc)
        yv = conv(uv, f + c * fc)
        act_scr[:, c * fc:(c + 1) * fc] = (yg * (1.0 / (1.0 + jnp.exp(-yg))) * yv).astype(BF16)

    y_ref[0] = x1 + gate_ref[0] * _dot(act_scr[...], wd_ref[...])


def _mix_ffn(x, o, w_out, mgate, g, sc, sh, gate, w_up, conv_w, conv_b, w_down, tm=2 * MXU_DIM, fc=MXU_DIM):
    b, s, d = x.shape
    f = w_down.shape[0]
    k = o.shape[-1]
    kern = functools.partial(_ffn_kernel, fc=fc)
    vec = lambda a: a.reshape(b, 1, d)
    const = lambda i, j: (0, 0)
    return pl.pallas_call(
        kern,
        grid=(b, s // tm),
        in_specs=[
            pl.BlockSpec((1, tm, d), lambda i, j: (i, j, 0)),
            pl.BlockSpec((1, tm, k), lambda i, j: (i, j, 0)),
            pl.BlockSpec((k, d), const),
            pl.BlockSpec((1, 1, d), lambda i, j: (i, 0, 0)),
            pl.BlockSpec((1, d), const),
            pl.BlockSpec((1, 1, d), lambda i, j: (i, 0, 0)),
            pl.BlockSpec((1, 1, d), lambda i, j: (i, 0, 0)),
            pl.BlockSpec((1, 1, d), lambda i, j: (i, 0, 0)),
            pl.BlockSpec((d, 2 * f), const),
            pl.BlockSpec((CONV_W, 2 * f), const),
            pl.BlockSpec((1, 2 * f), const),
            pl.BlockSpec((f, d), const),
        ],
        out_specs=pl.BlockSpec((1, tm, d), lambda i, j: (i, j, 0)),
        out_shape=jax.ShapeDtypeStruct((b, s, d), F32),
        scratch_shapes=[
            pltpu.VMEM((tm, f), BF16),
            pltpu.VMEM((SUBLANES, 2 * f), F32),
        ],
        compiler_params=_cparams("arbitrary", "arbitrary"),
        name="ffn",
    )(x, o, w_out, vec(mgate), g.reshape(1, d), vec(sc), vec(sh), vec(gate), w_up, conv_w,
      conv_b.reshape(1, 2 * f), w_down)


def _sb_kernel(q_ref, k_ref, v_ref, tri_ref, o_ref, acc_scr, r_scr, *, tq):
    qi = pl.program_id(2)
    npair = acc_scr.shape[0]
    lane = lax.broadcasted_iota(I32, (1, LANES), 1)
    halves = (lane < SB_HEAD_DIM, lane >= SB_HEAD_DIM)
    tri2 = tri_ref[...]
    row = lax.broadcasted_iota(I32, (2 * tq, tq), 0)
    col = lax.broadcasted_iota(I32, (2 * tq, tq), 1)
    strict = col < jnp.where(row >= tq, row - tq, row)
    has_prev = qi >= 1
    jp = jnp.maximum(qi - 1, 0)

    def log_keep(z, masked):
        lk = -(jnp.maximum(z, 0.0) + jnp.log2(1.0 + jnp.exp2(-jnp.abs(z))))
        return jnp.where(strict, lk, 0.0) if masked else lk

    def suffix(lk):
        hi, lo = _split_bf16(lk)
        return _dot(jnp.concatenate([hi, lo], axis=1), tri2)

    def weights(z, incl, r, masked):
        a = jnp.exp2(z + incl + r)
        return jnp.where(strict, a, 0.0) if masked else a

    def pair_fns(pp):
        lanes = slice(pp * LANES, (pp + 1) * LANES)
        q2 = (q_ref[0, :, lanes].astype(F32) * (SB_HEAD_DIM ** -0.5 * LOG2E)).astype(BF16)
        zero = jnp.zeros_like(q2)
        qstack = jnp.concatenate([jnp.where(hm, q2, zero) for hm in halves], axis=0)

        def qk(jb):
            start = pl.multiple_of(jb * tq, tq)
            return _dot_nt(qstack, k_ref[0, pl.ds(start, tq), lanes])

        def av(a, jb):
            start = pl.multiple_of(jb * tq, tq)
            vb = v_ref[0, pl.ds(start, tq), lanes]
            vstack = jnp.concatenate([jnp.where(hm, vb, jnp.zeros_like(vb)) for hm in halves], axis=0)
            a2 = jnp.concatenate([a[:tq], a[tq:]], axis=1).astype(BF16)
            return _dot(a2, vstack)

        return qk, av

    fns = [pair_fns(pp) for pp in range(npair)]
    zs = {0: (fns[0][0](qi), fns[0][0](jp))}
    for pp in range(npair):
        qk, av = fns[pp]
        if pp + 1 < npair:
            zs[pp + 1] = (fns[pp + 1][0](qi), fns[pp + 1][0](jp))
        z_d, z_p = zs.pop(pp)
        incl_d = suffix(log_keep(z_d, True))
        incl_p = suffix(log_keep(z_p, False))
        r0 = jnp.zeros((2 * tq, 1), F32)
        acc = av(weights(z_d, incl_d, r0, True), qi)
        r1 = incl_d[:, 0:1]
        acc = acc + av(jnp.where(has_prev, weights(z_p, incl_p, r1, False), 0.0), jp)
        acc_scr[pp] = acc
        r_scr[pp] = r1 + jnp.where(has_prev, incl_p[:, 0:1], 0.0)

    def live(r):
        return jnp.max(r) > SB_EXIT

    for pp in range(npair):
        qk, av = pair_fns(pp)

        def step(state, qk=qk, av=av):
            i, acc, r, _ = state
            jb = qi - 2 - i
            z = qk(jb)
            incl = suffix(log_keep(z, False))
            acc = acc + av(weights(z, incl, r, False), jb)
            r = r + incl[:, 0:1]
            return i + 1, acc, r, live(r)

        r2 = r_scr[pp]
        _, acc, _, _ = lax.while_loop(lambda st: (st[0] < qi - 1) & st[3], step,
                                      (jnp.int32(0), acc_scr[pp], r2, live(r2)))
        o_ref[0, :, pp * LANES:(pp + 1) * LANES] = acc.astype(o_ref.dtype)


def _sb_attention(qkv, tq=MXU_DIM, pairs_per_step=4):
    b, s, n3 = qkv.shape
    n = n3 // 3
    gw = pairs_per_step * LANES
    ngrp = n // gw
    tri = (jnp.arange(tq)[:, None] >= jnp.arange(tq)[None, :]).astype(BF16)
    tri = jnp.concatenate([tri, tri], axis=0)
    kern = functools.partial(_sb_kernel, tq=tq)
    return pl.pallas_call(
        kern,
        grid=(b, ngrp, s // tq),
        in_specs=[
            pl.BlockSpec((1, tq, gw), lambda i, p, j: (i, j, p)),
            pl.BlockSpec((1, s, gw), lambda i, p, j: (i, 0, ngrp + p)),
            pl.BlockSpec((1, s, gw), lambda i, p, j: (i, 0, 2 * ngrp + p)),
            pl.BlockSpec((2 * tq, tq), lambda i, p, j: (0, 0)),
        ],
        out_specs=pl.BlockSpec((1, tq, gw), lambda i, p, j: (i, j, p)),
        out_shape=jax.ShapeDtypeStruct((b, s, n), BF16),
        scratch_shapes=[
            pltpu.VMEM((pairs_per_step, tq, LANES), F32),
            pltpu.VMEM((pairs_per_step, 2 * tq, 1), F32),
        ],
        compiler_params=_cparams("arbitrary", "arbitrary", "arbitrary"),
        name="sb_attn",
    )(qkv, qkv, qkv, tri)


def _t5_bucket_thresholds():
    max_exact = NUM_BUCKETS // 2
    n = np.arange(0, 4 * MAX_DISTANCE, dtype=np.int32)
    nf = np.maximum(n, 1).astype(np.float32)
    large = max_exact + (np.log(nf / np.float32(max_exact)) / np.float32(math.log(MAX_DISTANCE / max_exact))
                         * np.float32(NUM_BUCKETS - max_exact)).astype(np.int32)
    large = np.minimum(large, NUM_BUCKETS - 1)
    bucket = np.where(n < max_exact, n, large)
    assert np.all(np.diff(bucket) >= 0) and bucket[-1] == NUM_BUCKETS - 1
    return [int(np.argmax(bucket >= bb)) for bb in range(max_exact + 1, NUM_BUCKETS)]


def _sortable_key(v):
    v = jnp.where(v == 0.0, 0.0, v)
    bits = lax.bitcast_convert_type(v, I32)
    return bits ^ ((bits >> 31) & 0x7FFFFFFF)


def _tree_sum(parts):
    parts = list(parts)
    while len(parts) > 1:
        parts = [parts[i] + parts[i + 1] if i + 1 < len(parts) else parts[i] for i in range(0, len(parts), 2)]
    return parts[0]


def _dsa_kernel(rb_ref, q_ref, qidx_ref, lat_ref, kidx_ref, qg_ref, kg_ref, wuv_ref, tri_ref,
                o_ref, kn_scr, vt_scr, key_scr, add_scr, qn_scr, qis_scr, olat_scr, bias_scr,
                lg_a, lg_b, p_a, p_b, acc_scr, *, tq, topk, thresholds):
    bi = pl.program_id(0)
    qi = pl.program_id(1)
    nkb = vt_scr.shape[0]
    krow = lax.broadcasted_iota(I32, (tq, tq), 0)
    qcol = lax.broadcasted_iota(I32, (tq, tq), 1)
    qcol1 = lax.broadcasted_iota(I32, (1, tq), 1)

    @pl.when((bi == 0) & (qi == 0))
    def _():
        add_scr[nkb + 2 * DSA_HEADS] = jnp.full((tq, tq), NEG, F32)
        max_exact = NUM_BUCKETS // 2
        for near in range(2):
            dist = qcol - krow + near * tq
            large = jnp.full((tq, tq), max_exact, I32)
            for th in thresholds:
                large = large + jnp.where(dist >= th, 1, 0)
            bucket = jnp.where(dist < max_exact, jnp.maximum(dist, 0), large)

            def head_tile(h, _, near=near, bucket=bucket):
                far = rb_ref[NUM_BUCKETS - 1, h]
                tile = jnp.zeros((tq, tq), F32)
                for bb in range(NUM_BUCKETS - 1):
                    tile = jnp.where(bucket == bb, (rb_ref[bb, h] - far) * LOG2E, tile)
                bias_scr[near, h] = tile
                return 0

            lax.fori_loop(0, DSA_HEADS, head_tile, 0)

    @pl.when(qi == 0)
    def _():
        ext_row = lax.broadcasted_iota(I32, (V_PAD, tq), 0)
        for jb in range(nkb):
            lat = lat_ref[0, jb * tq:(jb + 1) * tq, :].astype(F32)
            ms = jnp.mean(lat * lat, axis=-1, keepdims=True)
            kn_scr[jb * tq:(jb + 1) * tq, :] = (lat * lax.rsqrt(ms + RMS_EPS) * kg_ref[...]).astype(BF16)
            vt_scr[jb, :DSA_LATENT, :] = lat.T.astype(BF16)
            vt_scr[jb, DSA_LATENT:, :] = jnp.where(ext_row == 0, 1.0, 0.0).astype(BF16)

    scale = DSA_LATENT ** -0.5 * LOG2E
    for h in range(DSA_HEADS):
        qh = q_ref[0, :, h * DSA_LATENT:(h + 1) * DSA_LATENT].astype(F32)
        ms = jnp.mean(qh * qh, axis=-1, keepdims=True)
        qn_scr[h * tq:(h + 1) * tq, :] = (qh * lax.rsqrt(ms + RMS_EPS) * (qg_ref[...] * scale)).astype(BF16)

    lane = lax.broadcasted_iota(I32, (1, LANES), 1)
    halves = (lane < IDX_DIM, lane >= IDX_DIM)
    n_qi = IDX_HEADS * IDX_DIM
    wi = qidx_ref[0, :, n_qi + LANES:n_qi + 2 * LANES].astype(F32) * (IDX_HEADS ** -0.5)
    wit = wi.T
    for hh in range(IDX_HEADS):
        blk = qidx_ref[0, :, (hh // 2) * LANES:(hh // 2 + 1) * LANES]
        qis_scr[hh * tq:(hh + 1) * tq, :] = jnp.where(halves[hh % 2], blk, jnp.zeros_like(blk))

    def causal(jb):
        return krow + jb * tq <= qcol + qi * tq

    def score_matmul(t, sc_ref):
        start = pl.multiple_of(jnp.minimum(t, qi) * tq, tq)
        sc_ref[...] = _dot_nt(kidx_ref[0, pl.ds(start, tq), :], qis_scr[...])

    def score_keys(t, sc_ref):
        jb = jnp.minimum(t, qi)
        isc = jnp.zeros((tq, tq), F32)
        for hh in range(IDX_HEADS):
            isc = isc + jnp.maximum(sc_ref[:, hh * tq:(hh + 1) * tq], 0.0) * wit[hh:hh + 1, :]
        key_scr[jb] = jnp.where(causal(jb), _sortable_key(isc), INT_MIN)

    score_matmul(0, lg_a)

    def score_pair(u, _):
        t0 = 2 * u
        score_matmul(t0 + 1, lg_b)
        score_keys(t0, lg_a)
        score_matmul(t0 + 2, lg_a)
        score_keys(t0 + 1, lg_b)
        return 0

    lax.fori_loop(0, (qi + 2) // 2, score_pair, 0)

    def count(pred):
        def body(jb, part):
            c = jnp.where(pred(key_scr[jb]), 1.0, 0.0)
            return part + _tree_sum(c[g * SUBLANES:(g + 1) * SUBLANES, :] for g in range(tq // SUBLANES))
        part = lax.fori_loop(0, qi + 1, body, jnp.zeros((SUBLANES, tq), F32))
        return jnp.sum(part, axis=0, keepdims=True)

    kf = float(topk)
    c0 = count(lambda k: k >= 0)
    thr = jnp.where(c0 >= kf, 0, INT_MIN).astype(I32)
    cnt = jnp.where(c0 >= kf, c0, ((qi + 1) * tq).astype(F32))

    def bisect(i, state):
        thr, cnt = state
        cand = thr | lax.shift_left(jnp.int32(1), jnp.int32(30) - i)
        c = count(lambda k: k >= cand)
        return jnp.where(c >= kf, cand, thr), jnp.where(c >= kf, c, cnt)

    thr, cnt = lax.fori_loop(0, 31, bisect, (thr, cnt))
    short = qcol1 + qi * tq + 1 <= topk
    thr = jnp.where(short, INT_MIN, thr)
    straddle = jnp.max(jnp.where(short, 0.0, cnt - kf)) > 0.0

    @pl.when(jnp.logical_not(straddle))
    def _():
        def far_mask(jb, _):
            add_scr[jb] = jnp.where(key_scr[jb] >= thr, 0.0, NEG)
            return 0
        lax.fori_loop(0, qi, far_mask, 0)
        add_scr[qi] = jnp.where((key_scr[qi] >= thr) & causal(qi), 0.0, NEG)

    @pl.when(straddle)
    def _():
        need = kf - count(lambda k: k > thr)
        tri = tri_ref[...]

        def mask_block(jb, eq_before):
            key = key_scr[jb]
            eq = key == thr
            rank = _dot(tri, jnp.where(eq, 1.0, 0.0).astype(BF16)) + eq_before
            sel = ((key > thr) | (eq & (rank <= need))) & causal(jb)
            add_scr[jb] = jnp.where(sel, 0.0, NEG)
            return rank[tq - 1:tq, :]

        lax.fori_loop(0, qi + 1, mask_block, jnp.zeros((1, tq), F32))

    prev = jnp.maximum(qi - 1, 0)
    prev_pen = jnp.where(qi >= 1, 0.0, NEG)
    nblk = prev + 2

    def near_tiles(h, _):
        add_scr[nkb + h] = add_scr[prev] + bias_scr[1, h] + prev_pen
        add_scr[nkb + DSA_HEADS + h] = add_scr[qi] + bias_scr[0, h]
        return 0

    lax.fori_loop(0, DSA_HEADS, near_tiles, 0)

    def kv_block(t):
        t = jnp.maximum(t, 0)
        return jnp.minimum(jnp.where(t < prev, t, jnp.where(t == prev, prev, qi)), qi)

    def add_index(t, h):
        return jnp.where(t < prev, t,
                         jnp.where(t == prev, nkb + h,
                                   jnp.where(t == prev + 1, nkb + DSA_HEADS + h, nkb + 2 * DSA_HEADS)))

    gw = DSA_GROUP * tq

    def head_group(g, _):
        row0 = pl.multiple_of(g * gw, gw)
        qg = qn_scr[pl.ds(row0, gw), :]

        def stage_qk(t, lk_ref):
            start = pl.multiple_of(kv_block(t) * tq, tq)
            lg = _dot_nt(kn_scr[pl.ds(start, tq), :], qg)
            tops = []
            for k in range(DSA_GROUP):
                lk = lg[:, k * tq:(k + 1) * tq] + add_scr[add_index(t, g * DSA_GROUP + k)]
                lk_ref[:, k * tq:(k + 1) * tq] = lk
                tops.append(jnp.max(lk, axis=0, keepdims=True))
            return tuple(tops)

        def stage_softmax(lk_ref, p_ref, stats, tops):
            new_stats, alphas = [], []
            for k, (m, top) in enumerate(zip(stats, tops)):
                m_new = jnp.maximum(m, top)
                alphas.append(jnp.exp2(m - m_new))
                p_ref[:, k * tq:(k + 1) * tq] = jnp.exp2(lk_ref[:, k * tq:(k + 1) * tq] - m_new).astype(BF16)
                new_stats.append(m_new)
            return tuple(new_stats), tuple(alphas)

        def stage_pv(t, p_ref, alphas):
            pv = _dot(vt_scr[kv_block(t)], p_ref[...])
            for k, alpha in enumerate(alphas):
                acc_scr[k] = alpha * acc_scr[k] + pv[:, k * tq:(k + 1) * tq]

        stats = (jnp.full((1, tq), 0.5 * NEG, F32),) * DSA_GROUP
        ones = (jnp.ones((1, tq), F32),) * DSA_GROUP
        acc_scr[...] = jnp.zeros_like(acc_scr)
        p_b[...] = jnp.zeros_like(p_b)
        tops_a = stage_qk(0, lg_a)

        def pair(u, carry):
            stats, alphas, tops_a = carry
            t0 = 2 * u
            tops_b = stage_qk(t0 + 1, lg_b)
            stage_pv(t0 - 1, p_b, alphas)
            stats, alphas = stage_softmax(lg_a, p_a, stats, tops_a)
            tops_a = stage_qk(t0 + 2, lg_a)
            stage_pv(t0, p_a, alphas)
            stats, alphas = stage_softmax(lg_b, p_b, stats, tops_b)
            return stats, alphas, tops_a

        npair = (nblk + 1) // 2
        stats, alphas, _ = lax.fori_loop(0, npair, pair, (stats, ones, tops_a))
        stage_pv(2 * npair - 1, p_b, alphas)
        for k in range(DSA_GROUP):
            acc = acc_scr[k]
            l = acc[DSA_LATENT:DSA_LATENT + 1, :]
            olat_scr[g * DSA_GROUP + k] = (acc[:DSA_LATENT, :] * (1.0 / l)).T.astype(BF16)
        return 0

    lax.fori_loop(0, DSA_HEADS // DSA_GROUP, head_group, 0)

    for pr in range(DSA_HEADS // 2):
        out = _dot(olat_scr[2 * pr], wuv_ref[2 * pr]) + _dot(olat_scr[2 * pr + 1], wuv_ref[2 * pr + 1])
        o_ref[0, :, pr * LANES:(pr + 1) * LANES] = out.astype(o_ref.dtype)


def _dsa_attention(qp, idx, lat, q_gain, k_gain, wuv_ext, rel_bias, tq=MXU_DIM):
    b, s, nq = qp.shape
    assert DSA_GROUP == IDX_HEADS
    topk = min(TOPK_MAX, s // 4)
    nkb = s // tq
    n_idx = idx.shape[-1]
    tri = (jnp.arange(tq)[:, None] >= jnp.arange(tq)[None, :]).astype(BF16)
    kern = functools.partial(_dsa_kernel, tq=tq, topk=topk, thresholds=tuple(_t5_bucket_thresholds()))
    return pl.pallas_call(
        kern,
        grid=(b, s // tq),
        in_specs=[
            pl.BlockSpec(memory_space=pltpu.SMEM),
            pl.BlockSpec((1, tq, nq), lambda i, j: (i, j, 0)),
            pl.BlockSpec((1, tq, n_idx), lambda i, j: (i, j, 0)),
            pl.BlockSpec((1, s, LANES), lambda i, j: (i, 0, 0)),
            pl.BlockSpec((1, s, LANES), lambda i, j: (i, 0, IDX_HEADS * IDX_DIM // LANES)),
            pl.BlockSpec((1, DSA_LATENT), lambda i, j: (0, 0)),
            pl.BlockSpec((1, DSA_LATENT), lambda i, j: (0, 0)),
            pl.BlockSpec((DSA_HEADS, DSA_LATENT, LANES), lambda i, j: (0, 0, 0)),
            pl.BlockSpec((tq, tq), lambda i, j: (0, 0)),
        ],
        out_specs=pl.BlockSpec((1, tq, DSA_HEADS * DSA_V_DIM), lambda i, j: (i, j, 0)),
        out_shape=jax.ShapeDtypeStruct((b, s, DSA_HEADS * DSA_V_DIM), BF16),
        scratch_shapes=[
            pltpu.VMEM((s, DSA_LATENT), BF16),
            pltpu.VMEM((nkb, DSA_LATENT + V_PAD, tq), BF16),
            pltpu.VMEM((nkb, tq, tq), I32),
            pltpu.VMEM((nkb + 2 * DSA_HEADS + 1, tq, tq), F32),
            pltpu.VMEM((DSA_HEADS * tq, DSA_LATENT), BF16),
            pltpu.VMEM((IDX_HEADS * tq, LANES), BF16),
            pltpu.VMEM((DSA_HEADS, tq, DSA_LATENT), BF16),
            pltpu.VMEM((2, DSA_HEADS, tq, tq), F32),
            pltpu.VMEM((tq, DSA_GROUP * tq), F32),
            pltpu.VMEM((tq, DSA_GROUP * tq), F32),
            pltpu.VMEM((tq, DSA_GROUP * tq), BF16),
            pltpu.VMEM((tq, DSA_GROUP * tq), BF16),
            pltpu.VMEM((DSA_GROUP, DSA_LATENT + V_PAD, tq), F32),
        ],
        compiler_params=_cparams("arbitrary", "arbitrary"),
        name="dsa_attn",
    )(rel_bias, qp, idx, lat, idx, q_gain.reshape(1, -1), k_gain.reshape(1, -1), wuv_ext, tri)


def _dsa_weights(w_in, w_uv):
    o1 = DSA_HEADS * DSA_LATENT
    o2 = o1 + DSA_LATENT
    o3 = o2 + IDX_HEADS * IDX_DIM
    o4 = o3 + IDX_DIM
    d = w_in.shape[0]
    w_ki = w_in[:, o3:o4]
    w_wi = w_in[:, o4:]
    pad = jnp.zeros((d, LANES - IDX_HEADS), w_in.dtype)
    w = jnp.concatenate([w_in[:, :o3], w_ki, w_ki, w_wi, pad], axis=1).astype(BF16)
    zeros = jnp.zeros_like(w_uv)
    even = jnp.concatenate([w_uv, zeros], axis=-1)
    odd = jnp.concatenate([zeros, w_uv], axis=-1)
    is_even = (jnp.arange(DSA_HEADS) % 2 == 0)[:, None, None]
    wuv_ext = jnp.where(is_even, even, odd).astype(BF16)
    splits = (o1, DSA_LATENT, IDX_HEADS * IDX_DIM + 2 * LANES)
    return w, wuv_ext, splits


def kernel(x, c, ada_w, ada_b, norm_mix, norm_ffn, sb_w_in, sb_w_out, dsa_w_in, dsa_q_norm,
           dsa_k_norm, dsa_w_uv, dsa_w_out, rel_bias, ffn_w_up, ffn_conv_w, ffn_conv_b, ffn_w_down):
    depth = ada_w.shape[0]
    d = x.shape[-1]
    mod = _modulation(c, ada_w, ada_b)
    for i in range(depth):
        sh1, sc1, g1, sh2, sc2, g2 = [mod[i, :, k * d:(k + 1) * d] for k in range(6)]
        j = i // 2
        if i % 2 == 0:
            w_in = sb_w_in[j].astype(BF16)
            (qkv,) = _norm_proj(x, norm_mix[i], sc1, sh1, w_in, (w_in.shape[1],))
            o = _sb_attention(qkv)
            w_out = sb_w_out[j]
        else:
            w_in, wuv_ext, splits = _dsa_weights(dsa_w_in[j], dsa_w_uv[j])
            qp, lat, idx = _norm_proj(x, norm_mix[i], sc1, sh1, w_in, splits)
            o = _dsa_attention(qp, idx, lat, dsa_q_norm[j], dsa_k_norm[j], wuv_ext, rel_bias)
            w_out = dsa_w_out[j]
        x = _mix_ffn(x, o, w_out.astype(BF16), g1, norm_ffn[i], sc2, sh2, g2, ffn_w_up[i].astype(BF16),
                     ffn_conv_w[i], ffn_conv_b[i], ffn_w_down[i].astype(BF16))
    return x
```

```python
import functools
import math

import numpy as np
import jax
import jax.numpy as jnp
from jax import lax
from jax.experimental import pallas as pl
from jax.experimental.pallas import tpu as pltpu

F32 = jnp.float32
BF16 = jnp.bfloat16
I32 = jnp.int32

LANES = 128
SUBLANES = 8
MXU_DIM = 256
VMEM_LIMIT = 60 * 1024 * 1024

RMS_EPS = 1e-6
NEG = -1e30
LOG2E = math.log2(math.e)
SB_EXIT = -151.0
INT_MIN = -(2 ** 31)

SB_HEADS = 16
SB_HEAD_DIM = 64
DSA_HEADS = 16
DSA_LATENT = 128
DSA_V_DIM = 64
V_PAD = 16
DSA_GROUP = 16
IDX_HEADS = 8
IDX_DIM = 64
TOPK_MAX = 256
NUM_BUCKETS = 32
MAX_DISTANCE = 128
CONV_W = 3


def _cparams(*sem):
    return pltpu.CompilerParams(dimension_semantics=sem, vmem_limit_bytes=VMEM_LIMIT)


def _dot(a, b):
    return jnp.dot(a, b, preferred_element_type=F32)


def _dot_nt(a, b):
    return lax.dot_general(a, b, (((1,), (1,)), ((), ())), preferred_element_type=F32)


def _split_bf16(v):
    hi = v.astype(BF16)
    lo = (v - hi.astype(F32)).astype(BF16)
    return hi, lo


def _mod_kernel(c_ref, w_ref, b_ref, o_ref):
    c = c_ref[...]
    cond = c * (1.0 / (1.0 + jnp.exp(-c)))
    ch, cl = _split_bf16(cond)
    wh, wl = _split_bf16(w_ref[0])
    o_ref[0] = _dot(ch, wh) + _dot(ch, wl) + _dot(cl, wh) + b_ref[0]


def _modulation(c, ada_w, ada_b):
    depth, d, n = ada_w.shape
    b = c.shape[0]
    tn = 1024
    return pl.pallas_call(
        _mod_kernel,
        grid=(depth, n // tn),
        in_specs=[
            pl.BlockSpec((b, d), lambda i, j: (0, 0)),
            pl.BlockSpec((1, d, tn), lambda i, j: (i, 0, j)),
            pl.BlockSpec((1, 1, tn), lambda i, j: (i, 0, j)),
        ],
        out_specs=pl.BlockSpec((1, b, tn), lambda i, j: (i, 0, j)),
        out_shape=jax.ShapeDtypeStruct((depth, b, n), F32),
        compiler_params=_cparams("arbitrary", "arbitrary"),
        name="mod",
    )(c, ada_w, ada_b.reshape(depth, 1, n))


def _modulated_norm(x, g, sc, sh):
    ms = jnp.mean(x * x, axis=-1, keepdims=True)
    return (x * lax.rsqrt(ms + RMS_EPS) * g) * (1.0 + sc) + sh


def _norm_proj_kernel(x_ref, g_ref, sc_ref, sh_ref, w_ref, *o_refs, splits, chunk):
    hb = _modulated_norm(x_ref[0], g_ref[...], sc_ref[0], sh_ref[0]).astype(BF16)
    off = 0
    for o_ref, n in zip(o_refs, splits):
        for c0 in range(0, n, chunk):
            c1 = min(n, c0 + chunk)
            o_ref[0, :, c0:c1] = _dot(hb, w_ref[:, off + c0:off + c1]).astype(o_ref.dtype)
        off += n


def _norm_proj(x, g, sc, sh, w, splits, tm=2 * MXU_DIM):
    b, s, d = x.shape
    n = w.shape[1]
    assert sum(splits) == n
    kern = functools.partial(_norm_proj_kernel, splits=tuple(splits), chunk=512)
    return pl.pallas_call(
        kern,
        grid=(b, s // tm),
        in_specs=[
            pl.BlockSpec((1, tm, d), lambda i, j: (i, j, 0)),
            pl.BlockSpec((1, d), lambda i, j: (0, 0)),
            pl.BlockSpec((1, 1, d), lambda i, j: (i, 0, 0)),
            pl.BlockSpec((1, 1, d), lambda i, j: (i, 0, 0)),
            pl.BlockSpec((d, n), lambda i, j: (0, 0)),
        ],
        out_specs=[pl.BlockSpec((1, tm, m), lambda i, j: (i, j, 0)) for m in splits],
        out_shape=[jax.ShapeDtypeStruct((b, s, m), BF16) for m in splits],
        compiler_params=_cparams("arbitrary", "arbitrary"),
        name="norm_proj",
    )(x, g.reshape(1, d), sc.reshape(b, 1, d), sh.reshape(b, 1, d), w)


def _ffn_kernel(x_ref, o_ref, wo_ref, mgate_ref, g_ref, sc_ref, sh_ref, gate_ref, wu_ref, cw_ref, cb_ref,
                wd_ref, y_ref, act_scr, carry_scr, *, fc):
    si = pl.program_id(1)
    tm = act_scr.shape[0]
    f = act_scr.shape[1]
    nf = f // fc

    @pl.when(si == 0)
    def _():
        carry_scr[...] = jnp.zeros_like(carry_scr)

    x1 = x_ref[0] + mgate_ref[0] * _dot(o_ref[0], wo_ref[...])
    hb = _modulated_norm(x1, g_ref[...], sc_ref[0], sh_ref[0]).astype(BF16)
    row = lax.broadcasted_iota(I32, (tm, 1), 0)

    def up(c):
        return (_dot(hb, wu_ref[:, c * fc:(c + 1) * fc]), _dot(hb, wu_ref[:, f + c * fc:f + (c + 1) * fc]))

    def conv(u, col0):
        prev = carry_scr[:, col0:col0 + fc]
        carry_scr[:, col0:col0 + fc] = u[tm - SUBLANES:, :]
        p1 = prev[SUBLANES - 1:SUBLANES, :]
        p2 = prev[SUBLANES - 2:SUBLANES - 1, :]
        u1 = jnp.where(row == 0, p1, pltpu.roll(u, 1, 0))
        u2 = jnp.where(row == 0, p2, jnp.where(row == 1, p1, pltpu.roll(u, 2, 0)))
        cw = cw_ref[:, col0:col0 + fc]
        return cb_ref[:, col0:col0 + fc] + u2 * cw[0:1, :] + u1 * cw[1:2, :] + u * cw[2:3, :]

    nxt = up(0)
    for c in range(nf):
        ug, uv = nxt
        if c + 1 < nf:
            nxt = up(c + 1)
        yg = conv(ug, c * fc)
        yv = conv(uv, f + c * fc)
        act_scr[:, c * fc:(c + 1) * fc] = (yg * (1.0 / (1.0 + jnp.exp(-yg))) * yv).astype(BF16)

    y_ref[0] = x1 + gate_ref[0] * _dot(act_scr[...], wd_ref[...])


def _mix_ffn(x, o, w_out, mgate, g, sc, sh, gate, w_up, conv_w, conv_b, w_down, tm=2 * MXU_DIM, fc=MXU_DIM):
    b, s, d = x.shape
    f = w_down.shape[0]
    k = o.shape[-1]
    kern = functools.partial(_ffn_kernel, fc=fc)
    vec = lambda a: a.reshape(b, 1, d)
    const = lambda i, j: (0, 0)
    return pl.pallas_call(
        kern,
        grid=(b, s // tm),
        in_specs=[
            pl.BlockSpec((1, tm, d), lambda i, j: (i, j, 0)),
            pl.BlockSpec((1, tm, k), lambda i, j: (i, j, 0)),
            pl.BlockSpec((k, d), const),
            pl.BlockSpec((1, 1, d), lambda i, j: (i, 0, 0)),
            pl.BlockSpec((1, d), const),
            pl.BlockSpec((1, 1, d), lambda i, j: (i, 0, 0)),
            pl.BlockSpec((1, 1, d), lambda i, j: (i, 0, 0)),
            pl.BlockSpec((1, 1, d), lambda i, j: (i, 0, 0)),
            pl.BlockSpec((d, 2 * f), const),
            pl.BlockSpec((CONV_W, 2 * f), const),
            pl.BlockSpec((1, 2 * f), const),
            pl.BlockSpec((f, d), const),
        ],
        out_specs=pl.BlockSpec((1, tm, d), lambda i, j: (i, j, 0)),
        out_shape=jax.ShapeDtypeStruct((b, s, d), F32),
        scratch_shapes=[
            pltpu.VMEM((tm, f), BF16),
            pltpu.VMEM((SUBLANES, 2 * f), F32),
        ],
        compiler_params=_cparams("arbitrary", "arbitrary"),
        name="ffn",
    )(x, o, w_out, vec(mgate), g.reshape(1, d), vec(sc), vec(sh), vec(gate), w_up, conv_w,
      conv_b.reshape(1, 2 * f), w_down)


def _sb_kernel(q_ref, k_ref, v_ref, tri_ref, o_ref, acc_scr, r_scr, *, tq):
    qi = pl.program_id(2)
    npair = acc_scr.shape[0]
    lane = lax.broadcasted_iota(I32, (1, LANES), 1)
    halves = (lane < SB_HEAD_DIM, lane >= SB_HEAD_DIM)
    tri2 = tri_ref[...]
    row = lax.broadcasted_iota(I32, (2 * tq, tq), 0)
    col = lax.broadcasted_iota(I32, (2 * tq, tq), 1)
    strict = col < jnp.where(row >= tq, row - tq, row)
    has_prev = qi >= 1
    jp = jnp.maximum(qi - 1, 0)

    def log_keep(z, masked):
        lk = -(jnp.maximum(z, 0.0) + jnp.log2(1.0 + jnp.exp2(-jnp.abs(z))))
        return jnp.where(strict, lk, 0.0) if masked else lk

    def suffix(lk):
        hi, lo = _split_bf16(lk)
        return _dot(jnp.concatenate([hi, lo], axis=1), tri2)

    def weights(z, incl, r, masked):
        a = jnp.exp2(z + incl + r)
        return jnp.where(strict, a, 0.0) if masked else a

    def pair_fns(pp):
        lanes = slice(pp * LANES, (pp + 1) * LANES)
        q2 = (q_ref[0, :, lanes].astype(F32) * (SB_HEAD_DIM ** -0.5 * LOG2E)).astype(BF16)
        zero = jnp.zeros_like(q2)
        qstack = jnp.concatenate([jnp.where(hm, q2, zero) for hm in halves], axis=0)

        def qk(jb):
            start = pl.multiple_of(jb * tq, tq)
            return _dot_nt(qstack, k_ref[0, pl.ds(start, tq), lanes])

        def av(a, jb):
            start = pl.multiple_of(jb * tq, tq)
            vb = v_ref[0, pl.ds(start, tq), lanes]
            vstack = jnp.concatenate([jnp.where(hm, vb, jnp.zeros_like(vb)) for hm in halves], axis=0)
            a2 = jnp.concatenate([a[:tq], a[tq:]], axis=1).astype(BF16)
            return _dot(a2, vstack)

        return qk, av

    fns = [pair_fns(pp) for pp in range(npair)]
    zs = {0: (fns[0][0](qi), fns[0][0](jp))}
    for pp in range(npair):
        qk, av = fns[pp]
        if pp + 1 < npair:
            zs[pp + 1] = (fns[pp + 1][0](qi), fns[pp + 1][0](jp))
        z_d, z_p = zs.pop(pp)
        incl_d = suffix(log_keep(z_d, True))
        incl_p = suffix(log_keep(z_p, False))
        r0 = jnp.zeros((2 * tq, 1), F32)
        acc = av(weights(z_d, incl_d, r0, True), qi)
        r1 = incl_d[:, 0:1]
        acc = acc + av(jnp.where(has_prev, weights(z_p, incl_p, r1, False), 0.0), jp)
        acc_scr[pp] = acc
        r_scr[pp] = r1 + jnp.where(has_prev, incl_p[:, 0:1], 0.0)

    def live(r):
        return jnp.max(r) > SB_EXIT

    for pp in range(npair):
        qk, av = pair_fns(pp)

        def step(state, qk=qk, av=av):
            i, acc, r, _ = state
            jb = qi - 2 - i
            z = qk(jb)
            incl = suffix(log_keep(z, False))
            acc = acc + av(weights(z, incl, r, False), jb)
            r = r + incl[:, 0:1]
            return i + 1, acc, r, live(r)

        r2 = r_scr[pp]
        _, acc, _, _ = lax.while_loop(lambda st: (st[0] < qi - 1) & st[3], step,
                                      (jnp.int32(0), acc_scr[pp], r2, live(r2)))
        o_ref[0, :, pp * LANES:(pp + 1) * LANES] = acc.astype(o_ref.dtype)


def _sb_attention(qkv, tq=MXU_DIM, pairs_per_step=4):
    b, s, n3 = qkv.shape
    n = n3 // 3
    gw = pairs_per_step * LANES
    ngrp = n // gw
    tri = (jnp.arange(tq)[:, None] >= jnp.arange(tq)[None, :]).astype(BF16)
    tri = jnp.concatenate([tri, tri], axis=0)
    kern = functools.partial(_sb_kernel, tq=tq)
    return pl.pallas_call(
        kern,
        grid=(b, ngrp, s // tq),
        in_specs=[
            pl.BlockSpec((1, tq, gw), lambda i, p, j: (i, j, p)),
            pl.BlockSpec((1, s, gw), lambda i, p, j: (i, 0, ngrp + p)),
            pl.BlockSpec((1, s, gw), lambda i, p, j: (i, 0, 2 * ngrp + p)),
            pl.BlockSpec((2 * tq, tq), lambda i, p, j: (0, 0)),
        ],
        out_specs=pl.BlockSpec((1, tq, gw), lambda i, p, j: (i, j, p)),
        out_shape=jax.ShapeDtypeStruct((b, s, n), BF16),
        scratch_shapes=[
            pltpu.VMEM((pairs_per_step, tq, LANES), F32),
            pltpu.VMEM((pairs_per_step, 2 * tq, 1), F32),
        ],
        compiler_params=_cparams("arbitrary", "arbitrary", "arbitrary"),
        name="sb_attn",
    )(qkv, qkv, qkv, tri)


def _t5_bucket_thresholds():
    max_exact = NUM_BUCKETS // 2
    n = np.arange(0, 4 * MAX_DISTANCE, dtype=np.int32)
    nf = np.maximum(n, 1).astype(np.float32)
    large = max_exact + (np.log(nf / np.float32(max_exact)) / np.float32(math.log(MAX_DISTANCE / max_exact))
                         * np.float32(NUM_BUCKETS - max_exact)).astype(np.int32)
    large = np.minimum(large, NUM_BUCKETS - 1)
    bucket = np.where(n < max_exact, n, large)
    assert np.all(np.diff(bucket) >= 0) and bucket[-1] == NUM_BUCKETS - 1
    return [int(np.argmax(bucket >= bb)) for bb in range(max_exact + 1, NUM_BUCKETS)]


def _sortable_key(v):
    v = jnp.where(v == 0.0, 0.0, v)
    bits = lax.bitcast_convert_type(v, I32)
    return bits ^ ((bits >> 31) & 0x7FFFFFFF)


def _tree_sum(parts):
    parts = list(parts)
    while len(parts) > 1:
        parts = [parts[i] + parts[i + 1] if i + 1 < len(parts) else parts[i] for i in range(0, len(parts), 2)]
    return parts[0]


def _dsa_kernel(rb_ref, q_ref, qidx_ref, lat_ref, kidx_ref, qg_ref, kg_ref, wuv_ref, tri_ref,
                o_ref, kn_scr, vt_scr, key_scr, add_scr, qn_scr, qis_scr, olat_scr, bias_scr,
                lg_a, lg_b, p_a, p_b, acc_scr, *, tq, topk, thresholds):
    bi = pl.program_id(0)
    qi = pl.program_id(1)
    nkb = vt_scr.shape[0]
    krow = lax.broadcasted_iota(I32, (tq, tq), 0)
    qcol = lax.broadcasted_iota(I32, (tq, tq), 1)
    qcol1 = lax.broadcasted_iota(I32, (1, tq), 1)

    @pl.when((bi == 0) & (qi == 0))
    def _():
        add_scr[nkb + 2 * DSA_HEADS] = jnp.full((tq, tq), NEG, F32)
        max_exact = NUM_BUCKETS // 2
        for near in range(2):
            dist = qcol - krow + near * tq
            large = jnp.full((tq, tq), max_exact, I32)
            for th in thresholds:
                large = large + jnp.where(dist >= th, 1, 0)
            bucket = jnp.where(dist < max_exact, jnp.maximum(dist, 0), large)

            def head_tile(h, _, near=near, bucket=bucket):
                far = rb_ref[NUM_BUCKETS - 1, h]
                tile = jnp.zeros((tq, tq), F32)
                for bb in range(NUM_BUCKETS - 1):
                    tile = jnp.where(bucket == bb, (rb_ref[bb, h] - far) * LOG2E, tile)
                bias_scr[near, h] = tile
                return 0

            lax.fori_loop(0, DSA_HEADS, head_tile, 0)

    @pl.when(qi == 0)
    def _():
        ext_row = lax.broadcasted_iota(I32, (V_PAD, tq), 0)
        for jb in range(nkb):
            lat = lat_ref[0, jb * tq:(jb + 1) * tq, :].astype(F32)
            ms = jnp.mean(lat * lat, axis=-1, keepdims=True)
            kn_scr[jb * tq:(jb + 1) * tq, :] = (lat * lax.rsqrt(ms + RMS_EPS) * kg_ref[...]).astype(BF16)
            vt_scr[jb, :DSA_LATENT, :] = lat.T.astype(BF16)
            vt_scr[jb, DSA_LATENT:, :] = jnp.where(ext_row == 0, 1.0, 0.0).astype(BF16)

    scale = DSA_LATENT ** -0.5 * LOG2E
    for h in range(DSA_HEADS):
        qh = q_ref[0, :, h * DSA_LATENT:(h + 1) * DSA_LATENT].astype(F32)
        ms = jnp.mean(qh * qh, axis=-1, keepdims=True)
        qn_scr[h * tq:(h + 1) * tq, :] = (qh * lax.rsqrt(ms + RMS_EPS) * (qg_ref[...] * scale)).astype(BF16)

    lane = lax.broadcasted_iota(I32, (1, LANES), 1)
    halves = (lane < IDX_DIM, lane >= IDX_DIM)
    n_qi = IDX_HEADS * IDX_DIM
    wi = qidx_ref[0, :, n_qi + LANES:n_qi + 2 * LANES].astype(F32) * (IDX_HEADS ** -0.5)
    wit = wi.T
    for hh in range(IDX_HEADS):
        blk = qidx_ref[0, :, (hh // 2) * LANES:(hh // 2 + 1) * LANES]
        qis_scr[hh * tq:(hh + 1) * tq, :] = jnp.where(halves[hh % 2], blk, jnp.zeros_like(blk))

    def causal(jb):
        return krow + jb * tq <= qcol + qi * tq

    def score_matmul(t, sc_ref):
        start = pl.multiple_of(jnp.minimum(t, qi) * tq, tq)
        sc_ref[:, :IDX_HEADS * tq] = _dot_nt(kidx_ref[0, pl.ds(start, tq), :], qis_scr[...])

    def score_keys(t, sc_ref):
        jb = jnp.minimum(t, qi)
        isc = jnp.zeros((tq, tq), F32)
        for hh in range(IDX_HEADS):
            isc = isc + jnp.maximum(sc_ref[:, hh * tq:(hh + 1) * tq], 0.0) * wit[hh:hh + 1, :]
        key_scr[jb] = jnp.where(causal(jb), _sortable_key(isc), INT_MIN)

    score_matmul(0, lg_a)

    def score_pair(u, _):
        t0 = 2 * u
        score_matmul(t0 + 1, lg_b)
        score_keys(t0, lg_a)
        score_matmul(t0 + 2, lg_a)
        score_keys(t0 + 1, lg_b)
        return 0

    lax.fori_loop(0, (qi + 2) // 2, score_pair, 0)

    def count(pred):
        def body(jb, part):
            c = jnp.where(pred(key_scr[jb]), 1.0, 0.0)
            return part + _tree_sum(c[g * SUBLANES:(g + 1) * SUBLANES, :] for g in range(tq // SUBLANES))
        part = lax.fori_loop(0, qi + 1, body, jnp.zeros((SUBLANES, tq), F32))
        return jnp.sum(part, axis=0, keepdims=True)

    kf = float(topk)
    c0 = count(lambda k: k >= 0)
    thr = jnp.where(c0 >= kf, 0, INT_MIN).astype(I32)
    cnt = jnp.where(c0 >= kf, c0, ((qi + 1) * tq).astype(F32))

    def bisect(i, state):
        thr, cnt = state
        cand = thr | lax.shift_left(jnp.int32(1), jnp.int32(30) - i)
        c = count(lambda k: k >= cand)
        return jnp.where(c >= kf, cand, thr), jnp.where(c >= kf, c, cnt)

    thr, cnt = lax.fori_loop(0, 31, bisect, (thr, cnt))
    short = qcol1 + qi * tq + 1 <= topk
    thr = jnp.where(short, INT_MIN, thr)
    straddle = jnp.max(jnp.where(short, 0.0, cnt - kf)) > 0.0

    @pl.when(jnp.logical_not(straddle))
    def _():
        def far_mask(jb, _):
            add_scr[jb] = jnp.where(key_scr[jb] >= thr, 0.0, NEG)
            return 0
        lax.fori_loop(0, qi, far_mask, 0)
        add_scr[qi] = jnp.where((key_scr[qi] >= thr) & causal(qi), 0.0, NEG)

    @pl.when(straddle)
    def _():
        need = kf - count(lambda k: k > thr)
        tri = tri_ref[...]

        def mask_block(jb, eq_before):
            key = key_scr[jb]
            eq = key == thr
            rank = _dot(tri, jnp.where(eq, 1.0, 0.0).astype(BF16)) + eq_before
            sel = ((key > thr) | (eq & (rank <= need))) & causal(jb)
            add_scr[jb] = jnp.where(sel, 0.0, NEG)
            return rank[tq - 1:tq, :]

        lax.fori_loop(0, qi + 1, mask_block, jnp.zeros((1, tq), F32))

    prev = jnp.maximum(qi - 1, 0)
    prev_pen = jnp.where(qi >= 1, 0.0, NEG)
    nblk = prev + 2

    def near_tiles(h, _):
        add_scr[nkb + h] = add_scr[prev] + bias_scr[1, h] + prev_pen
        add_scr[nkb + DSA_HEADS + h] = add_scr[qi] + bias_scr[0, h]
        return 0

    lax.fori_loop(0, DSA_HEADS, near_tiles, 0)

    def kv_block(t):
        t = jnp.maximum(t, 0)
        return jnp.minimum(jnp.where(t < prev, t, jnp.where(t == prev, prev, qi)), qi)

    def add_index(t, h):
        return jnp.where(t < prev, t,
                         jnp.where(t == prev, nkb + h,
                                   jnp.where(t == prev + 1, nkb + DSA_HEADS + h, nkb + 2 * DSA_HEADS)))

    gw = DSA_GROUP * tq

    def head_group(g, _):
        row0 = pl.multiple_of(g * gw, gw)
        qg = qn_scr[pl.ds(row0, gw), :]

        def stage_qk(t, lk_ref):
            start = pl.multiple_of(kv_block(t) * tq, tq)
            lg = _dot_nt(kn_scr[pl.ds(start, tq), :], qg)
            tops = []
            for k in range(DSA_GROUP):
                lk = lg[:, k * tq:(k + 1) * tq] + add_scr[add_index(t, g * DSA_GROUP + k)]
                lk_ref[:, k * tq:(k + 1) * tq] = lk
                tops.append(jnp.max(lk, axis=0, keepdims=True))
            return tuple(tops)

        def stage_softmax(lk_ref, p_ref, stats, tops):
            new_stats, alphas = [], []
            for k, (m, top) in enumerate(zip(stats, tops)):
                m_new = jnp.maximum(m, top)
                alphas.append(jnp.exp2(m - m_new))
                p_ref[:, k * tq:(k + 1) * tq] = jnp.exp2(lk_ref[:, k * tq:(k + 1) * tq] - m_new).astype(BF16)
                new_stats.append(m_new)
            return tuple(new_stats), tuple(alphas)

        def stage_pv(t, p_ref, alphas):
            pv = _dot(vt_scr[kv_block(t)], p_ref[...])
            for k, alpha in enumerate(alphas):
                acc_scr[k] = alpha * acc_scr[k] + pv[:, k * tq:(k + 1) * tq]

        stats = (jnp.full((1, tq), 0.5 * NEG, F32),) * DSA_GROUP
        ones = (jnp.ones((1, tq), F32),) * DSA_GROUP
        acc_scr[...] = jnp.zeros_like(acc_scr)
        p_b[...] = jnp.zeros_like(p_b)
        tops_a = stage_qk(0, lg_a)

        def pair(u, carry):
            stats, alphas, tops_a = carry
            t0 = 2 * u
            tops_b = stage_qk(t0 + 1, lg_b)
            stage_pv(t0 - 1, p_b, alphas)
            stats, alphas = stage_softmax(lg_a, p_a, stats, tops_a)
            tops_a = stage_qk(t0 + 2, lg_a)
            stage_pv(t0, p_a, alphas)
            stats, alphas = stage_softmax(lg_b, p_b, stats, tops_b)
            return stats, alphas, tops_a

        npair = (nblk + 1) // 2
        stats, alphas, _ = lax.fori_loop(0, npair, pair, (stats, ones, tops_a))
        stage_pv(2 * npair - 1, p_b, alphas)
        for k in range(DSA_GROUP):
            acc = acc_scr[k]
            l = acc[DSA_LATENT:DSA_LATENT + 1, :]
            olat_scr[g * DSA_GROUP + k] = (acc[:DSA_LATENT, :] * (1.0 / l)).T.astype(BF16)
        return 0

    lax.fori_loop(0, DSA_HEADS // DSA_GROUP, head_group, 0)

    for pr in range(DSA_HEADS // 2):
        out = _dot(olat_scr[2 * pr], wuv_ref[2 * pr]) + _dot(olat_scr[2 * pr + 1], wuv_ref[2 * pr + 1])
        o_ref[0, :, pr * LANES:(pr + 1) * LANES] = out.astype(o_ref.dtype)


def _dsa_attention(qp, idx, lat, q_gain, k_gain, wuv_ext, rel_bias, tq=MXU_DIM):
    b, s, nq = qp.shape
    assert DSA_GROUP >= IDX_HEADS
    topk = min(TOPK_MAX, s // 4)
    nkb = s // tq
    n_idx = idx.shape[-1]
    tri = (jnp.arange(tq)[:, None] >= jnp.arange(tq)[None, :]).astype(BF16)
    kern = functools.partial(_dsa_kernel, tq=tq, topk=topk, thresholds=tuple(_t5_bucket_thresholds()))
    return pl.pallas_call(
        kern,
        grid=(b, s // tq),
        in_specs=[
            pl.BlockSpec(memory_space=pltpu.SMEM),
            pl.BlockSpec((1, tq, nq), lambda i, j: (i, j, 0)),
            pl.BlockSpec((1, tq, n_idx), lambda i, j: (i, j, 0)),
            pl.BlockSpec((1, s, LANES), lambda i, j: (i, 0, 0)),
            pl.BlockSpec((1, s, LANES), lambda i, j: (i, 0, IDX_HEADS * IDX_DIM // LANES)),
            pl.BlockSpec((1, DSA_LATENT), lambda i, j: (0, 0)),
            pl.BlockSpec((1, DSA_LATENT), lambda i, j: (0, 0)),
            pl.BlockSpec((DSA_HEADS, DSA_LATENT, LANES), lambda i, j: (0, 0, 0)),
            pl.BlockSpec((tq, tq), lambda i, j: (0, 0)),
        ],
        out_specs=pl.BlockSpec((1, tq, DSA_HEADS * DSA_V_DIM), lambda i, j: (i, j, 0)),
        out_shape=jax.ShapeDtypeStruct((b, s, DSA_HEADS * DSA_V_DIM), BF16),
        scratch_shapes=[
            pltpu.VMEM((s, DSA_LATENT), BF16),
            pltpu.VMEM((nkb, DSA_LATENT + V_PAD, tq), BF16),
            pltpu.VMEM((nkb, tq, tq), I32),
            pltpu.VMEM((nkb + 2 * DSA_HEADS + 1, tq, tq), F32),
            pltpu.VMEM((DSA_HEADS * tq, DSA_LATENT), BF16),
            pltpu.VMEM((IDX_HEADS * tq, LANES), BF16),
            pltpu.VMEM((DSA_HEADS, tq, DSA_LATENT), BF16),
            pltpu.VMEM((2, DSA_HEADS, tq, tq), F32),
            pltpu.VMEM((tq, DSA_GROUP * tq), F32),
            pltpu.VMEM((tq, DSA_GROUP * tq), F32),
            pltpu.VMEM((tq, DSA_GROUP * tq), BF16),
            pltpu.VMEM((tq, DSA_GROUP * tq), BF16),
            pltpu.VMEM((DSA_GROUP, DSA_LATENT + V_PAD, tq), F32),
        ],
        compiler_params=_cparams("arbitrary", "arbitrary"),
        name="dsa_attn",
    )(rel_bias, qp, idx, lat, idx, q_gain.reshape(1, -1), k_gain.reshape(1, -1), wuv_ext, tri)


def _dsa_weights(w_in, w_uv):
    o1 = DSA_HEADS * DSA_LATENT
    o2 = o1 + DSA_LATENT
    o3 = o2 + IDX_HEADS * IDX_DIM
    o4 = o3 + IDX_DIM
    d = w_in.shape[0]
    w_ki = w_in[:, o3:o4]
    w_wi = w_in[:, o4:]
    pad = jnp.zeros((d, LANES - IDX_HEADS), w_in.dtype)
    w = jnp.concatenate([w_in[:, :o3], w_ki, w_ki, w_wi, pad], axis=1).astype(BF16)
    zeros = jnp.zeros_like(w_uv)
    even = jnp.concatenate([w_uv, zeros], axis=-1)
    odd = jnp.concatenate([zeros, w_uv], axis=-1)
    is_even = (jnp.arange(DSA_HEADS) % 2 == 0)[:, None, None]
    wuv_ext = jnp.where(is_even, even, odd).astype(BF16)
    splits = (o1, DSA_LATENT, IDX_HEADS * IDX_DIM + 2 * LANES)
    return w, wuv_ext, splits


def kernel(x, c, ada_w, ada_b, norm_mix, norm_ffn, sb_w_in, sb_w_out, dsa_w_in, dsa_q_norm,
           dsa_k_norm, dsa_w_uv, dsa_w_out, rel_bias, ffn_w_up, ffn_conv_w, ffn_conv_b, ffn_w_down):
    depth = ada_w.shape[0]
    d = x.shape[-1]
    mod = _modulation(c, ada_w, ada_b)
    for i in range(depth):
        sh1, sc1, g1, sh2, sc2, g2 = [mod[i, :, k * d:(k + 1) * d] for k in range(6)]
        j = i // 2
        if i % 2 == 0:
            w_in = sb_w_in[j].astype(BF16)
            (qkv,) = _norm_proj(x, norm_mix[i], sc1, sh1, w_in, (w_in.shape[1],))
            o = _sb_attention(qkv)
            w_out = sb_w_out[j]
        else:
            w_in, wuv_ext, splits = _dsa_weights(dsa_w_in[j], dsa_w_uv[j])
            qp, lat, idx = _norm_proj(x, norm_mix[i], sc1, sh1, w_in, splits)
            o = _dsa_attention(qp, idx, lat, dsa_q_norm[j], dsa_k_norm[j], wuv_ext, rel_bias)
            w_out = dsa_w_out[j]
        x = _mix_ffn(x, o, w_out.astype(BF16), g1, norm_ffn[i], sc2, sh2, g2, ffn_w_up[i].astype(BF16),
                     ffn_conv_w[i], ffn_conv_b[i], ffn_w_down[i].astype(BF16))
    return x
```

```python
import functools
import math

import numpy as np
import jax
import jax.numpy as jnp
from jax import lax
from jax.experimental import pallas as pl
from jax.experimental.pallas import tpu as pltpu

F32 = jnp.float32
BF16 = jnp.bfloat16
I32 = jnp.int32

LANES = 128
SUBLANES = 8
MXU_DIM = 256
VMEM_LIMIT = 60 * 1024 * 1024

RMS_EPS = 1e-6
NEG = -1e30
LOG2E = math.log2(math.e)
SB_EXIT = -151.0
INT_MIN = -(2 ** 31)

SB_HEADS = 16
SB_HEAD_DIM = 64
DSA_HEADS = 16
DSA_LATENT = 128
DSA_V_DIM = 64
V_PAD = 16
DSA_GROUP = 16
IDX_HEADS = 8
IDX_DIM = 64
TOPK_MAX = 256
NUM_BUCKETS = 32
MAX_DISTANCE = 128
CONV_W = 3


def _cparams(*sem):
    return pltpu.CompilerParams(dimension_semantics=sem, vmem_limit_bytes=VMEM_LIMIT)


def _dot(a, b):
    return jnp.dot(a, b, preferred_element_type=F32)


def _dot_nt(a, b):
    return lax.dot_general(a, b, (((1,), (1,)), ((), ())), preferred_element_type=F32)


def _split_bf16(v):
    hi = v.astype(BF16)
    lo = (v - hi.astype(F32)).astype(BF16)
    return hi, lo


def _mod_kernel(c_ref, w_ref, b_ref, o_ref):
    c = c_ref[...]
    cond = c * (1.0 / (1.0 + jnp.exp(-c)))
    ch, cl = _split_bf16(cond)
    wh, wl = _split_bf16(w_ref[0])
    o_ref[0] = _dot(ch, wh) + _dot(ch, wl) + _dot(cl, wh) + b_ref[0]


def _modulation(c, ada_w, ada_b):
    depth, d, n = ada_w.shape
    b = c.shape[0]
    tn = 1024
    return pl.pallas_call(
        _mod_kernel,
        grid=(depth, n // tn),
        in_specs=[
            pl.BlockSpec((b, d), lambda i, j: (0, 0)),
            pl.BlockSpec((1, d, tn), lambda i, j: (i, 0, j)),
            pl.BlockSpec((1, 1, tn), lambda i, j: (i, 0, j)),
        ],
        out_specs=pl.BlockSpec((1, b, tn), lambda i, j: (i, 0, j)),
        out_shape=jax.ShapeDtypeStruct((depth, b, n), F32),
        compiler_params=_cparams("arbitrary", "arbitrary"),
        name="mod",
    )(c, ada_w, ada_b.reshape(depth, 1, n))


def _modulated_norm(x, g, sc, sh):
    ms = jnp.mean(x * x, axis=-1, keepdims=True)
    return (x * lax.rsqrt(ms + RMS_EPS) * g) * (1.0 + sc) + sh


def _norm_proj_kernel(x_ref, g_ref, sc_ref, sh_ref, w_ref, *refs, splits, chunk, head_norm):
    gain_ref, o_refs = (refs[0], refs[1:]) if head_norm else (None, refs)
    hb = _modulated_norm(x_ref[0], g_ref[...], sc_ref[0], sh_ref[0]).astype(BF16)
    off = 0
    for idx, (o_ref, n) in enumerate(zip(o_refs, splits)):
        for c0 in range(0, n, chunk):
            c1 = min(n, c0 + chunk)
            y = _dot(hb, w_ref[:, off + c0:off + c1])
            if head_norm and idx == 0:
                for j in range((c1 - c0) // LANES):
                    yh = y[:, j * LANES:(j + 1) * LANES]
                    ms = jnp.mean(yh * yh, axis=-1, keepdims=True)
                    o_ref[0, c0 // LANES + j] = (yh * lax.rsqrt(ms + RMS_EPS) * gain_ref[...]).astype(o_ref.dtype)
            else:
                o_ref[0, :, c0:c1] = y.astype(o_ref.dtype)
        off += n


def _norm_proj(x, g, sc, sh, w, splits, head_gain=None, tm=2 * MXU_DIM):
    b, s, d = x.shape
    n = w.shape[1]
    assert sum(splits) == n
    head_norm = head_gain is not None
    kern = functools.partial(_norm_proj_kernel, splits=tuple(splits), chunk=512, head_norm=head_norm)
    in_specs = [
        pl.BlockSpec((1, tm, d), lambda i, j: (i, j, 0)),
        pl.BlockSpec((1, d), lambda i, j: (0, 0)),
        pl.BlockSpec((1, 1, d), lambda i, j: (i, 0, 0)),
        pl.BlockSpec((1, 1, d), lambda i, j: (i, 0, 0)),
        pl.BlockSpec((d, n), lambda i, j: (0, 0)),
    ]
    args = [x, g.reshape(1, d), sc.reshape(b, 1, d), sh.reshape(b, 1, d), w]
    out_specs = [pl.BlockSpec((1, tm, m), lambda i, j: (i, j, 0)) for m in splits]
    out_shape = [jax.ShapeDtypeStruct((b, s, m), BF16) for m in splits]
    if head_norm:
        nh = splits[0] // LANES
        in_specs.append(pl.BlockSpec((1, LANES), lambda i, j: (0, 0)))
        args.append(head_gain.reshape(1, LANES))
        out_specs[0] = pl.BlockSpec((1, nh, tm, LANES), lambda i, j: (i, 0, j, 0))
        out_shape[0] = jax.ShapeDtypeStruct((b, nh, s, LANES), BF16)
    return pl.pallas_call(
        kern,
        grid=(b, s // tm),
        in_specs=in_specs,
        out_specs=out_specs,
        out_shape=out_shape,
        compiler_params=_cparams("arbitrary", "arbitrary"),
        name="norm_proj",
    )(*args)


def _ffn_kernel(x_ref, o_ref, wo_ref, mgate_ref, g_ref, sc_ref, sh_ref, gate_ref, wu_ref, cw_ref, cb_ref,
                wd_ref, y_ref, act_scr, carry_scr, *, fc):
    si = pl.program_id(1)
    tm = act_scr.shape[0]
    f = act_scr.shape[1]
    nf = f // fc

    @pl.when(si == 0)
    def _():
        carry_scr[...] = jnp.zeros_like(carry_scr)

    x1 = x_ref[0] + mgate_ref[0] * _dot(o_ref[0], wo_ref[...])
    hb = _modulated_norm(x1, g_ref[...], sc_ref[0], sh_ref[0]).astype(BF16)
    row = lax.broadcasted_iota(I32, (tm, 1), 0)

    def up(c):
        return (_dot(hb, wu_ref[:, c * fc:(c + 1) * fc]), _dot(hb, wu_ref[:, f + c * fc:f + (c + 1) * fc]))

    def conv(u, col0):
        prev = carry_scr[:, col0:col0 + fc]
        carry_scr[:, col0:col0 + fc] = u[tm - SUBLANES:, :]
        p1 = prev[SUBLANES - 1:SUBLANES, :]
        p2 = prev[SUBLANES - 2:SUBLANES - 1, :]
        u1 = jnp.where(row == 0, p1, pltpu.roll(u, 1, 0))
        u2 = jnp.where(row == 0, p2, jnp.where(row == 1, p1, pltpu.roll(u, 2, 0)))
        cw = cw_ref[:, col0:col0 + fc]
        return cb_ref[:, col0:col0 + fc] + u2 * cw[0:1, :] + u1 * cw[1:2, :] + u * cw[2:3, :]

    nxt = up(0)
    for c in range(nf):
        ug, uv = nxt
        if c + 1 < nf:
            nxt = up(c + 1)
        yg = conv(ug, c * fc)
        yv = conv(uv, f + c * fc)
        act_scr[:, c * fc:(c + 1) * fc] = (yg * (1.0 / (1.0 + jnp.exp(-yg))) * yv).astype(BF16)

    y_ref[0] = x1 + gate_ref[0] * _dot(act_scr[...], wd_ref[...])


def _mix_ffn(x, o, w_out, mgate, g, sc, sh, gate, w_up, conv_w, conv_b, w_down, tm=2 * MXU_DIM, fc=MXU_DIM):
    b, s, d = x.shape
    f = w_down.shape[0]
    k = o.shape[-1]
    kern = functools.partial(_ffn_kernel, fc=fc)
    vec = lambda a: a.reshape(b, 1, d)
    const = lambda i, j: (0, 0)
    return pl.pallas_call(
        kern,
        grid=(b, s // tm),
        in_specs=[
            pl.BlockSpec((1, tm, d), lambda i, j: (i, j, 0)),
            pl.BlockSpec((1, tm, k), lambda i, j: (i, j, 0)),
            pl.BlockSpec((k, d), const),
            pl.BlockSpec((1, 1, d), lambda i, j: (i, 0, 0)),
            pl.BlockSpec((1, d), const),
            pl.BlockSpec((1, 1, d), lambda i, j: (i, 0, 0)),
            pl.BlockSpec((1, 1, d), lambda i, j: (i, 0, 0)),
            pl.BlockSpec((1, 1, d), lambda i, j: (i, 0, 0)),
            pl.BlockSpec((d, 2 * f), const),
            pl.BlockSpec((CONV_W, 2 * f), const),
            pl.BlockSpec((1, 2 * f), const),
            pl.BlockSpec((f, d), const),
        ],
        out_specs=pl.BlockSpec((1, tm, d), lambda i, j: (i, j, 0)),
        out_shape=jax.ShapeDtypeStruct((b, s, d), F32),
        scratch_shapes=[
            pltpu.VMEM((tm, f), BF16),
            pltpu.VMEM((SUBLANES, 2 * f), F32),
        ],
        compiler_params=_cparams("arbitrary", "arbitrary"),
        name="ffn",
    )(x, o, w_out, vec(mgate), g.reshape(1, d), vec(sc), vec(sh), vec(gate), w_up, conv_w,
      conv_b.reshape(1, 2 * f), w_down)


def _sb_kernel(q_ref, k_ref, v_ref, tri_ref, o_ref, acc_scr, r_scr, *, tq):
    qi = pl.program_id(2)
    npair = acc_scr.shape[0]
    lane = lax.broadcasted_iota(I32, (1, LANES), 1)
    halves = (lane < SB_HEAD_DIM, lane >= SB_HEAD_DIM)
    tri2 = tri_ref[...]
    row = lax.broadcasted_iota(I32, (2 * tq, tq), 0)
    col = lax.broadcasted_iota(I32, (2 * tq, tq), 1)
    strict = col < jnp.where(row >= tq, row - tq, row)
    has_prev = qi >= 1
    jp = jnp.maximum(qi - 1, 0)

    def log_keep(z, masked):
        lk = -(jnp.maximum(z, 0.0) + jnp.log2(1.0 + jnp.exp2(-jnp.abs(z))))
        return jnp.where(strict, lk, 0.0) if masked else lk

    def suffix(lk):
        hi, lo = _split_bf16(lk)
        return _dot(jnp.concatenate([hi, lo], axis=1), tri2)

    def weights(z, incl, r, masked):
        a = jnp.exp2(z + incl + r)
        return jnp.where(strict, a, 0.0) if masked else a

    def pair_fns(pp):
        lanes = slice(pp * LANES, (pp + 1) * LANES)
        q2 = (q_ref[0, :, lanes].astype(F32) * (SB_HEAD_DIM ** -0.5 * LOG2E)).astype(BF16)
        zero = jnp.zeros_like(q2)
        qstack = jnp.concatenate([jnp.where(hm, q2, zero) for hm in halves], axis=0)

        def qk(jb):
            start = pl.multiple_of(jb * tq, tq)
            return _dot_nt(qstack, k_ref[0, pl.ds(start, tq), lanes])

        def av(a, jb):
            start = pl.multiple_of(jb * tq, tq)
            vb = v_ref[0, pl.ds(start, tq), lanes]
            vstack = jnp.concatenate([jnp.where(hm, vb, jnp.zeros_like(vb)) for hm in halves], axis=0)
            a2 = jnp.concatenate([a[:tq], a[tq:]], axis=1).astype(BF16)
            return _dot(a2, vstack)

        return qk, av

    fns = [pair_fns(pp) for pp in range(npair)]
    zs = {0: (fns[0][0](qi), fns[0][0](jp))}
    for pp in range(npair):
        qk, av = fns[pp]
        if pp + 1 < npair:
            zs[pp + 1] = (fns[pp + 1][0](qi), fns[pp + 1][0](jp))
        z_d, z_p = zs.pop(pp)
        incl_d = suffix(log_keep(z_d, True))
        incl_p = suffix(log_keep(z_p, False))
        r0 = jnp.zeros((2 * tq, 1), F32)
        acc = av(weights(z_d, incl_d, r0, True), qi)
        r1 = incl_d[:, 0:1]
        acc = acc + av(jnp.where(has_prev, weights(z_p, incl_p, r1, False), 0.0), jp)
        acc_scr[pp] = acc
        r_scr[pp] = r1 + jnp.where(has_prev, incl_p[:, 0:1], 0.0)

    def live(r):
        return jnp.max(r) > SB_EXIT

    for pp in range(npair):
        qk, av = pair_fns(pp)

        def step(state, qk=qk, av=av):
            i, acc, r, _ = state
            jb = qi - 2 - i
            z = qk(jb)
            incl = suffix(log_keep(z, False))
            acc = acc + av(weights(z, incl, r, False), jb)
            r = r + incl[:, 0:1]
            return i + 1, acc, r, live(r)

        r2 = r_scr[pp]
        _, acc, _, _ = lax.while_loop(lambda st: (st[0] < qi - 1) & st[3], step,
                                      (jnp.int32(0), acc_scr[pp], r2, live(r2)))
        o_ref[0, :, pp * LANES:(pp + 1) * LANES] = acc.astype(o_ref.dtype)


def _sb_attention(qkv, tq=MXU_DIM, pairs_per_step=8):
    b, s, n3 = qkv.shape
    n = n3 // 3
    gw = pairs_per_step * LANES
    ngrp = n // gw
    tri = (jnp.arange(tq)[:, None] >= jnp.arange(tq)[None, :]).astype(BF16)
    tri = jnp.concatenate([tri, tri], axis=0)
    kern = functools.partial(_sb_kernel, tq=tq)
    return pl.pallas_call(
        kern,
        grid=(b, ngrp, s // tq),
        in_specs=[
            pl.BlockSpec((1, tq, gw), lambda i, p, j: (i, j, p)),
            pl.BlockSpec((1, s, gw), lambda i, p, j: (i, 0, ngrp + p)),
            pl.BlockSpec((1, s, gw), lambda i, p, j: (i, 0, 2 * ngrp + p)),
            pl.BlockSpec((2 * tq, tq), lambda i, p, j: (0, 0)),
        ],
        out_specs=pl.BlockSpec((1, tq, gw), lambda i, p, j: (i, j, p)),
        out_shape=jax.ShapeDtypeStruct((b, s, n), BF16),
        scratch_shapes=[
            pltpu.VMEM((pairs_per_step, tq, LANES), F32),
            pltpu.VMEM((pairs_per_step, 2 * tq, 1), F32),
        ],
        compiler_params=_cparams("arbitrary", "arbitrary", "arbitrary"),
        name="sb_attn",
    )(qkv, qkv, qkv, tri)


def _t5_bucket_thresholds():
    max_exact = NUM_BUCKETS // 2
    n = np.arange(0, 4 * MAX_DISTANCE, dtype=np.int32)
    nf = np.maximum(n, 1).astype(np.float32)
    large = max_exact + (np.log(nf / np.float32(max_exact)) / np.float32(math.log(MAX_DISTANCE / max_exact))
                         * np.float32(NUM_BUCKETS - max_exact)).astype(np.int32)
    large = np.minimum(large, NUM_BUCKETS - 1)
    bucket = np.where(n < max_exact, n, large)
    assert np.all(np.diff(bucket) >= 0) and bucket[-1] == NUM_BUCKETS - 1
    return [int(np.argmax(bucket >= bb)) for bb in range(max_exact + 1, NUM_BUCKETS)]


def _sortable_key(v):
    v = jnp.where(v == 0.0, 0.0, v)
    bits = lax.bitcast_convert_type(v, I32)
    return bits ^ ((bits >> 31) & 0x7FFFFFFF)


def _tree_sum(parts):
    parts = list(parts)
    while len(parts) > 1:
        parts = [parts[i] + parts[i + 1] if i + 1 < len(parts) else parts[i] for i in range(0, len(parts), 2)]
    return parts[0]


def _dsa_kernel(rb_ref, q_ref, qidx_ref, lat_ref, kidx_ref, kg_ref, wuv_ref, tri_ref,
                o_ref, kn_scr, vt_scr, key_scr, add_scr, qis_scr, olat_scr, bias_scr,
                lg_a, lg_b, p_a, p_b, acc_scr, *, tq, topk, thresholds):
    bi = pl.program_id(0)
    qi = pl.program_id(1)
    nkb = vt_scr.shape[0]
    krow = lax.broadcasted_iota(I32, (tq, tq), 0)
    qcol = lax.broadcasted_iota(I32, (tq, tq), 1)
    qcol1 = lax.broadcasted_iota(I32, (1, tq), 1)

    @pl.when((bi == 0) & (qi == 0))
    def _():
        add_scr[nkb + 2 * DSA_HEADS] = jnp.full((tq, tq), NEG, F32)
        max_exact = NUM_BUCKETS // 2
        for near in range(2):
            dist = qcol - krow + near * tq
            large = jnp.full((tq, tq), max_exact, I32)
            for th in thresholds:
                large = large + jnp.where(dist >= th, 1, 0)
            bucket = jnp.where(dist < max_exact, jnp.maximum(dist, 0), large)

            def head_tile(h, _, near=near, bucket=bucket):
                far = rb_ref[NUM_BUCKETS - 1, h]
                tile = jnp.zeros((tq, tq), F32)
                for bb in range(NUM_BUCKETS - 1):
                    tile = jnp.where(bucket == bb, (rb_ref[bb, h] - far) * LOG2E, tile)
                bias_scr[near, h] = tile
                return 0

            lax.fori_loop(0, DSA_HEADS, head_tile, 0)

    @pl.when(qi == 0)
    def _():
        ext_row = lax.broadcasted_iota(I32, (V_PAD, tq), 0)
        for jb in range(nkb):
            lat = lat_ref[0, jb * tq:(jb + 1) * tq, :].astype(F32)
            ms = jnp.mean(lat * lat, axis=-1, keepdims=True)
            kn_scr[jb * tq:(jb + 1) * tq, :] = (lat * lax.rsqrt(ms + RMS_EPS) * kg_ref[...]).astype(BF16)
            vt_scr[jb, :DSA_LATENT, :] = lat.T.astype(BF16)
            vt_scr[jb, DSA_LATENT:, :] = jnp.where(ext_row == 0, 1.0, 0.0).astype(BF16)

    lane = lax.broadcasted_iota(I32, (1, LANES), 1)
    halves = (lane < IDX_DIM, lane >= IDX_DIM)
    n_qi = IDX_HEADS * IDX_DIM
    wi = qidx_ref[0, :, n_qi + LANES:n_qi + 2 * LANES].astype(F32) * (IDX_HEADS ** -0.5)
    wit = wi.T
    for hh in range(IDX_HEADS):
        blk = qidx_ref[0, :, (hh // 2) * LANES:(hh // 2 + 1) * LANES]
        qis_scr[hh * tq:(hh + 1) * tq, :] = jnp.where(halves[hh % 2], blk, jnp.zeros_like(blk))

    def causal(jb):
        return krow + jb * tq <= qcol + qi * tq

    def score_matmul(t, sc_ref):
        start = pl.multiple_of(jnp.minimum(t, qi) * tq, tq)
        sc_ref[:, :IDX_HEADS * tq] = _dot_nt(kidx_ref[0, pl.ds(start, tq), :], qis_scr[...])

    def score_keys(t, sc_ref):
        jb = jnp.minimum(t, qi)
        isc = jnp.zeros((tq, tq), F32)
        for hh in range(IDX_HEADS):
            isc = isc + jnp.maximum(sc_ref[:, hh * tq:(hh + 1) * tq], 0.0) * wit[hh:hh + 1, :]
        key_scr[jb] = jnp.where(causal(jb), _sortable_key(isc), INT_MIN)

    score_matmul(0, lg_a)

    def score_pair(u, _):
        t0 = 2 * u
        score_matmul(t0 + 1, lg_b)
        score_keys(t0, lg_a)
        score_matmul(t0 + 2, lg_a)
        score_keys(t0 + 1, lg_b)
        return 0

    lax.fori_loop(0, (qi + 2) // 2, score_pair, 0)

    def count(pred):
        def body(jb, part):
            c = jnp.where(pred(key_scr[jb]), 1.0, 0.0)
            return part + _tree_sum(c[g * SUBLANES:(g + 1) * SUBLANES, :] for g in range(tq // SUBLANES))
        part = lax.fori_loop(0, qi + 1, body, jnp.zeros((SUBLANES, tq), F32))
        return jnp.sum(part, axis=0, keepdims=True)

    kf = float(topk)
    c0 = count(lambda k: k >= 0)
    thr = jnp.where(c0 >= kf, 0, INT_MIN).astype(I32)
    cnt = jnp.where(c0 >= kf, c0, ((qi + 1) * tq).astype(F32))

    def bisect(i, state):
        thr, cnt = state
        cand = thr | lax.shift_left(jnp.int32(1), jnp.int32(30) - i)
        c = count(lambda k: k >= cand)
        return jnp.where(c >= kf, cand, thr), jnp.where(c >= kf, c, cnt)

    thr, cnt = lax.fori_loop(0, 31, bisect, (thr, cnt))
    short = qcol1 + qi * tq + 1 <= topk
    thr = jnp.where(short, INT_MIN, thr)
    straddle = jnp.max(jnp.where(short, 0.0, cnt - kf)) > 0.0

    @pl.when(jnp.logical_not(straddle))
    def _():
        def far_mask(jb, _):
            add_scr[jb] = jnp.where(key_scr[jb] >= thr, 0.0, NEG)
            return 0
        lax.fori_loop(0, qi, far_mask, 0)
        add_scr[qi] = jnp.where((key_scr[qi] >= thr) & causal(qi), 0.0, NEG)

    @pl.when(straddle)
    def _():
        need = kf - count(lambda k: k > thr)
        tri = tri_ref[...]

        def mask_block(jb, eq_before):
            key = key_scr[jb]
            eq = key == thr
            rank = _dot(tri, jnp.where(eq, 1.0, 0.0).astype(BF16)) + eq_before
            sel = ((key > thr) | (eq & (rank <= need))) & causal(jb)
            add_scr[jb] = jnp.where(sel, 0.0, NEG)
            return rank[tq - 1:tq, :]

        lax.fori_loop(0, qi + 1, mask_block, jnp.zeros((1, tq), F32))

    prev = jnp.maximum(qi - 1, 0)
    prev_pen = jnp.where(qi >= 1, 0.0, NEG)
    nblk = prev + 2

    def near_tiles(h, _):
        add_scr[nkb + h] = add_scr[prev] + bias_scr[1, h] + prev_pen
        add_scr[nkb + DSA_HEADS + h] = add_scr[qi] + bias_scr[0, h]
        return 0

    lax.fori_loop(0, DSA_HEADS, near_tiles, 0)

    def kv_block(t):
        t = jnp.maximum(t, 0)
        return jnp.minimum(jnp.where(t < prev, t, jnp.where(t == prev, prev, qi)), qi)

    def add_index(t, h):
        return jnp.where(t < prev, t,
                         jnp.where(t == prev, nkb + h,
                                   jnp.where(t == prev + 1, nkb + DSA_HEADS + h, nkb + 2 * DSA_HEADS)))

    gw = DSA_GROUP * tq

    def head_group(g, _):
        qg = q_ref[0, pl.ds(g * DSA_GROUP, DSA_GROUP)].reshape(gw, DSA_LATENT)

        def stage_qk(t, lk_ref):
            start = pl.multiple_of(kv_block(t) * tq, tq)
            lg = _dot_nt(kn_scr[pl.ds(start, tq), :], qg)
            tops = []
            for k in range(DSA_GROUP):
                lk = lg[:, k * tq:(k + 1) * tq] + add_scr[add_index(t, g * DSA_GROUP + k)]
                lk_ref[:, k * tq:(k + 1) * tq] = lk
                tops.append(jnp.max(lk, axis=0, keepdims=True))
            return tuple(tops)

        def stage_softmax(lk_ref, p_ref, stats, tops):
            new_stats, alphas = [], []
            for k, (m, top) in enumerate(zip(stats, tops)):
                m_new = jnp.maximum(m, top)
                alphas.append(jnp.exp2(m - m_new))
                p_ref[:, k * tq:(k + 1) * tq] = jnp.exp2(lk_ref[:, k * tq:(k + 1) * tq] - m_new).astype(BF16)
                new_stats.append(m_new)
            return tuple(new_stats), tuple(alphas)

        def stage_pv(t, p_ref, alphas):
            pv = _dot(vt_scr[kv_block(t)], p_ref[...])
            for k, alpha in enumerate(alphas):
                acc_scr[k] = alpha * acc_scr[k] + pv[:, k * tq:(k + 1) * tq]

        stats = (jnp.full((1, tq), 0.5 * NEG, F32),) * DSA_GROUP
        ones = (jnp.ones((1, tq), F32),) * DSA_GROUP
        acc_scr[...] = jnp.zeros_like(acc_scr)
        p_b[...] = jnp.zeros_like(p_b)
        tops_a = stage_qk(0, lg_a)

        def pair(u, carry):
            stats, alphas, tops_a = carry
            t0 = 2 * u
            tops_b = stage_qk(t0 + 1, lg_b)
            stage_pv(t0 - 1, p_b, alphas)
            stats, alphas = stage_softmax(lg_a, p_a, stats, tops_a)
            tops_a = stage_qk(t0 + 2, lg_a)
            stage_pv(t0, p_a, alphas)
            stats, alphas = stage_softmax(lg_b, p_b, stats, tops_b)
            return stats, alphas, tops_a

        npair = (nblk + 1) // 2
        stats, alphas, _ = lax.fori_loop(0, npair, pair, (stats, ones, tops_a))
        stage_pv(2 * npair - 1, p_b, alphas)
        for k in range(DSA_GROUP):
            acc = acc_scr[k]
            l = acc[DSA_LATENT:DSA_LATENT + 1, :]
            olat_scr[g * DSA_GROUP + k] = (acc[:DSA_LATENT, :] * (1.0 / l)).T.astype(BF16)
        return 0

    lax.fori_loop(0, DSA_HEADS // DSA_GROUP, head_group, 0)

    for pr in range(DSA_HEADS // 2):
        out = _dot(olat_scr[2 * pr], wuv_ref[2 * pr]) + _dot(olat_scr[2 * pr + 1], wuv_ref[2 * pr + 1])
        o_ref[0, :, pr * LANES:(pr + 1) * LANES] = out.astype(o_ref.dtype)


def _dsa_attention(qn, idx, lat, k_gain, wuv_ext, rel_bias, tq=MXU_DIM):
    b, _, s, _ = qn.shape
    assert DSA_GROUP >= IDX_HEADS
    topk = min(TOPK_MAX, s // 4)
    nkb = s // tq
    n_idx = idx.shape[-1]
    tri = (jnp.arange(tq)[:, None] >= jnp.arange(tq)[None, :]).astype(BF16)
    kern = functools.partial(_dsa_kernel, tq=tq, topk=topk, thresholds=tuple(_t5_bucket_thresholds()))
    return pl.pallas_call(
        kern,
        grid=(b, s // tq),
        in_specs=[
            pl.BlockSpec(memory_space=pltpu.SMEM),
            pl.BlockSpec((1, DSA_HEADS, tq, DSA_LATENT), lambda i, j: (i, 0, j, 0)),
            pl.BlockSpec((1, tq, n_idx), lambda i, j: (i, j, 0)),
            pl.BlockSpec((1, s, LANES), lambda i, j: (i, 0, 0)),
            pl.BlockSpec((1, s, LANES), lambda i, j: (i, 0, IDX_HEADS * IDX_DIM // LANES)),
            pl.BlockSpec((1, DSA_LATENT), lambda i, j: (0, 0)),
            pl.BlockSpec((DSA_HEADS, DSA_LATENT, LANES), lambda i, j: (0, 0, 0)),
            pl.BlockSpec((tq, tq), lambda i, j: (0, 0)),
        ],
        out_specs=pl.BlockSpec((1, tq, DSA_HEADS * DSA_V_DIM), lambda i, j: (i, j, 0)),
        out_shape=jax.ShapeDtypeStruct((b, s, DSA_HEADS * DSA_V_DIM), BF16),
        scratch_shapes=[
            pltpu.VMEM((s, DSA_LATENT), BF16),
            pltpu.VMEM((nkb, DSA_LATENT + V_PAD, tq), BF16),
            pltpu.VMEM((nkb, tq, tq), I32),
            pltpu.VMEM((nkb + 2 * DSA_HEADS + 1, tq, tq), F32),
            pltpu.VMEM((IDX_HEADS * tq, LANES), BF16),
            pltpu.VMEM((DSA_HEADS, tq, DSA_LATENT), BF16),
            pltpu.VMEM((2, DSA_HEADS, tq, tq), F32),
            pltpu.VMEM((tq, DSA_GROUP * tq), F32),
            pltpu.VMEM((tq, DSA_GROUP * tq), F32),
            pltpu.VMEM((tq, DSA_GROUP * tq), BF16),
            pltpu.VMEM((tq, DSA_GROUP * tq), BF16),
            pltpu.VMEM((DSA_GROUP, DSA_LATENT + V_PAD, tq), F32),
        ],
        compiler_params=_cparams("arbitrary", "arbitrary"),
        name="dsa_attn",
    )(rel_bias, qn, idx, lat, idx, k_gain.reshape(1, -1), wuv_ext, tri)


def _dsa_weights(w_in, w_uv):
    o1 = DSA_HEADS * DSA_LATENT
    o2 = o1 + DSA_LATENT
    o3 = o2 + IDX_HEADS * IDX_DIM
    o4 = o3 + IDX_DIM
    d = w_in.shape[0]
    w_ki = w_in[:, o3:o4]
    w_wi = w_in[:, o4:]
    pad = jnp.zeros((d, LANES - IDX_HEADS), w_in.dtype)
    w = jnp.concatenate([w_in[:, :o3], w_ki, w_ki, w_wi, pad], axis=1).astype(BF16)
    zeros = jnp.zeros_like(w_uv)
    even = jnp.concatenate([w_uv, zeros], axis=-1)
    odd = jnp.concatenate([zeros, w_uv], axis=-1)
    is_even = (jnp.arange(DSA_HEADS) % 2 == 0)[:, None, None]
    wuv_ext = jnp.where(is_even, even, odd).astype(BF16)
    splits = (o1, DSA_LATENT, IDX_HEADS * IDX_DIM + 2 * LANES)
    return w, wuv_ext, splits


def kernel(x, c, ada_w, ada_b, norm_mix, norm_ffn, sb_w_in, sb_w_out, dsa_w_in, dsa_q_norm,
           dsa_k_norm, dsa_w_uv, dsa_w_out, rel_bias, ffn_w_up, ffn_conv_w, ffn_conv_b, ffn_w_down):
    depth = ada_w.shape[0]
    d = x.shape[-1]
    mod = _modulation(c, ada_w, ada_b)
    for i in range(depth):
        sh1, sc1, g1, sh2, sc2, g2 = [mod[i, :, k * d:(k + 1) * d] for k in range(6)]
        j = i // 2
        if i % 2 == 0:
            w_in = sb_w_in[j].astype(BF16)
            (qkv,) = _norm_proj(x, norm_mix[i], sc1, sh1, w_in, (w_in.shape[1],))
            o = _sb_attention(qkv)
            w_out = sb_w_out[j]
        else:
            w_in, wuv_ext, splits = _dsa_weights(dsa_w_in[j], dsa_w_uv[j])
            q_gain = dsa_q_norm[j] * (DSA_LATENT ** -0.5 * LOG2E)
            qn, lat, idx = _norm_proj(x, norm_mix[i], sc1, sh1, w_in, splits, head_gain=q_gain)
            o = _dsa_attention(qn, idx, lat, dsa_k_norm[j], wuv_ext, rel_bias)
            w_out = dsa_w_out[j]
        x = _mix_ffn(x, o, w_out.astype(BF16), g1, norm_ffn[i], sc2, sh2, g2, ffn_w_up[i].astype(BF16),
                     ffn_conv_w[i], ffn_conv_b[i], ffn_w_down[i].astype(BF16))
    return x
```

```python
import functools
import math

import numpy as np
import jax
import jax.numpy as jnp
from jax import lax
from jax.experimental import pallas as pl
from jax.experimental.pallas import tpu as pltpu

F32 = jnp.float32
BF16 = jnp.bfloat16
I32 = jnp.int32

LANES = 128
SUBLANES = 8
MXU_DIM = 256
VMEM_LIMIT = 60 * 1024 * 1024

RMS_EPS = 1e-6
NEG = -1e30
LOG2E = math.log2(math.e)
SB_EXIT = -151.0
INT_MIN = -(2 ** 31)
TOP_BITS = 8

SB_HEADS = 16
SB_HEAD_DIM = 64
DSA_HEADS = 16
DSA_LATENT = 128
DSA_V_DIM = 64
V_PAD = 16
DSA_GROUP = 16
IDX_HEADS = 8
IDX_DIM = 64
TOPK_MAX = 256
NUM_BUCKETS = 32
MAX_DISTANCE = 128
CONV_W = 3


def _cparams(*sem):
    return pltpu.CompilerParams(dimension_semantics=sem, vmem_limit_bytes=VMEM_LIMIT)


def _dot(a, b):
    return jnp.dot(a, b, preferred_element_type=F32)


def _dot_nt(a, b):
    return lax.dot_general(a, b, (((1,), (1,)), ((), ())), preferred_element_type=F32)


def _split_bf16(v):
    hi = v.astype(BF16)
    lo = (v - hi.astype(F32)).astype(BF16)
    return hi, lo


def _mod_kernel(c_ref, w_ref, b_ref, o_ref):
    c = c_ref[...]
    cond = c * (1.0 / (1.0 + jnp.exp(-c)))
    ch, cl = _split_bf16(cond)
    wh, wl = _split_bf16(w_ref[0])
    o_ref[0] = _dot(ch, wh) + _dot(ch, wl) + _dot(cl, wh) + b_ref[0]


def _modulation(c, ada_w, ada_b):
    depth, d, n = ada_w.shape
    b = c.shape[0]
    tn = 1024
    return pl.pallas_call(
        _mod_kernel,
        grid=(depth, n // tn),
        in_specs=[
            pl.BlockSpec((b, d), lambda i, j: (0, 0)),
            pl.BlockSpec((1, d, tn), lambda i, j: (i, 0, j)),
            pl.BlockSpec((1, 1, tn), lambda i, j: (i, 0, j)),
        ],
        out_specs=pl.BlockSpec((1, b, tn), lambda i, j: (i, 0, j)),
        out_shape=jax.ShapeDtypeStruct((depth, b, n), F32),
        compiler_params=_cparams("arbitrary", "arbitrary"),
        name="mod",
    )(c, ada_w, ada_b.reshape(depth, 1, n))


def _modulated_norm(x, g, sc, sh):
    ms = jnp.mean(x * x, axis=-1, keepdims=True)
    return (x * lax.rsqrt(ms + RMS_EPS) * g) * (1.0 + sc) + sh


def _norm_proj_kernel(x_ref, g_ref, sc_ref, sh_ref, w_ref, *refs, splits, chunk, head_norm):
    gain_ref, o_refs = (refs[0], refs[1:]) if head_norm else (None, refs)
    hb = _modulated_norm(x_ref[0], g_ref[...], sc_ref[0], sh_ref[0]).astype(BF16)
    off = 0
    for idx, (o_ref, n) in enumerate(zip(o_refs, splits)):
        for c0 in range(0, n, chunk):
            c1 = min(n, c0 + chunk)
            y = _dot(hb, w_ref[:, off + c0:off + c1])
            if head_norm and idx == 0:
                for j in range((c1 - c0) // LANES):
                    yh = y[:, j * LANES:(j + 1) * LANES]
                    ms = jnp.mean(yh * yh, axis=-1, keepdims=True)
                    o_ref[0, c0 // LANES + j] = (yh * lax.rsqrt(ms + RMS_EPS) * gain_ref[...]).astype(o_ref.dtype)
            else:
                o_ref[0, :, c0:c1] = y.astype(o_ref.dtype)
        off += n


def _norm_proj(x, g, sc, sh, w, splits, head_gain=None, tm=2 * MXU_DIM):
    b, s, d = x.shape
    n = w.shape[1]
    assert sum(splits) == n
    head_norm = head_gain is not None
    kern = functools.partial(_norm_proj_kernel, splits=tuple(splits), chunk=512, head_norm=head_norm)
    in_specs = [
        pl.BlockSpec((1, tm, d), lambda i, j: (i, j, 0)),
        pl.BlockSpec((1, d), lambda i, j: (0, 0)),
        pl.BlockSpec((1, 1, d), lambda i, j: (i, 0, 0)),
        pl.BlockSpec((1, 1, d), lambda i, j: (i, 0, 0)),
        pl.BlockSpec((d, n), lambda i, j: (0, 0)),
    ]
    args = [x, g.reshape(1, d), sc.reshape(b, 1, d), sh.reshape(b, 1, d), w]
    out_specs = [pl.BlockSpec((1, tm, m), lambda i, j: (i, j, 0)) for m in splits]
    out_shape = [jax.ShapeDtypeStruct((b, s, m), BF16) for m in splits]
    if head_norm:
        nh = splits[0] // LANES
        in_specs.append(pl.BlockSpec((1, LANES), lambda i, j: (0, 0)))
        args.append(head_gain.reshape(1, LANES))
        out_specs[0] = pl.BlockSpec((1, nh, tm, LANES), lambda i, j: (i, 0, j, 0))
        out_shape[0] = jax.ShapeDtypeStruct((b, nh, s, LANES), BF16)
    return pl.pallas_call(
        kern,
        grid=(b, s // tm),
        in_specs=in_specs,
        out_specs=out_specs,
        out_shape=out_shape,
        compiler_params=_cparams("arbitrary", "arbitrary"),
        name="norm_proj",
    )(*args)


def _ffn_kernel(x_ref, o_ref, wo_ref, mgate_ref, g_ref, sc_ref, sh_ref, gate_ref, wu_ref, cw_ref, cb_ref,
                wd_ref, y_ref, act_scr, carry_scr, *, fc):
    si = pl.program_id(1)
    tm = act_scr.shape[0]
    f = act_scr.shape[1]
    nf = f // fc

    @pl.when(si == 0)
    def _():
        carry_scr[...] = jnp.zeros_like(carry_scr)

    x1 = x_ref[0] + mgate_ref[0] * _dot(o_ref[0], wo_ref[...])
    hb = _modulated_norm(x1, g_ref[...], sc_ref[0], sh_ref[0]).astype(BF16)
    row = lax.broadcasted_iota(I32, (tm, 1), 0)

    def up(c):
        return (_dot(hb, wu_ref[:, c * fc:(c + 1) * fc]), _dot(hb, wu_ref[:, f + c * fc:f + (c + 1) * fc]))

    def conv(u, col0):
        prev = carry_scr[:, col0:col0 + fc]
        carry_scr[:, col0:col0 + fc] = u[tm - SUBLANES:, :]
        p1 = prev[SUBLANES - 1:SUBLANES, :]
        p2 = prev[SUBLANES - 2:SUBLANES - 1, :]
        u1 = jnp.where(row == 0, p1, pltpu.roll(u, 1, 0))
        u2 = jnp.where(row == 0, p2, jnp.where(row == 1, p1, pltpu.roll(u, 2, 0)))
        cw = cw_ref[:, col0:col0 + fc]
        return cb_ref[:, col0:col0 + fc] + u2 * cw[0:1, :] + u1 * cw[1:2, :] + u * cw[2:3, :]

    nxt = up(0)
    for c in range(nf):
        ug, uv = nxt
        if c + 1 < nf:
            nxt = up(c + 1)
        yg = conv(ug, c * fc)
        yv = conv(uv, f + c * fc)
        act_scr[:, c * fc:(c + 1) * fc] = (yg * (1.0 / (1.0 + jnp.exp(-yg))) * yv).astype(BF16)

    y_ref[0] = x1 + gate_ref[0] * _dot(act_scr[...], wd_ref[...])


def _mix_ffn(x, o, w_out, mgate, g, sc, sh, gate, w_up, conv_w, conv_b, w_down, tm=2 * MXU_DIM, fc=MXU_DIM):
    b, s, d = x.shape
    f = w_down.shape[0]
    k = o.shape[-1]
    kern = functools.partial(_ffn_kernel, fc=fc)
    vec = lambda a: a.reshape(b, 1, d)
    const = lambda i, j: (0, 0)
    return pl.pallas_call(
        kern,
        grid=(b, s // tm),
        in_specs=[
            pl.BlockSpec((1, tm, d), lambda i, j: (i, j, 0)),
            pl.BlockSpec((1, tm, k), lambda i, j: (i, j, 0)),
            pl.BlockSpec((k, d), const),
            pl.BlockSpec((1, 1, d), lambda i, j: (i, 0, 0)),
            pl.BlockSpec((1, d), const),
            pl.BlockSpec((1, 1, d), lambda i, j: (i, 0, 0)),
            pl.BlockSpec((1, 1, d), lambda i, j: (i, 0, 0)),
            pl.BlockSpec((1, 1, d), lambda i, j: (i, 0, 0)),
            pl.BlockSpec((d, 2 * f), const),
            pl.BlockSpec((CONV_W, 2 * f), const),
            pl.BlockSpec((1, 2 * f), const),
            pl.BlockSpec((f, d), const),
        ],
        out_specs=pl.BlockSpec((1, tm, d), lambda i, j: (i, j, 0)),
        out_shape=jax.ShapeDtypeStruct((b, s, d), F32),
        scratch_shapes=[
            pltpu.VMEM((tm, f), BF16),
            pltpu.VMEM((SUBLANES, 2 * f), F32),
        ],
        compiler_params=_cparams("arbitrary", "arbitrary"),
        name="ffn",
    )(x, o, w_out, vec(mgate), g.reshape(1, d), vec(sc), vec(sh), vec(gate), w_up, conv_w,
      conv_b.reshape(1, 2 * f), w_down)


def _sb_kernel(q_ref, k_ref, v_ref, tri_ref, o_ref, acc_scr, r_scr, *, tq):
    qi = pl.program_id(2)
    npair = acc_scr.shape[0]
    lane = lax.broadcasted_iota(I32, (1, LANES), 1)
    halves = (lane < SB_HEAD_DIM, lane >= SB_HEAD_DIM)
    tri2 = tri_ref[...]
    row = lax.broadcasted_iota(I32, (2 * tq, tq), 0)
    col = lax.broadcasted_iota(I32, (2 * tq, tq), 1)
    strict = col < jnp.where(row >= tq, row - tq, row)
    has_prev = qi >= 1
    jp = jnp.maximum(qi - 1, 0)

    def log_keep(z, masked):
        lk = -(jnp.maximum(z, 0.0) + jnp.log2(1.0 + jnp.exp2(-jnp.abs(z))))
        return jnp.where(strict, lk, 0.0) if masked else lk

    def suffix(lk):
        hi, lo = _split_bf16(lk)
        return _dot(jnp.concatenate([hi, lo], axis=1), tri2)

    def weights(z, incl, r, masked):
        a = jnp.exp2(z + incl + r)
        return jnp.where(strict, a, 0.0) if masked else a

    def pair_fns(pp):
        lanes = slice(pp * LANES, (pp + 1) * LANES)
        q2 = (q_ref[0, :, lanes].astype(F32) * (SB_HEAD_DIM ** -0.5 * LOG2E)).astype(BF16)
        zero = jnp.zeros_like(q2)
        qstack = jnp.concatenate([jnp.where(hm, q2, zero) for hm in halves], axis=0)

        def qk(jb):
            start = pl.multiple_of(jb * tq, tq)
            return _dot_nt(qstack, k_ref[0, pl.ds(start, tq), lanes])

        def av(a, jb):
            start = pl.multiple_of(jb * tq, tq)
            vb = v_ref[0, pl.ds(start, tq), lanes]
            vstack = jnp.concatenate([jnp.where(hm, vb, jnp.zeros_like(vb)) for hm in halves], axis=0)
            a2 = jnp.concatenate([a[:tq], a[tq:]], axis=1).astype(BF16)
            return _dot(a2, vstack)

        return qk, av

    fns = [pair_fns(pp) for pp in range(npair)]
    zs = {0: (fns[0][0](qi), fns[0][0](jp))}
    for pp in range(npair):
        qk, av = fns[pp]
        if pp + 1 < npair:
            zs[pp + 1] = (fns[pp + 1][0](qi), fns[pp + 1][0](jp))
        z_d, z_p = zs.pop(pp)
        incl_d = suffix(log_keep(z_d, True))
        incl_p = suffix(log_keep(z_p, False))
        r0 = jnp.zeros((2 * tq, 1), F32)
        acc = av(weights(z_d, incl_d, r0, True), qi)
        r1 = incl_d[:, 0:1]
        acc = acc + av(jnp.where(has_prev, weights(z_p, incl_p, r1, False), 0.0), jp)
        acc_scr[pp] = acc
        r_scr[pp] = r1 + jnp.where(has_prev, incl_p[:, 0:1], 0.0)

    def live(r):
        return jnp.max(r) > SB_EXIT

    for pp in range(npair):
        qk, av = pair_fns(pp)

        def step(state, qk=qk, av=av):
            i, acc, r, _ = state
            jb = qi - 2 - i
            z = qk(jb)
            incl = suffix(log_keep(z, False))
            acc = acc + av(weights(z, incl, r, False), jb)
            r = r + incl[:, 0:1]
            return i + 1, acc, r, live(r)

        r2 = r_scr[pp]
        _, acc, _, _ = lax.while_loop(lambda st: (st[0] < qi - 1) & st[3], step,
                                      (jnp.int32(0), acc_scr[pp], r2, live(r2)))
        o_ref[0, :, pp * LANES:(pp + 1) * LANES] = acc.astype(o_ref.dtype)


def _sb_attention(qkv, tq=MXU_DIM, pairs_per_step=8):
    b, s, n3 = qkv.shape
    n = n3 // 3
    gw = pairs_per_step * LANES
    ngrp = n // gw
    tri = (jnp.arange(tq)[:, None] >= jnp.arange(tq)[None, :]).astype(BF16)
    tri = jnp.concatenate([tri, tri], axis=0)
    kern = functools.partial(_sb_kernel, tq=tq)
    return pl.pallas_call(
        kern,
        grid=(b, ngrp, s // tq),
        in_specs=[
            pl.BlockSpec((1, tq, gw), lambda i, p, j: (i, j, p)),
            pl.BlockSpec((1, s, gw), lambda i, p, j: (i, 0, ngrp + p)),
            pl.BlockSpec((1, s, gw), lambda i, p, j: (i, 0, 2 * ngrp + p)),
            pl.BlockSpec((2 * tq, tq), lambda i, p, j: (0, 0)),
        ],
        out_specs=pl.BlockSpec((1, tq, gw), lambda i, p, j: (i, j, p)),
        out_shape=jax.ShapeDtypeStruct((b, s, n), BF16),
        scratch_shapes=[
            pltpu.VMEM((pairs_per_step, tq, LANES), F32),
            pltpu.VMEM((pairs_per_step, 2 * tq, 1), F32),
        ],
        compiler_params=_cparams("arbitrary", "arbitrary", "arbitrary"),
        name="sb_attn",
    )(qkv, qkv, qkv, tri)


def _t5_bucket_thresholds():
    max_exact = NUM_BUCKETS // 2
    n = np.arange(0, 4 * MAX_DISTANCE, dtype=np.int32)
    nf = np.maximum(n, 1).astype(np.float32)
    large = max_exact + (np.log(nf / np.float32(max_exact)) / np.float32(math.log(MAX_DISTANCE / max_exact))
                         * np.float32(NUM_BUCKETS - max_exact)).astype(np.int32)
    large = np.minimum(large, NUM_BUCKETS - 1)
    bucket = np.where(n < max_exact, n, large)
    assert np.all(np.diff(bucket) >= 0) and bucket[-1] == NUM_BUCKETS - 1
    return [int(np.argmax(bucket >= bb)) for bb in range(max_exact + 1, NUM_BUCKETS)]


def _sortable_key(v):
    v = jnp.where(v == 0.0, 0.0, v)
    bits = lax.bitcast_convert_type(v, I32)
    return bits ^ ((bits >> 31) & 0x7FFFFFFF)


def _tree_sum(parts):
    parts = list(parts)
    while len(parts) > 1:
        parts = [parts[i] + parts[i + 1] if i + 1 < len(parts) else parts[i] for i in range(0, len(parts), 2)]
    return parts[0]


def _dsa_kernel(rb_ref, q_ref, qidx_ref, lat_ref, kidx_ref, kg_ref, wuv_ref, tri_ref,
                o_ref, kn_scr, vt_scr, key_scr, top_scr, add_scr, qis_scr, olat_scr, bias_scr,
                lg_a, lg_b, p_a, p_b, acc_scr, *, tq, topk, thresholds):
    bi = pl.program_id(0)
    qi = pl.program_id(1)
    nkb = vt_scr.shape[0]
    krow = lax.broadcasted_iota(I32, (tq, tq), 0)
    qcol = lax.broadcasted_iota(I32, (tq, tq), 1)
    qcol1 = lax.broadcasted_iota(I32, (1, tq), 1)

    @pl.when((bi == 0) & (qi == 0))
    def _():
        add_scr[nkb + 2 * DSA_HEADS] = jnp.full((tq, tq), NEG, F32)
        max_exact = NUM_BUCKETS // 2
        for near in range(2):
            dist = qcol - krow + near * tq
            large = jnp.full((tq, tq), max_exact, I32)
            for th in thresholds:
                large = large + jnp.where(dist >= th, 1, 0)
            bucket = jnp.where(dist < max_exact, jnp.maximum(dist, 0), large)

            def head_tile(h, _, near=near, bucket=bucket):
                far = rb_ref[NUM_BUCKETS - 1, h]
                tile = jnp.zeros((tq, tq), F32)
                for bb in range(NUM_BUCKETS - 1):
                    tile = jnp.where(bucket == bb, (rb_ref[bb, h] - far) * LOG2E, tile)
                bias_scr[near, h] = tile
                return 0

            lax.fori_loop(0, DSA_HEADS, head_tile, 0)

    @pl.when(qi == 0)
    def _():
        ext_row = lax.broadcasted_iota(I32, (V_PAD, tq), 0)
        for jb in range(nkb):
            lat = lat_ref[0, jb * tq:(jb + 1) * tq, :].astype(F32)
            ms = jnp.mean(lat * lat, axis=-1, keepdims=True)
            kn_scr[jb * tq:(jb + 1) * tq, :] = (lat * lax.rsqrt(ms + RMS_EPS) * kg_ref[...]).astype(BF16)
            vt_scr[jb, :DSA_LATENT, :] = lat.T.astype(BF16)
            vt_scr[jb, DSA_LATENT:, :] = jnp.where(ext_row == 0, 1.0, 0.0).astype(BF16)

    lane = lax.broadcasted_iota(I32, (1, LANES), 1)
    halves = (lane < IDX_DIM, lane >= IDX_DIM)
    n_qi = IDX_HEADS * IDX_DIM
    wi = qidx_ref[0, :, n_qi + LANES:n_qi + 2 * LANES].astype(F32) * (IDX_HEADS ** -0.5)
    wit = wi.T
    for hh in range(IDX_HEADS):
        blk = qidx_ref[0, :, (hh // 2) * LANES:(hh // 2 + 1) * LANES]
        qis_scr[hh * tq:(hh + 1) * tq, :] = jnp.where(halves[hh % 2], blk, jnp.zeros_like(blk))

    def causal(jb):
        return krow + jb * tq <= qcol + qi * tq

    def score_matmul(t, sc_ref):
        start = pl.multiple_of(jnp.minimum(t, qi) * tq, tq)
        sc_ref[:, :IDX_HEADS * tq] = _dot_nt(kidx_ref[0, pl.ds(start, tq), :], qis_scr[...])

    def score_keys(t, sc_ref):
        jb = jnp.minimum(t, qi)
        isc = jnp.zeros((tq, tq), F32)
        for hh in range(IDX_HEADS):
            isc = isc + jnp.maximum(sc_ref[:, hh * tq:(hh + 1) * tq], 0.0) * wit[hh:hh + 1, :]
        key = jnp.where(causal(jb), _sortable_key(isc), INT_MIN)
        key_scr[jb] = key
        top_scr[jb] = (key >> (32 - TOP_BITS)).astype(F32).astype(BF16)

    score_matmul(0, lg_a)

    def score_pair(u, _):
        t0 = 2 * u
        score_matmul(t0 + 1, lg_b)
        score_keys(t0, lg_a)
        score_matmul(t0 + 2, lg_a)
        score_keys(t0 + 1, lg_b)
        return 0

    lax.fori_loop(0, (qi + 2) // 2, score_pair, 0)

    def count(pred):
        def body(jb, part):
            c = jnp.where(pred(key_scr[jb]), 1.0, 0.0)
            return part + _tree_sum(c[g * SUBLANES:(g + 1) * SUBLANES, :] for g in range(tq // SUBLANES))
        part = lax.fori_loop(0, qi + 1, body, jnp.zeros((SUBLANES, tq), F32))
        return jnp.sum(part, axis=0, keepdims=True)

    def count_top(pred):
        one = jnp.ones((), BF16)
        zero = jnp.zeros((), BF16)
        rows = 2 * SUBLANES

        def body(jb, part):
            c = jnp.where(pred(top_scr[jb]), one, zero)
            return part + _tree_sum(c[g * rows:(g + 1) * rows, :] for g in range(tq // rows)).astype(F32)
        part = lax.fori_loop(0, qi + 1, body, jnp.zeros((rows, tq), F32))
        return jnp.sum(part, axis=0, keepdims=True)

    kf = float(topk)
    half = 2 ** (TOP_BITS - 1)

    def bisect_top(i, v):
        cand = v | lax.shift_left(jnp.int32(1), jnp.int32(TOP_BITS - 1) - i)
        cand_d = (cand - half).astype(F32).astype(BF16)
        return jnp.where(count_top(lambda d: d >= cand_d) >= kf, cand, v)

    top = lax.fori_loop(0, TOP_BITS, bisect_top, jnp.zeros((1, tq), I32)) - half
    top_d = top.astype(F32).astype(BF16)
    thr = lax.shift_left(top, 32 - TOP_BITS)
    cnt = count_top(lambda d: d >= top_d)

    def bisect(i, state):
        thr, cnt = state
        cand = thr | lax.shift_left(jnp.int32(1), jnp.int32(31 - TOP_BITS) - i)
        c = count(lambda k: k >= cand)
        return jnp.where(c >= kf, cand, thr), jnp.where(c >= kf, c, cnt)

    thr, cnt = lax.fori_loop(0, 32 - TOP_BITS, bisect, (thr, cnt))
    short = qcol1 + qi * tq + 1 <= topk
    thr = jnp.where(short, INT_MIN, thr)
    straddle = jnp.max(jnp.where(short, 0.0, cnt - kf)) > 0.0

    @pl.when(jnp.logical_not(straddle))
    def _():
        def far_mask(jb, _):
            add_scr[jb] = jnp.where(key_scr[jb] >= thr, 0.0, NEG)
            return 0
        lax.fori_loop(0, qi, far_mask, 0)
        add_scr[qi] = jnp.where((key_scr[qi] >= thr) & causal(qi), 0.0, NEG)

    @pl.when(straddle)
    def _():
        need = kf - count(lambda k: k > thr)
        tri = tri_ref[...]

        def mask_block(jb, eq_before):
            key = key_scr[jb]
            eq = key == thr
            rank = _dot(tri, jnp.where(eq, 1.0, 0.0).astype(BF16)) + eq_before
            sel = ((key > thr) | (eq & (rank <= need))) & causal(jb)
            add_scr[jb] = jnp.where(sel, 0.0, NEG)
            return rank[tq - 1:tq, :]

        lax.fori_loop(0, qi + 1, mask_block, jnp.zeros((1, tq), F32))

    prev = jnp.maximum(qi - 1, 0)
    prev_pen = jnp.where(qi >= 1, 0.0, NEG)
    nblk = prev + 2

    def near_tiles(h, _):
        add_scr[nkb + h] = add_scr[prev] + bias_scr[1, h] + prev_pen
        add_scr[nkb + DSA_HEADS + h] = add_scr[qi] + bias_scr[0, h]
        return 0

    lax.fori_loop(0, DSA_HEADS, near_tiles, 0)

    def kv_block(t):
        t = jnp.maximum(t, 0)
        return jnp.minimum(jnp.where(t < prev, t, jnp.where(t == prev, prev, qi)), qi)

    def add_index(t, h):
        return jnp.where(t < prev, t,
                         jnp.where(t == prev, nkb + h,
                                   jnp.where(t == prev + 1, nkb + DSA_HEADS + h, nkb + 2 * DSA_HEADS)))

    gw = DSA_GROUP * tq

    def head_group(g, _):
        qg = q_ref[0, pl.ds(g * DSA_GROUP, DSA_GROUP)].reshape(gw, DSA_LATENT)

        def stage_qk(t, lk_ref):
            start = pl.multiple_of(kv_block(t) * tq, tq)
            lg = _dot_nt(kn_scr[pl.ds(start, tq), :], qg)
            tops = []
            for k in range(DSA_GROUP):
                lk = lg[:, k * tq:(k + 1) * tq] + add_scr[add_index(t, g * DSA_GROUP + k)]
                lk_ref[:, k * tq:(k + 1) * tq] = lk
                tops.append(jnp.max(lk, axis=0, keepdims=True))
            return tuple(tops)

        def stage_softmax(lk_ref, p_ref, stats, tops):
            new_stats, alphas = [], []
            for k, (m, top) in enumerate(zip(stats, tops)):
                m_new = jnp.maximum(m, top)
                alphas.append(jnp.exp2(m - m_new))
                p_ref[:, k * tq:(k + 1) * tq] = jnp.exp2(lk_ref[:, k * tq:(k + 1) * tq] - m_new).astype(BF16)
                new_stats.append(m_new)
            return tuple(new_stats), tuple(alphas)

        def stage_pv(t, p_ref, alphas):
            pv = _dot(vt_scr[kv_block(t)], p_ref[...])
            for k, alpha in enumerate(alphas):
                acc_scr[k] = alpha * acc_scr[k] + pv[:, k * tq:(k + 1) * tq]

        stats = (jnp.full((1, tq), 0.5 * NEG, F32),) * DSA_GROUP
        ones = (jnp.ones((1, tq), F32),) * DSA_GROUP
        acc_scr[...] = jnp.zeros_like(acc_scr)
        p_b[...] = jnp.zeros_like(p_b)
        tops_a = stage_qk(0, lg_a)

        def pair(u, carry):
            stats, alphas, tops_a = carry
            t0 = 2 * u
            tops_b = stage_qk(t0 + 1, lg_b)
            stage_pv(t0 - 1, p_b, alphas)
            stats, alphas = stage_softmax(lg_a, p_a, stats, tops_a)
            tops_a = stage_qk(t0 + 2, lg_a)
            stage_pv(t0, p_a, alphas)
            stats, alphas = stage_softmax(lg_b, p_b, stats, tops_b)
            return stats, alphas, tops_a

        npair = (nblk + 1) // 2
        stats, alphas, _ = lax.fori_loop(0, npair, pair, (stats, ones, tops_a))
        stage_pv(2 * npair - 1, p_b, alphas)
        for k in range(DSA_GROUP):
            acc = acc_scr[k]
            l = acc[DSA_LATENT:DSA_LATENT + 1, :]
            olat_scr[g * DSA_GROUP + k] = (acc[:DSA_LATENT, :] * (1.0 / l)).T.astype(BF16)
        return 0

    lax.fori_loop(0, DSA_HEADS // DSA_GROUP, head_group, 0)

    for pr in range(DSA_HEADS // 2):
        out = _dot(olat_scr[2 * pr], wuv_ref[2 * pr]) + _dot(olat_scr[2 * pr + 1], wuv_ref[2 * pr + 1])
        o_ref[0, :, pr * LANES:(pr + 1) * LANES] = out.astype(o_ref.dtype)


def _dsa_attention(qn, idx, lat, k_gain, wuv_ext, rel_bias, tq=MXU_DIM):
    b, _, s, _ = qn.shape
    assert DSA_GROUP >= IDX_HEADS
    topk = min(TOPK_MAX, s // 4)
    nkb = s // tq
    n_idx = idx.shape[-1]
    tri = (jnp.arange(tq)[:, None] >= jnp.arange(tq)[None, :]).astype(BF16)
    kern = functools.partial(_dsa_kernel, tq=tq, topk=topk, thresholds=tuple(_t5_bucket_thresholds()))
    return pl.pallas_call(
        kern,
        grid=(b, s // tq),
        in_specs=[
            pl.BlockSpec(memory_space=pltpu.SMEM),
            pl.BlockSpec((1, DSA_HEADS, tq, DSA_LATENT), lambda i, j: (i, 0, j, 0)),
            pl.BlockSpec((1, tq, n_idx), lambda i, j: (i, j, 0)),
            pl.BlockSpec((1, s, LANES), lambda i, j: (i, 0, 0)),
            pl.BlockSpec((1, s, LANES), lambda i, j: (i, 0, IDX_HEADS * IDX_DIM // LANES)),
            pl.BlockSpec((1, DSA_LATENT), lambda i, j: (0, 0)),
            pl.BlockSpec((DSA_HEADS, DSA_LATENT, LANES), lambda i, j: (0, 0, 0)),
            pl.BlockSpec((tq, tq), lambda i, j: (0, 0)),
        ],
        out_specs=pl.BlockSpec((1, tq, DSA_HEADS * DSA_V_DIM), lambda i, j: (i, j, 0)),
        out_shape=jax.ShapeDtypeStruct((b, s, DSA_HEADS * DSA_V_DIM), BF16),
        scratch_shapes=[
            pltpu.VMEM((s, DSA_LATENT), BF16),
            pltpu.VMEM((nkb, DSA_LATENT + V_PAD, tq), BF16),
            pltpu.VMEM((nkb, tq, tq), I32),
            pltpu.VMEM((nkb, tq, tq), BF16),
            pltpu.VMEM((nkb + 2 * DSA_HEADS + 1, tq, tq), F32),
            pltpu.VMEM((IDX_HEADS * tq, LANES), BF16),
            pltpu.VMEM((DSA_HEADS, tq, DSA_LATENT), BF16),
            pltpu.VMEM((2, DSA_HEADS, tq, tq), F32),
            pltpu.VMEM((tq, DSA_GROUP * tq), F32),
            pltpu.VMEM((tq, DSA_GROUP * tq), F32),
            pltpu.VMEM((tq, DSA_GROUP * tq), BF16),
            pltpu.VMEM((tq, DSA_GROUP * tq), BF16),
            pltpu.VMEM((DSA_GROUP, DSA_LATENT + V_PAD, tq), F32),
        ],
        compiler_params=_cparams("arbitrary", "arbitrary"),
        name="dsa_attn",
    )(rel_bias, qn, idx, lat, idx, k_gain.reshape(1, -1), wuv_ext, tri)


def _dsa_weights(w_in, w_uv):
    o1 = DSA_HEADS * DSA_LATENT
    o2 = o1 + DSA_LATENT
    o3 = o2 + IDX_HEADS * IDX_DIM
    o4 = o3 + IDX_DIM
    d = w_in.shape[0]
    w_ki = w_in[:, o3:o4]
    w_wi = w_in[:, o4:]
    pad = jnp.zeros((d, LANES - IDX_HEADS), w_in.dtype)
    w = jnp.concatenate([w_in[:, :o3], w_ki, w_ki, w_wi, pad], axis=1).astype(BF16)
    zeros = jnp.zeros_like(w_uv)
    even = jnp.concatenate([w_uv, zeros], axis=-1)
    odd = jnp.concatenate([zeros, w_uv], axis=-1)
    is_even = (jnp.arange(DSA_HEADS) % 2 == 0)[:, None, None]
    wuv_ext = jnp.where(is_even, even, odd).astype(BF16)
    splits = (o1, DSA_LATENT, IDX_HEADS * IDX_DIM + 2 * LANES)
    return w, wuv_ext, splits


def kernel(x, c, ada_w, ada_b, norm_mix, norm_ffn, sb_w_in, sb_w_out, dsa_w_in, dsa_q_norm,
           dsa_k_norm, dsa_w_uv, dsa_w_out, rel_bias, ffn_w_up, ffn_conv_w, ffn_conv_b, ffn_w_down):
    depth = ada_w.shape[0]
    d = x.shape[-1]
    mod = _modulation(c, ada_w, ada_b)
    for i in range(depth):
        sh1, sc1, g1, sh2, sc2, g2 = [mod[i, :, k * d:(k + 1) * d] for k in range(6)]
        j = i // 2
        if i % 2 == 0:
            w_in = sb_w_in[j].astype(BF16)
            (qkv,) = _norm_proj(x, norm_mix[i], sc1, sh1, w_in, (w_in.shape[1],))
            o = _sb_attention(qkv)
            w_out = sb_w_out[j]
        else:
            w_in, wuv_ext, splits = _dsa_weights(dsa_w_in[j], dsa_w_uv[j])
            q_gain = dsa_q_norm[j] * (DSA_LATENT ** -0.5 * LOG2E)
            qn, lat, idx = _norm_proj(x, norm_mix[i], sc1, sh1, w_in, splits, head_gain=q_gain)
            o = _dsa_attention(qn, idx, lat, dsa_k_norm[j], wuv_ext, rel_bias)
            w_out = dsa_w_out[j]
        x = _mix_ffn(x, o, w_out.astype(BF16), g1, norm_ffn[i], sc2, sh2, g2, ffn_w_up[i].astype(BF16),
                     ffn_conv_w[i], ffn_conv_b[i], ffn_w_down[i].astype(BF16))
    return x
```

```python
import functools
import math

import numpy as np
import jax
import jax.numpy as jnp
from jax import lax
from jax.experimental import pallas as pl
from jax.experimental.pallas import tpu as pltpu

F32 = jnp.float32
BF16 = jnp.bfloat16
I32 = jnp.int32

LANES = 128
SUBLANES = 8
MXU_DIM = 256
VMEM_LIMIT = 60 * 1024 * 1024

RMS_EPS = 1e-6
NEG = -1e30
LOG2E = math.log2(math.e)
SB_EXIT = -151.0
INT_MIN = -(2 ** 31)
TOP_BITS = 8

SB_HEADS = 16
SB_HEAD_DIM = 64
DSA_HEADS = 16
DSA_LATENT = 128
DSA_V_DIM = 64
V_PAD = 16
DSA_GROUP = 16
IDX_HEADS = 8
IDX_DIM = 64
TOPK_MAX = 256
NUM_BUCKETS = 32
MAX_DISTANCE = 128
CONV_W = 3


def _cparams(*sem):
    return pltpu.CompilerParams(dimension_semantics=sem, vmem_limit_bytes=VMEM_LIMIT)


def _dot(a, b):
    return jnp.dot(a, b, preferred_element_type=F32)


def _dot_nt(a, b):
    return lax.dot_general(a, b, (((1,), (1,)), ((), ())), preferred_element_type=F32)


def _split_bf16(v):
    hi = v.astype(BF16)
    lo = (v - hi.astype(F32)).astype(BF16)
    return hi, lo


def _mod_kernel(c_ref, w_ref, b_ref, o_ref):
    c = c_ref[...]
    cond = c * (1.0 / (1.0 + jnp.exp(-c)))
    ch, cl = _split_bf16(cond)
    wh, wl = _split_bf16(w_ref[0])
    o_ref[0] = _dot(ch, wh) + _dot(ch, wl) + _dot(cl, wh) + b_ref[0]


def _modulation(c, ada_w, ada_b):
    depth, d, n = ada_w.shape
    b = c.shape[0]
    tn = 1024
    return pl.pallas_call(
        _mod_kernel,
        grid=(depth, n // tn),
        in_specs=[
            pl.BlockSpec((b, d), lambda i, j: (0, 0)),
            pl.BlockSpec((1, d, tn), lambda i, j: (i, 0, j)),
            pl.BlockSpec((1, 1, tn), lambda i, j: (i, 0, j)),
        ],
        out_specs=pl.BlockSpec((1, b, tn), lambda i, j: (i, 0, j)),
        out_shape=jax.ShapeDtypeStruct((depth, b, n), F32),
        compiler_params=_cparams("arbitrary", "arbitrary"),
        name="mod",
    )(c, ada_w, ada_b.reshape(depth, 1, n))


def _modulated_norm(x, g, sc, sh):
    ms = jnp.mean(x * x, axis=-1, keepdims=True)
    return (x * lax.rsqrt(ms + RMS_EPS) * g) * (1.0 + sc) + sh


def _norm_proj_kernel(x_ref, g_ref, sc_ref, sh_ref, w_ref, *refs, splits, chunk, head_norm):
    gain_ref, o_refs = (refs[0], refs[1:]) if head_norm else (None, refs)
    hb = _modulated_norm(x_ref[0], g_ref[...], sc_ref[0], sh_ref[0]).astype(BF16)
    off = 0
    for idx, (o_ref, n) in enumerate(zip(o_refs, splits)):
        for c0 in range(0, n, chunk):
            c1 = min(n, c0 + chunk)
            y = _dot(hb, w_ref[:, off + c0:off + c1])
            if head_norm and idx == 0:
                for j in range((c1 - c0) // LANES):
                    yh = y[:, j * LANES:(j + 1) * LANES]
                    ms = jnp.mean(yh * yh, axis=-1, keepdims=True)
                    o_ref[0, c0 // LANES + j] = (yh * lax.rsqrt(ms + RMS_EPS) * gain_ref[...]).astype(o_ref.dtype)
            else:
                o_ref[0, :, c0:c1] = y.astype(o_ref.dtype)
        off += n


def _norm_proj(x, g, sc, sh, w, splits, head_gain=None, tm=2 * MXU_DIM):
    b, s, d = x.shape
    n = w.shape[1]
    assert sum(splits) == n
    head_norm = head_gain is not None
    kern = functools.partial(_norm_proj_kernel, splits=tuple(splits), chunk=512, head_norm=head_norm)
    in_specs = [
        pl.BlockSpec((1, tm, d), lambda i, j: (i, j, 0)),
        pl.BlockSpec((1, d), lambda i, j: (0, 0)),
        pl.BlockSpec((1, 1, d), lambda i, j: (i, 0, 0)),
        pl.BlockSpec((1, 1, d), lambda i, j: (i, 0, 0)),
        pl.BlockSpec((d, n), lambda i, j: (0, 0)),
    ]
    args = [x, g.reshape(1, d), sc.reshape(b, 1, d), sh.reshape(b, 1, d), w]
    out_specs = [pl.BlockSpec((1, tm, m), lambda i, j: (i, j, 0)) for m in splits]
    out_shape = [jax.ShapeDtypeStruct((b, s, m), BF16) for m in splits]
    if head_norm:
        nh = splits[0] // LANES
        in_specs.append(pl.BlockSpec((1, LANES), lambda i, j: (0, 0)))
        args.append(head_gain.reshape(1, LANES))
        out_specs[0] = pl.BlockSpec((1, nh, tm, LANES), lambda i, j: (i, 0, j, 0))
        out_shape[0] = jax.ShapeDtypeStruct((b, nh, s, LANES), BF16)
    return pl.pallas_call(
        kern,
        grid=(b, s // tm),
        in_specs=in_specs,
        out_specs=out_specs,
        out_shape=out_shape,
        compiler_params=_cparams("arbitrary", "arbitrary"),
        name="norm_proj",
    )(*args)


def _ffn_kernel(x_ref, o_ref, wo_ref, mgate_ref, g_ref, sc_ref, sh_ref, gate_ref, wu_ref, cw_ref, cb_ref,
                wd_ref, y_ref, act_scr, carry_scr, *, fc):
    si = pl.program_id(1)
    tm = act_scr.shape[0]
    f = act_scr.shape[1]
    nf = f // fc

    @pl.when(si == 0)
    def _():
        carry_scr[...] = jnp.zeros_like(carry_scr)

    x1 = x_ref[0] + mgate_ref[0] * _dot(o_ref[0], wo_ref[...])
    hb = _modulated_norm(x1, g_ref[...], sc_ref[0], sh_ref[0]).astype(BF16)
    row = lax.broadcasted_iota(I32, (tm, 1), 0)

    def up(c):
        return (_dot(hb, wu_ref[:, c * fc:(c + 1) * fc]), _dot(hb, wu_ref[:, f + c * fc:f + (c + 1) * fc]))

    def conv(u, col0):
        prev = carry_scr[:, col0:col0 + fc]
        carry_scr[:, col0:col0 + fc] = u[tm - SUBLANES:, :]
        p1 = prev[SUBLANES - 1:SUBLANES, :]
        p2 = prev[SUBLANES - 2:SUBLANES - 1, :]
        u1 = jnp.where(row == 0, p1, pltpu.roll(u, 1, 0))
        u2 = jnp.where(row == 0, p2, jnp.where(row == 1, p1, pltpu.roll(u, 2, 0)))
        cw = cw_ref[:, col0:col0 + fc]
        return cb_ref[:, col0:col0 + fc] + u2 * cw[0:1, :] + u1 * cw[1:2, :] + u * cw[2:3, :]

    nxt = up(0)
    for c in range(nf):
        ug, uv = nxt
        if c + 1 < nf:
            nxt = up(c + 1)
        yg = conv(ug, c * fc)
        yv = conv(uv, f + c * fc)
        act_scr[:, c * fc:(c + 1) * fc] = (yg * (1.0 / (1.0 + jnp.exp(-yg))) * yv).astype(BF16)

    y_ref[0] = x1 + gate_ref[0] * _dot(act_scr[...], wd_ref[...])


def _mix_ffn(x, o, w_out, mgate, g, sc, sh, gate, w_up, conv_w, conv_b, w_down, tm=2 * MXU_DIM, fc=MXU_DIM):
    b, s, d = x.shape
    f = w_down.shape[0]
    k = o.shape[-1]
    kern = functools.partial(_ffn_kernel, fc=fc)
    vec = lambda a: a.reshape(b, 1, d)
    const = lambda i, j: (0, 0)
    return pl.pallas_call(
        kern,
        grid=(b, s // tm),
        in_specs=[
            pl.BlockSpec((1, tm, d), lambda i, j: (i, j, 0)),
            pl.BlockSpec((1, tm, k), lambda i, j: (i, j, 0)),
            pl.BlockSpec((k, d), const),
            pl.BlockSpec((1, 1, d), lambda i, j: (i, 0, 0)),
            pl.BlockSpec((1, d), const),
            pl.BlockSpec((1, 1, d), lambda i, j: (i, 0, 0)),
            pl.BlockSpec((1, 1, d), lambda i, j: (i, 0, 0)),
            pl.BlockSpec((1, 1, d), lambda i, j: (i, 0, 0)),
            pl.BlockSpec((d, 2 * f), const),
            pl.BlockSpec((CONV_W, 2 * f), const),
            pl.BlockSpec((1, 2 * f), const),
            pl.BlockSpec((f, d), const),
        ],
        out_specs=pl.BlockSpec((1, tm, d), lambda i, j: (i, j, 0)),
        out_shape=jax.ShapeDtypeStruct((b, s, d), F32),
        scratch_shapes=[
            pltpu.VMEM((tm, f), BF16),
            pltpu.VMEM((SUBLANES, 2 * f), F32),
        ],
        compiler_params=_cparams("arbitrary", "arbitrary"),
        name="ffn",
    )(x, o, w_out, vec(mgate), g.reshape(1, d), vec(sc), vec(sh), vec(gate), w_up, conv_w,
      conv_b.reshape(1, 2 * f), w_down)


def _sb_kernel(q_ref, k_ref, v_ref, tri_ref, o_ref, acc_scr, r_scr, *, tq):
    qi = pl.program_id(2)
    npair = acc_scr.shape[0]
    lane = lax.broadcasted_iota(I32, (1, LANES), 1)
    halves = (lane < SB_HEAD_DIM, lane >= SB_HEAD_DIM)
    tri2 = tri_ref[...]
    row = lax.broadcasted_iota(I32, (2 * tq, tq), 0)
    col = lax.broadcasted_iota(I32, (2 * tq, tq), 1)
    strict = col < jnp.where(row >= tq, row - tq, row)
    has_prev = qi >= 1
    jp = jnp.maximum(qi - 1, 0)

    def log_keep(z, masked):
        lk = -(jnp.maximum(z, 0.0) + jnp.log2(1.0 + jnp.exp2(-jnp.abs(z))))
        return jnp.where(strict, lk, 0.0) if masked else lk

    def suffix(lk):
        hi, lo = _split_bf16(lk)
        return _dot(jnp.concatenate([hi, lo], axis=1), tri2)

    def weights(z, incl, r, masked):
        a = jnp.exp2(z + incl + r)
        return jnp.where(strict, a, 0.0) if masked else a

    def pair_fns(pp):
        lanes = slice(pp * LANES, (pp + 1) * LANES)
        q2 = (q_ref[0, :, lanes].astype(F32) * (SB_HEAD_DIM ** -0.5 * LOG2E)).astype(BF16)
        zero = jnp.zeros_like(q2)
        qstack = jnp.concatenate([jnp.where(hm, q2, zero) for hm in halves], axis=0)

        def qk(jb):
            start = pl.multiple_of(jb * tq, tq)
            return _dot_nt(qstack, k_ref[0, pl.ds(start, tq), lanes])

        def av(a, jb):
            start = pl.multiple_of(jb * tq, tq)
            vb = v_ref[0, pl.ds(start, tq), lanes]
            vstack = jnp.concatenate([jnp.where(hm, vb, jnp.zeros_like(vb)) for hm in halves], axis=0)
            a2 = jnp.concatenate([a[:tq], a[tq:]], axis=1).astype(BF16)
            return _dot(a2, vstack)

        return qk, av

    fns = [pair_fns(pp) for pp in range(npair)]
    zs = {0: (fns[0][0](qi), fns[0][0](jp))}
    for pp in range(npair):
        qk, av = fns[pp]
        if pp + 1 < npair:
            zs[pp + 1] = (fns[pp + 1][0](qi), fns[pp + 1][0](jp))
        z_d, z_p = zs.pop(pp)
        incl_d = suffix(log_keep(z_d, True))
        incl_p = suffix(log_keep(z_p, False))
        r0 = jnp.zeros((2 * tq, 1), F32)
        acc = av(weights(z_d, incl_d, r0, True), qi)
        r1 = incl_d[:, 0:1]
        acc = acc + av(jnp.where(has_prev, weights(z_p, incl_p, r1, False), 0.0), jp)
        acc_scr[pp] = acc
        r_scr[pp] = r1 + jnp.where(has_prev, incl_p[:, 0:1], 0.0)

    def live(r):
        return jnp.max(r) > SB_EXIT

    for pp in range(npair):
        qk, av = pair_fns(pp)

        def step(state, qk=qk, av=av):
            i, acc, r, _ = state
            jb = qi - 2 - i
            z = qk(jb)
            incl = suffix(log_keep(z, False))
            acc = acc + av(weights(z, incl, r, False), jb)
            r = r + incl[:, 0:1]
            return i + 1, acc, r, live(r)

        r2 = r_scr[pp]
        _, acc, _, _ = lax.while_loop(lambda st: (st[0] < qi - 1) & st[3], step,
                                      (jnp.int32(0), acc_scr[pp], r2, live(r2)))
        o_ref[0, :, pp * LANES:(pp + 1) * LANES] = acc.astype(o_ref.dtype)


def _sb_attention(qkv, tq=MXU_DIM, pairs_per_step=8):
    b, s, n3 = qkv.shape
    n = n3 // 3
    gw = pairs_per_step * LANES
    ngrp = n // gw
    tri = (jnp.arange(tq)[:, None] >= jnp.arange(tq)[None, :]).astype(BF16)
    tri = jnp.concatenate([tri, tri], axis=0)
    kern = functools.partial(_sb_kernel, tq=tq)
    return pl.pallas_call(
        kern,
        grid=(b, ngrp, s // tq),
        in_specs=[
            pl.BlockSpec((1, tq, gw), lambda i, p, j: (i, j, p)),
            pl.BlockSpec((1, s, gw), lambda i, p, j: (i, 0, ngrp + p)),
            pl.BlockSpec((1, s, gw), lambda i, p, j: (i, 0, 2 * ngrp + p)),
            pl.BlockSpec((2 * tq, tq), lambda i, p, j: (0, 0)),
        ],
        out_specs=pl.BlockSpec((1, tq, gw), lambda i, p, j: (i, j, p)),
        out_shape=jax.ShapeDtypeStruct((b, s, n), BF16),
        scratch_shapes=[
            pltpu.VMEM((pairs_per_step, tq, LANES), F32),
            pltpu.VMEM((pairs_per_step, 2 * tq, 1), F32),
        ],
        compiler_params=_cparams("arbitrary", "arbitrary", "arbitrary"),
        name="sb_attn",
    )(qkv, qkv, qkv, tri)


def _t5_bucket_thresholds():
    max_exact = NUM_BUCKETS // 2
    n = np.arange(0, 4 * MAX_DISTANCE, dtype=np.int32)
    nf = np.maximum(n, 1).astype(np.float32)
    large = max_exact + (np.log(nf / np.float32(max_exact)) / np.float32(math.log(MAX_DISTANCE / max_exact))
                         * np.float32(NUM_BUCKETS - max_exact)).astype(np.int32)
    large = np.minimum(large, NUM_BUCKETS - 1)
    bucket = np.where(n < max_exact, n, large)
    assert np.all(np.diff(bucket) >= 0) and bucket[-1] == NUM_BUCKETS - 1
    return [int(np.argmax(bucket >= bb)) for bb in range(max_exact + 1, NUM_BUCKETS)]


def _sortable_key(v):
    v = jnp.where(v == 0.0, 0.0, v)
    bits = lax.bitcast_convert_type(v, I32)
    return bits ^ ((bits >> 31) & 0x7FFFFFFF)


def _tree_sum(parts):
    parts = list(parts)
    while len(parts) > 1:
        parts = [parts[i] + parts[i + 1] if i + 1 < len(parts) else parts[i] for i in range(0, len(parts), 2)]
    return parts[0]


def _dsa_kernel(rb_ref, q_ref, qidx_ref, lat_ref, kidx_ref, kg_ref, wuv_ref, tri_ref,
                o_ref, kn_scr, vt_scr, key_scr, top_scr, add_scr, qis_scr, olat_scr, bias_scr,
                lg_a, lg_b, p_a, p_b, acc_scr, *, tq, topk, thresholds):
    bi = pl.program_id(0)
    qi = pl.program_id(1)
    nkb = vt_scr.shape[0]
    krow = lax.broadcasted_iota(I32, (tq, tq), 0)
    qcol = lax.broadcasted_iota(I32, (tq, tq), 1)
    qcol1 = lax.broadcasted_iota(I32, (1, tq), 1)

    @pl.when((bi == 0) & (qi == 0))
    def _():
        add_scr[nkb + 2 * DSA_HEADS] = jnp.full((tq, tq), NEG, F32)
        max_exact = NUM_BUCKETS // 2
        for near in range(2):
            dist = qcol - krow + near * tq
            large = jnp.full((tq, tq), max_exact, I32)
            for th in thresholds:
                large = large + jnp.where(dist >= th, 1, 0)
            bucket = jnp.where(dist < max_exact, jnp.maximum(dist, 0), large)

            def head_tile(h, _, near=near, bucket=bucket):
                far = rb_ref[NUM_BUCKETS - 1, h]
                tile = jnp.zeros((tq, tq), F32)
                for bb in range(NUM_BUCKETS - 1):
                    tile = jnp.where(bucket == bb, (rb_ref[bb, h] - far) * LOG2E, tile)
                bias_scr[near, h] = tile
                return 0

            lax.fori_loop(0, DSA_HEADS, head_tile, 0)

    @pl.when(qi == 0)
    def _():
        ext_row = lax.broadcasted_iota(I32, (V_PAD, tq), 0)
        for jb in range(nkb):
            lat = lat_ref[0, jb * tq:(jb + 1) * tq, :].astype(F32)
            ms = jnp.mean(lat * lat, axis=-1, keepdims=True)
            kn_scr[jb * tq:(jb + 1) * tq, :] = (lat * lax.rsqrt(ms + RMS_EPS) * kg_ref[...]).astype(BF16)
            vt_scr[jb, :DSA_LATENT, :] = lat.T.astype(BF16)
            vt_scr[jb, DSA_LATENT:, :] = jnp.where(ext_row == 0, 1.0, 0.0).astype(BF16)

    lane = lax.broadcasted_iota(I32, (1, LANES), 1)
    halves = (lane < IDX_DIM, lane >= IDX_DIM)
    n_qi = IDX_HEADS * IDX_DIM
    wi = qidx_ref[0, :, n_qi + LANES:n_qi + 2 * LANES].astype(F32) * (IDX_HEADS ** -0.5)
    wit = wi.T
    for hh in range(IDX_HEADS):
        blk = qidx_ref[0, :, (hh // 2) * LANES:(hh // 2 + 1) * LANES]
        qis_scr[hh * tq:(hh + 1) * tq, :] = jnp.where(halves[hh % 2], blk, jnp.zeros_like(blk))

    def causal(jb):
        return krow + jb * tq <= qcol + qi * tq

    def score_matmul(t, sc_ref):
        start = pl.multiple_of(jnp.minimum(t, qi) * tq, tq)
        sc_ref[:, :IDX_HEADS * tq] = _dot_nt(kidx_ref[0, pl.ds(start, tq), :], qis_scr[...])

    def score_keys(t, sc_ref):
        jb = jnp.minimum(t, qi)
        isc = jnp.zeros((tq, tq), F32)
        for hh in range(IDX_HEADS):
            isc = isc + jnp.maximum(sc_ref[:, hh * tq:(hh + 1) * tq], 0.0) * wit[hh:hh + 1, :]
        key = jnp.where(causal(jb), _sortable_key(isc), INT_MIN)
        key_scr[jb] = key
        top_scr[jb] = (key >> (32 - TOP_BITS)).astype(F32).astype(BF16)

    score_matmul(0, lg_a)

    def score_pair(u, _):
        t0 = 2 * u
        score_matmul(t0 + 1, lg_b)
        score_keys(t0, lg_a)
        score_matmul(t0 + 2, lg_a)
        score_keys(t0 + 1, lg_b)
        return 0

    lax.fori_loop(0, (qi + 2) // 2, score_pair, 0)

    def count(pred):
        def body(jb, part):
            c = jnp.where(pred(key_scr[jb]), 1.0, 0.0)
            return part + _tree_sum(c[g * SUBLANES:(g + 1) * SUBLANES, :] for g in range(tq // SUBLANES))
        part = lax.fori_loop(0, qi + 1, body, jnp.zeros((SUBLANES, tq), F32))
        return jnp.sum(part, axis=0, keepdims=True)

    def count_top(pred):
        one = jnp.ones((), BF16)
        zero = jnp.zeros((), BF16)
        rows = 2 * SUBLANES

        def body(jb, part):
            c = jnp.where(pred(top_scr[jb]), one, zero)
            return part + _tree_sum(c[g * rows:(g + 1) * rows, :] for g in range(tq // rows)).astype(F32)
        part = lax.fori_loop(0, qi + 1, body, jnp.zeros((rows, tq), F32))
        return jnp.sum(part, axis=0, keepdims=True)

    kf = float(topk)
    half = 2 ** (TOP_BITS - 1)

    def bisect_top(i, v):
        cand = v | lax.shift_left(jnp.int32(1), jnp.int32(TOP_BITS - 1) - i)
        cand_d = (cand - half).astype(F32).astype(BF16)
        return jnp.where(count_top(lambda d: d >= cand_d) >= kf, cand, v)

    top = lax.fori_loop(0, TOP_BITS, bisect_top, jnp.zeros((1, tq), I32)) - half
    top_d = top.astype(F32).astype(BF16)
    thr = lax.shift_left(top, 32 - TOP_BITS)
    cnt = count_top(lambda d: d >= top_d)

    def bisect(i, state):
        thr, cnt = state
        cand = thr | lax.shift_left(jnp.int32(1), jnp.int32(31 - TOP_BITS) - i)
        c = count(lambda k: k >= cand)
        return jnp.where(c >= kf, cand, thr), jnp.where(c >= kf, c, cnt)

    thr, cnt = lax.fori_loop(0, 32 - TOP_BITS, bisect, (thr, cnt))
    short = qcol1 + qi * tq + 1 <= topk
    thr = jnp.where(short, INT_MIN, thr)
    straddle = jnp.max(jnp.where(short, 0.0, cnt - kf)) > 0.0

    @pl.when(jnp.logical_not(straddle))
    def _():
        def far_mask(jb, _):
            add_scr[jb] = jnp.where(key_scr[jb] >= thr, 0.0, NEG)
            return 0
        lax.fori_loop(0, qi, far_mask, 0)
        add_scr[qi] = jnp.where((key_scr[qi] >= thr) & causal(qi), 0.0, NEG)

    @pl.when(straddle)
    def _():
        need = kf - count(lambda k: k > thr)
        tri = tri_ref[...]

        def mask_block(jb, eq_before):
            key = key_scr[jb]
            eq = key == thr
            rank = _dot(tri, jnp.where(eq, 1.0, 0.0).astype(BF16)) + eq_before
            sel = ((key > thr) | (eq & (rank <= need))) & causal(jb)
            add_scr[jb] = jnp.where(sel, 0.0, NEG)
            return rank[tq - 1:tq, :]

        lax.fori_loop(0, qi + 1, mask_block, jnp.zeros((1, tq), F32))

    prev = jnp.maximum(qi - 1, 0)
    prev_pen = jnp.where(qi >= 1, 0.0, NEG)
    nblk = prev + 2

    def near_tiles(hq, _):
        m_prev = add_scr[prev] + prev_pen
        m_diag = add_scr[qi]
        for k in range(4):
            h = 4 * hq + k
            add_scr[nkb + h] = m_prev + bias_scr[1, h]
            add_scr[nkb + DSA_HEADS + h] = m_diag + bias_scr[0, h]
        return 0

    lax.fori_loop(0, DSA_HEADS // 4, near_tiles, 0)

    def kv_block(t):
        t = jnp.maximum(t, 0)
        return jnp.minimum(jnp.where(t < prev, t, jnp.where(t == prev, prev, qi)), qi)

    def add_index(t, h):
        return jnp.where(t < prev, t,
                         jnp.where(t == prev, nkb + h,
                                   jnp.where(t == prev + 1, nkb + DSA_HEADS + h, nkb + 2 * DSA_HEADS)))

    gw = DSA_GROUP * tq

    def head_group(g, _):
        qg = q_ref[0, pl.ds(g * DSA_GROUP, DSA_GROUP)].reshape(gw, DSA_LATENT)

        def stage_qk(t, lk_ref):
            start = pl.multiple_of(kv_block(t) * tq, tq)
            lg = _dot_nt(kn_scr[pl.ds(start, tq), :], qg)
            tops = []
            for k in range(DSA_GROUP):
                lk = lg[:, k * tq:(k + 1) * tq] + add_scr[add_index(t, g * DSA_GROUP + k)]
                lk_ref[:, k * tq:(k + 1) * tq] = lk
                tops.append(jnp.max(lk, axis=0, keepdims=True))
            return tuple(tops)

        def stage_softmax(lk_ref, p_ref, stats, tops):
            new_stats, alphas = [], []
            for k, (m, top) in enumerate(zip(stats, tops)):
                m_new = jnp.maximum(m, top)
                alphas.append(jnp.exp2(m - m_new))
                p_ref[:, k * tq:(k + 1) * tq] = jnp.exp2(lk_ref[:, k * tq:(k + 1) * tq] - m_new).astype(BF16)
                new_stats.append(m_new)
            return tuple(new_stats), tuple(alphas)

        def stage_pv(t, p_ref, alphas):
            pv = _dot(vt_scr[kv_block(t)], p_ref[...])
            for k, alpha in enumerate(alphas):
                acc_scr[k] = alpha * acc_scr[k] + pv[:, k * tq:(k + 1) * tq]

        stats = (jnp.full((1, tq), 0.5 * NEG, F32),) * DSA_GROUP
        ones = (jnp.ones((1, tq), F32),) * DSA_GROUP
        acc_scr[...] = jnp.zeros_like(acc_scr)
        p_b[...] = jnp.zeros_like(p_b)
        tops_a = stage_qk(0, lg_a)

        def pair(u, carry):
            stats, alphas, tops_a = carry
            t0 = 2 * u
            tops_b = stage_qk(t0 + 1, lg_b)
            stage_pv(t0 - 1, p_b, alphas)
            stats, alphas = stage_softmax(lg_a, p_a, stats, tops_a)
            tops_a = stage_qk(t0 + 2, lg_a)
            stage_pv(t0, p_a, alphas)
            stats, alphas = stage_softmax(lg_b, p_b, stats, tops_b)
            return stats, alphas, tops_a

        npair = (nblk + 1) // 2
        stats, alphas, _ = lax.fori_loop(0, npair, pair, (stats, ones, tops_a))
        stage_pv(2 * npair - 1, p_b, alphas)
        for k in range(DSA_GROUP):
            acc = acc_scr[k]
            l = acc[DSA_LATENT:DSA_LATENT + 1, :]
            olat_scr[g * DSA_GROUP + k] = (acc[:DSA_LATENT, :] * (1.0 / l)).T.astype(BF16)
        return 0

    lax.fori_loop(0, DSA_HEADS // DSA_GROUP, head_group, 0)

    for pr in range(DSA_HEADS // 2):
        out = _dot(olat_scr[2 * pr], wuv_ref[2 * pr]) + _dot(olat_scr[2 * pr + 1], wuv_ref[2 * pr + 1])
        o_ref[0, :, pr * LANES:(pr + 1) * LANES] = out.astype(o_ref.dtype)


def _dsa_attention(qn, idx, lat, k_gain, wuv_ext, rel_bias, tq=MXU_DIM):
    b, _, s, _ = qn.shape
    assert DSA_GROUP >= IDX_HEADS
    topk = min(TOPK_MAX, s // 4)
    nkb = s // tq
    n_idx = idx.shape[-1]
    tri = (jnp.arange(tq)[:, None] >= jnp.arange(tq)[None, :]).astype(BF16)
    kern = functools.partial(_dsa_kernel, tq=tq, topk=topk, thresholds=tuple(_t5_bucket_thresholds()))
    return pl.pallas_call(
        kern,
        grid=(b, s // tq),
        in_specs=[
            pl.BlockSpec(memory_space=pltpu.SMEM),
            pl.BlockSpec((1, DSA_HEADS, tq, DSA_LATENT), lambda i, j: (i, 0, j, 0)),
            pl.BlockSpec((1, tq, n_idx), lambda i, j: (i, j, 0)),
            pl.BlockSpec((1, s, LANES), lambda i, j: (i, 0, 0)),
            pl.BlockSpec((1, s, LANES), lambda i, j: (i, 0, IDX_HEADS * IDX_DIM // LANES)),
            pl.BlockSpec((1, DSA_LATENT), lambda i, j: (0, 0)),
            pl.BlockSpec((DSA_HEADS, DSA_LATENT, LANES), lambda i, j: (0, 0, 0)),
            pl.BlockSpec((tq, tq), lambda i, j: (0, 0)),
        ],
        out_specs=pl.BlockSpec((1, tq, DSA_HEADS * DSA_V_DIM), lambda i, j: (i, j, 0)),
        out_shape=jax.ShapeDtypeStruct((b, s, DSA_HEADS * DSA_V_DIM), BF16),
        scratch_shapes=[
            pltpu.VMEM((s, DSA_LATENT), BF16),
            pltpu.VMEM((nkb, DSA_LATENT + V_PAD, tq), BF16),
            pltpu.VMEM((nkb, tq, tq), I32),
            pltpu.VMEM((nkb, tq, tq), BF16),
            pltpu.VMEM((nkb + 2 * DSA_HEADS + 1, tq, tq), F32),
            pltpu.VMEM((IDX_HEADS * tq, LANES), BF16),
            pltpu.VMEM((DSA_HEADS, tq, DSA_LATENT), BF16),
            pltpu.VMEM((2, DSA_HEADS, tq, tq), F32),
            pltpu.VMEM((tq, DSA_GROUP * tq), F32),
            pltpu.VMEM((tq, DSA_GROUP * tq), F32),
            pltpu.VMEM((tq, DSA_GROUP * tq), BF16),
            pltpu.VMEM((tq, DSA_GROUP * tq), BF16),
            pltpu.VMEM((DSA_GROUP, DSA_LATENT + V_PAD, tq), F32),
        ],
        compiler_params=_cparams("arbitrary", "arbitrary"),
        name="dsa_attn",
    )(rel_bias, qn, idx, lat, idx, k_gain.reshape(1, -1), wuv_ext, tri)


def _dsa_weights(w_in, w_uv):
    o1 = DSA_HEADS * DSA_LATENT
    o2 = o1 + DSA_LATENT
    o3 = o2 + IDX_HEADS * IDX_DIM
    o4 = o3 + IDX_DIM
    d = w_in.shape[0]
    w_ki = w_in[:, o3:o4]
    w_wi = w_in[:, o4:]
    pad = jnp.zeros((d, LANES - IDX_HEADS), w_in.dtype)
    w = jnp.concatenate([w_in[:, :o3], w_ki, w_ki, w_wi, pad], axis=1).astype(BF16)
    zeros = jnp.zeros_like(w_uv)
    even = jnp.concatenate([w_uv, zeros], axis=-1)
    odd = jnp.concatenate([zeros, w_uv], axis=-1)
    is_even = (jnp.arange(DSA_HEADS) % 2 == 0)[:, None, None]
    wuv_ext = jnp.where(is_even, even, odd).astype(BF16)
    splits = (o1, DSA_LATENT, IDX_HEADS * IDX_DIM + 2 * LANES)
    return w, wuv_ext, splits


def kernel(x, c, ada_w, ada_b, norm_mix, norm_ffn, sb_w_in, sb_w_out, dsa_w_in, dsa_q_norm,
           dsa_k_norm, dsa_w_uv, dsa_w_out, rel_bias, ffn_w_up, ffn_conv_w, ffn_conv_b, ffn_w_down):
    depth = ada_w.shape[0]
    d = x.shape[-1]
    mod = _modulation(c, ada_w, ada_b)
    for i in range(depth):
        sh1, sc1, g1, sh2, sc2, g2 = [mod[i, :, k * d:(k + 1) * d] for k in range(6)]
        j = i // 2
        if i % 2 == 0:
            w_in = sb_w_in[j].astype(BF16)
            (qkv,) = _norm_proj(x, norm_mix[i], sc1, sh1, w_in, (w_in.shape[1],))
            o = _sb_attention(qkv)
            w_out = sb_w_out[j]
        else:
            w_in, wuv_ext, splits = _dsa_weights(dsa_w_in[j], dsa_w_uv[j])
            q_gain = dsa_q_norm[j] * (DSA_LATENT ** -0.5 * LOG2E)
            qn, lat, idx = _norm_proj(x, norm_mix[i], sc1, sh1, w_in, splits, head_gain=q_gain)
            o = _dsa_attention(qn, idx, lat, dsa_k_norm[j], wuv_ext, rel_bias)
            w_out = dsa_w_out[j]
        x = _mix_ffn(x, o, w_out.astype(BF16), g1, norm_ffn[i], sc2, sh2, g2, ffn_w_up[i].astype(BF16),
                     ffn_conv_w[i], ffn_conv_b[i], ffn_w_down[i].astype(BF16))
    return x
```

```python
import functools
import math

import numpy as np
import jax
import jax.numpy as jnp
from jax import lax
from jax.experimental import pallas as pl
from jax.experimental.pallas import tpu as pltpu

F32 = jnp.float32
BF16 = jnp.bfloat16
I32 = jnp.int32

LANES = 128
SUBLANES = 8
MXU_DIM = 256
VMEM_LIMIT = 60 * 1024 * 1024

RMS_EPS = 1e-6
NEG = -1e30
LOG2E = math.log2(math.e)
SB_EXIT = -151.0
INT_MIN = -(2 ** 31)
TOP_BITS = 8

SB_HEADS = 16
SB_HEAD_DIM = 64
DSA_HEADS = 16
DSA_LATENT = 128
DSA_V_DIM = 64
V_PAD = 16
DSA_GROUP = 16
IDX_HEADS = 8
IDX_DIM = 64
TOPK_MAX = 256
NUM_BUCKETS = 32
MAX_DISTANCE = 128
CONV_W = 3


def _cparams(*sem):
    return pltpu.CompilerParams(dimension_semantics=sem, vmem_limit_bytes=VMEM_LIMIT)


def _dot(a, b):
    return jnp.dot(a, b, preferred_element_type=F32)


def _dot_nt(a, b):
    return lax.dot_general(a, b, (((1,), (1,)), ((), ())), preferred_element_type=F32)


def _split_bf16(v):
    hi = v.astype(BF16)
    lo = (v - hi.astype(F32)).astype(BF16)
    return hi, lo


def _mod_kernel(c_ref, w_ref, b_ref, o_ref):
    c = c_ref[...]
    cond = c * (1.0 / (1.0 + jnp.exp(-c)))
    ch, cl = _split_bf16(cond)
    wh, wl = _split_bf16(w_ref[0])
    o_ref[0] = _dot(ch, wh) + _dot(ch, wl) + _dot(cl, wh) + b_ref[0]


def _modulation(c, ada_w, ada_b):
    depth, d, n = ada_w.shape
    b = c.shape[0]
    tn = 1024
    return pl.pallas_call(
        _mod_kernel,
        grid=(depth, n // tn),
        in_specs=[
            pl.BlockSpec((b, d), lambda i, j: (0, 0)),
            pl.BlockSpec((1, d, tn), lambda i, j: (i, 0, j)),
            pl.BlockSpec((1, 1, tn), lambda i, j: (i, 0, j)),
        ],
        out_specs=pl.BlockSpec((1, b, tn), lambda i, j: (i, 0, j)),
        out_shape=jax.ShapeDtypeStruct((depth, b, n), F32),
        compiler_params=_cparams("arbitrary", "arbitrary"),
        name="mod",
    )(c, ada_w, ada_b.reshape(depth, 1, n))


def _modulated_norm(x, g, sc, sh):
    ms = jnp.mean(x * x, axis=-1, keepdims=True)
    return (x * lax.rsqrt(ms + RMS_EPS) * g) * (1.0 + sc) + sh


def _norm_proj_kernel(x_ref, g_ref, sc_ref, sh_ref, w_ref, *refs, splits, chunk, head_norm):
    gain_ref, o_refs = (refs[0], refs[1:]) if head_norm else (None, refs)
    hb = _modulated_norm(x_ref[0], g_ref[...], sc_ref[0], sh_ref[0]).astype(BF16)
    off = 0
    for idx, (o_ref, n) in enumerate(zip(o_refs, splits)):
        for c0 in range(0, n, chunk):
            c1 = min(n, c0 + chunk)
            y = _dot(hb, w_ref[:, off + c0:off + c1])
            if head_norm and idx == 0:
                for j in range((c1 - c0) // LANES):
                    yh = y[:, j * LANES:(j + 1) * LANES]
                    ms = jnp.mean(yh * yh, axis=-1, keepdims=True)
                    o_ref[0, c0 // LANES + j] = (yh * lax.rsqrt(ms + RMS_EPS) * gain_ref[...]).astype(o_ref.dtype)
            else:
                o_ref[0, :, c0:c1] = y.astype(o_ref.dtype)
        off += n


def _norm_proj(x, g, sc, sh, w, splits, head_gain=None, tm=2 * MXU_DIM):
    b, s, d = x.shape
    n = w.shape[1]
    assert sum(splits) == n
    head_norm = head_gain is not None
    kern = functools.partial(_norm_proj_kernel, splits=tuple(splits), chunk=512, head_norm=head_norm)
    in_specs = [
        pl.BlockSpec((1, tm, d), lambda i, j: (i, j, 0)),
        pl.BlockSpec((1, d), lambda i, j: (0, 0)),
        pl.BlockSpec((1, 1, d), lambda i, j: (i, 0, 0)),
        pl.BlockSpec((1, 1, d), lambda i, j: (i, 0, 0)),
        pl.BlockSpec((d, n), lambda i, j: (0, 0)),
    ]
    args = [x, g.reshape(1, d), sc.reshape(b, 1, d), sh.reshape(b, 1, d), w]
    out_specs = [pl.BlockSpec((1, tm, m), lambda i, j: (i, j, 0)) for m in splits]
    out_shape = [jax.ShapeDtypeStruct((b, s, m), BF16) for m in splits]
    if head_norm:
        nh = splits[0] // LANES
        in_specs.append(pl.BlockSpec((1, LANES), lambda i, j: (0, 0)))
        args.append(head_gain.reshape(1, LANES))
        out_specs[0] = pl.BlockSpec((1, nh, tm, LANES), lambda i, j: (i, 0, j, 0))
        out_shape[0] = jax.ShapeDtypeStruct((b, nh, s, LANES), BF16)
    return pl.pallas_call(
        kern,
        grid=(b, s // tm),
        in_specs=in_specs,
        out_specs=out_specs,
        out_shape=out_shape,
        compiler_params=_cparams("arbitrary", "arbitrary"),
        name="norm_proj",
    )(*args)


def _ffn_kernel(x_ref, o_ref, wo_ref, mgate_ref, g_ref, sc_ref, sh_ref, gate_ref, wu_ref, cw_ref, cb_ref,
                wd_ref, y_ref, act_scr, carry_scr, *, fc):
    si = pl.program_id(1)
    tm = act_scr.shape[0]
    f = act_scr.shape[1]
    nf = f // fc

    @pl.when(si == 0)
    def _():
        carry_scr[...] = jnp.zeros_like(carry_scr)

    x1 = x_ref[0] + mgate_ref[0] * _dot(o_ref[0], wo_ref[...])
    hb = _modulated_norm(x1, g_ref[...], sc_ref[0], sh_ref[0]).astype(BF16)
    row = lax.broadcasted_iota(I32, (tm, 1), 0)

    def up(c):
        return (_dot(hb, wu_ref[:, c * fc:(c + 1) * fc]), _dot(hb, wu_ref[:, f + c * fc:f + (c + 1) * fc]))

    def conv(u, col0):
        prev = carry_scr[:, col0:col0 + fc]
        carry_scr[:, col0:col0 + fc] = u[tm - SUBLANES:, :]
        p1 = prev[SUBLANES - 1:SUBLANES, :]
        p2 = prev[SUBLANES - 2:SUBLANES - 1, :]
        u1 = jnp.where(row == 0, p1, pltpu.roll(u, 1, 0))
        u2 = jnp.where(row == 0, p2, jnp.where(row == 1, p1, pltpu.roll(u, 2, 0)))
        cw = cw_ref[:, col0:col0 + fc]
        return cb_ref[:, col0:col0 + fc] + u2 * cw[0:1, :] + u1 * cw[1:2, :] + u * cw[2:3, :]

    nxt = up(0)
    for c in range(nf):
        ug, uv = nxt
        if c + 1 < nf:
            nxt = up(c + 1)
        yg = conv(ug, c * fc)
        yv = conv(uv, f + c * fc)
        act_scr[:, c * fc:(c + 1) * fc] = (yg * (1.0 / (1.0 + jnp.exp(-yg))) * yv).astype(BF16)

    y_ref[0] = x1 + gate_ref[0] * _dot(act_scr[...], wd_ref[...])


def _mix_ffn(x, o, w_out, mgate, g, sc, sh, gate, w_up, conv_w, conv_b, w_down, tm=4 * MXU_DIM, fc=MXU_DIM):
    b, s, d = x.shape
    f = w_down.shape[0]
    k = o.shape[-1]
    kern = functools.partial(_ffn_kernel, fc=fc)
    vec = lambda a: a.reshape(b, 1, d)
    const = lambda i, j: (0, 0)
    return pl.pallas_call(
        kern,
        grid=(b, s // tm),
        in_specs=[
            pl.BlockSpec((1, tm, d), lambda i, j: (i, j, 0)),
            pl.BlockSpec((1, tm, k), lambda i, j: (i, j, 0)),
            pl.BlockSpec((k, d), const),
            pl.BlockSpec((1, 1, d), lambda i, j: (i, 0, 0)),
            pl.BlockSpec((1, d), const),
            pl.BlockSpec((1, 1, d), lambda i, j: (i, 0, 0)),
            pl.BlockSpec((1, 1, d), lambda i, j: (i, 0, 0)),
            pl.BlockSpec((1, 1, d), lambda i, j: (i, 0, 0)),
            pl.BlockSpec((d, 2 * f), const),
            pl.BlockSpec((CONV_W, 2 * f), const),
            pl.BlockSpec((1, 2 * f), const),
            pl.BlockSpec((f, d), const),
        ],
        out_specs=pl.BlockSpec((1, tm, d), lambda i, j: (i, j, 0)),
        out_shape=jax.ShapeDtypeStruct((b, s, d), F32),
        scratch_shapes=[
            pltpu.VMEM((tm, f), BF16),
            pltpu.VMEM((SUBLANES, 2 * f), F32),
        ],
        compiler_params=_cparams("arbitrary", "arbitrary"),
        name="ffn",
    )(x, o, w_out, vec(mgate), g.reshape(1, d), vec(sc), vec(sh), vec(gate), w_up, conv_w,
      conv_b.reshape(1, 2 * f), w_down)


def _sb_kernel(q_ref, k_ref, v_ref, tri_ref, o_ref, acc_scr, r_scr, *, tq):
    qi = pl.program_id(2)
    npair = acc_scr.shape[0]
    lane = lax.broadcasted_iota(I32, (1, LANES), 1)
    halves = (lane < SB_HEAD_DIM, lane >= SB_HEAD_DIM)
    tri2 = tri_ref[...]
    row = lax.broadcasted_iota(I32, (2 * tq, tq), 0)
    col = lax.broadcasted_iota(I32, (2 * tq, tq), 1)
    strict = col < jnp.where(row >= tq, row - tq, row)
    has_prev = qi >= 1
    jp = jnp.maximum(qi - 1, 0)

    def log_keep(z, masked):
        lk = -(jnp.maximum(z, 0.0) + jnp.log2(1.0 + jnp.exp2(-jnp.abs(z))))
        return jnp.where(strict, lk, 0.0) if masked else lk

    def suffix(lk):
        hi, lo = _split_bf16(lk)
        return _dot(jnp.concatenate([hi, lo], axis=1), tri2)

    def weights(z, incl, r, masked):
        a = jnp.exp2(z + incl + r)
        return jnp.where(strict, a, 0.0) if masked else a

    def pair_fns(pp):
        lanes = slice(pp * LANES, (pp + 1) * LANES)
        q2 = (q_ref[0, :, lanes].astype(F32) * (SB_HEAD_DIM ** -0.5 * LOG2E)).astype(BF16)
        zero = jnp.zeros_like(q2)
        qstack = jnp.concatenate([jnp.where(hm, q2, zero) for hm in halves], axis=0)

        def qk(jb):
            start = pl.multiple_of(jb * tq, tq)
            return _dot_nt(qstack, k_ref[0, pl.ds(start, tq), lanes])

        def av(a, jb):
            start = pl.multiple_of(jb * tq, tq)
            vb = v_ref[0, pl.ds(start, tq), lanes]
            vstack = jnp.concatenate([jnp.where(hm, vb, jnp.zeros_like(vb)) for hm in halves], axis=0)
            a2 = jnp.concatenate([a[:tq], a[tq:]], axis=1).astype(BF16)
            return _dot(a2, vstack)

        return qk, av

    fns = [pair_fns(pp) for pp in range(npair)]
    zs = {0: (fns[0][0](qi), fns[0][0](jp))}
    for pp in range(npair):
        qk, av = fns[pp]
        if pp + 1 < npair:
            zs[pp + 1] = (fns[pp + 1][0](qi), fns[pp + 1][0](jp))
        z_d, z_p = zs.pop(pp)
        incl_d = suffix(log_keep(z_d, True))
        incl_p = suffix(log_keep(z_p, False))
        r0 = jnp.zeros((2 * tq, 1), F32)
        acc = av(weights(z_d, incl_d, r0, True), qi)
        r1 = incl_d[:, 0:1]
        acc = acc + av(jnp.where(has_prev, weights(z_p, incl_p, r1, False), 0.0), jp)
        acc_scr[pp] = acc
        r_scr[pp] = r1 + jnp.where(has_prev, incl_p[:, 0:1], 0.0)

    def live(r):
        return jnp.max(r) > SB_EXIT

    for pp in range(npair):
        qk, av = pair_fns(pp)

        def step(state, qk=qk, av=av):
            i, acc, r, _ = state
            jb = qi - 2 - i
            z = qk(jb)
            incl = suffix(log_keep(z, False))
            acc = acc + av(weights(z, incl, r, False), jb)
            r = r + incl[:, 0:1]
            return i + 1, acc, r, live(r)

        r2 = r_scr[pp]
        _, acc, _, _ = lax.while_loop(lambda st: (st[0] < qi - 1) & st[3], step,
                                      (jnp.int32(0), acc_scr[pp], r2, live(r2)))
        o_ref[0, :, pp * LANES:(pp + 1) * LANES] = acc.astype(o_ref.dtype)


def _sb_attention(qkv, tq=MXU_DIM, pairs_per_step=8):
    b, s, n3 = qkv.shape
    n = n3 // 3
    gw = pairs_per_step * LANES
    ngrp = n // gw
    tri = (jnp.arange(tq)[:, None] >= jnp.arange(tq)[None, :]).astype(BF16)
    tri = jnp.concatenate([tri, tri], axis=0)
    kern = functools.partial(_sb_kernel, tq=tq)
    return pl.pallas_call(
        kern,
        grid=(b, ngrp, s // tq),
        in_specs=[
            pl.BlockSpec((1, tq, gw), lambda i, p, j: (i, j, p)),
            pl.BlockSpec((1, s, gw), lambda i, p, j: (i, 0, ngrp + p)),
            pl.BlockSpec((1, s, gw), lambda i, p, j: (i, 0, 2 * ngrp + p)),
            pl.BlockSpec((2 * tq, tq), lambda i, p, j: (0, 0)),
        ],
        out_specs=pl.BlockSpec((1, tq, gw), lambda i, p, j: (i, j, p)),
        out_shape=jax.ShapeDtypeStruct((b, s, n), BF16),
        scratch_shapes=[
            pltpu.VMEM((pairs_per_step, tq, LANES), F32),
            pltpu.VMEM((pairs_per_step, 2 * tq, 1), F32),
        ],
        compiler_params=_cparams("arbitrary", "arbitrary", "arbitrary"),
        name="sb_attn",
    )(qkv, qkv, qkv, tri)


def _t5_bucket_thresholds():
    max_exact = NUM_BUCKETS // 2
    n = np.arange(0, 4 * MAX_DISTANCE, dtype=np.int32)
    nf = np.maximum(n, 1).astype(np.float32)
    large = max_exact + (np.log(nf / np.float32(max_exact)) / np.float32(math.log(MAX_DISTANCE / max_exact))
                         * np.float32(NUM_BUCKETS - max_exact)).astype(np.int32)
    large = np.minimum(large, NUM_BUCKETS - 1)
    bucket = np.where(n < max_exact, n, large)
    assert np.all(np.diff(bucket) >= 0) and bucket[-1] == NUM_BUCKETS - 1
    return [int(np.argmax(bucket >= bb)) for bb in range(max_exact + 1, NUM_BUCKETS)]


def _sortable_key(v):
    v = jnp.where(v == 0.0, 0.0, v)
    bits = lax.bitcast_convert_type(v, I32)
    return bits ^ ((bits >> 31) & 0x7FFFFFFF)


def _tree_sum(parts):
    parts = list(parts)
    while len(parts) > 1:
        parts = [parts[i] + parts[i + 1] if i + 1 < len(parts) else parts[i] for i in range(0, len(parts), 2)]
    return parts[0]


def _dsa_kernel(rb_ref, q_ref, qidx_ref, lat_ref, kidx_ref, kg_ref, wuv_ref, tri_ref,
                o_ref, kn_scr, vt_scr, key_scr, top_scr, add_scr, qis_scr, olat_scr, bias_scr,
                lg_a, lg_b, p_a, p_b, acc_scr, *, tq, topk, thresholds):
    bi = pl.program_id(0)
    qi = pl.program_id(1)
    nkb = vt_scr.shape[0]
    krow = lax.broadcasted_iota(I32, (tq, tq), 0)
    qcol = lax.broadcasted_iota(I32, (tq, tq), 1)
    qcol1 = lax.broadcasted_iota(I32, (1, tq), 1)

    @pl.when((bi == 0) & (qi == 0))
    def _():
        add_scr[nkb + 2 * DSA_HEADS] = jnp.full((tq, tq), NEG, F32)
        max_exact = NUM_BUCKETS // 2
        for near in range(2):
            dist = qcol - krow + near * tq
            large = jnp.full((tq, tq), max_exact, I32)
            for th in thresholds:
                large = large + jnp.where(dist >= th, 1, 0)
            bucket = jnp.where(dist < max_exact, jnp.maximum(dist, 0), large)

            def head_tile(h, _, near=near, bucket=bucket):
                far = rb_ref[NUM_BUCKETS - 1, h]
                tile = jnp.zeros((tq, tq), F32)
                for bb in range(NUM_BUCKETS - 1):
                    tile = jnp.where(bucket == bb, (rb_ref[bb, h] - far) * LOG2E, tile)
                bias_scr[near, h] = tile
                return 0

            lax.fori_loop(0, DSA_HEADS, head_tile, 0)

    @pl.when(qi == 0)
    def _():
        ext_row = lax.broadcasted_iota(I32, (V_PAD, tq), 0)
        for jb in range(nkb):
            lat = lat_ref[0, jb * tq:(jb + 1) * tq, :].astype(F32)
            ms = jnp.mean(lat * lat, axis=-1, keepdims=True)
            kn_scr[jb * tq:(jb + 1) * tq, :] = (lat * lax.rsqrt(ms + RMS_EPS) * kg_ref[...]).astype(BF16)
            vt_scr[jb, :DSA_LATENT, :] = lat.T.astype(BF16)
            vt_scr[jb, DSA_LATENT:, :] = jnp.where(ext_row == 0, 1.0, 0.0).astype(BF16)

    lane = lax.broadcasted_iota(I32, (1, LANES), 1)
    halves = (lane < IDX_DIM, lane >= IDX_DIM)
    n_qi = IDX_HEADS * IDX_DIM
    wi = qidx_ref[0, :, n_qi + LANES:n_qi + 2 * LANES].astype(F32) * (IDX_HEADS ** -0.5)
    wit = wi.T
    for hh in range(IDX_HEADS):
        blk = qidx_ref[0, :, (hh // 2) * LANES:(hh // 2 + 1) * LANES]
        qis_scr[hh * tq:(hh + 1) * tq, :] = jnp.where(halves[hh % 2], blk, jnp.zeros_like(blk))

    def causal(jb):
        return krow + jb * tq <= qcol + qi * tq

    def score_matmul(t, sc_ref):
        start = pl.multiple_of(jnp.minimum(t, qi) * tq, tq)
        sc_ref[:, :IDX_HEADS * tq] = _dot_nt(kidx_ref[0, pl.ds(start, tq), :], qis_scr[...])

    def score_keys(t, sc_ref):
        jb = jnp.minimum(t, qi)
        isc = jnp.zeros((tq, tq), F32)
        for hh in range(IDX_HEADS):
            isc = isc + jnp.maximum(sc_ref[:, hh * tq:(hh + 1) * tq], 0.0) * wit[hh:hh + 1, :]
        key = jnp.where(causal(jb), _sortable_key(isc), INT_MIN)
        key_scr[jb] = key
        top_scr[jb] = (key >> (32 - TOP_BITS)).astype(F32).astype(BF16)

    score_matmul(0, lg_a)

    def score_pair(u, _):
        t0 = 2 * u
        score_matmul(t0 + 1, lg_b)
        score_keys(t0, lg_a)
        score_matmul(t0 + 2, lg_a)
        score_keys(t0 + 1, lg_b)
        return 0

    lax.fori_loop(0, (qi + 2) // 2, score_pair, 0)

    def count(pred):
        def body(jb, part):
            c = jnp.where(pred(key_scr[jb]), 1.0, 0.0)
            return part + _tree_sum(c[g * SUBLANES:(g + 1) * SUBLANES, :] for g in range(tq // SUBLANES))
        part = lax.fori_loop(0, qi + 1, body, jnp.zeros((SUBLANES, tq), F32))
        return jnp.sum(part, axis=0, keepdims=True)

    def count_top(pred):
        one = jnp.ones((), BF16)
        zero = jnp.zeros((), BF16)
        rows = 2 * SUBLANES

        def body(jb, part):
            c = jnp.where(pred(top_scr[jb]), one, zero)
            return part + _tree_sum(c[g * rows:(g + 1) * rows, :] for g in range(tq // rows)).astype(F32)
        part = lax.fori_loop(0, qi + 1, body, jnp.zeros((rows, tq), F32))
        return jnp.sum(part, axis=0, keepdims=True)

    kf = float(topk)
    half = 2 ** (TOP_BITS - 1)

    def bisect_top(i, v):
        cand = v | lax.shift_left(jnp.int32(1), jnp.int32(TOP_BITS - 1) - i)
        cand_d = (cand - half).astype(F32).astype(BF16)
        return jnp.where(count_top(lambda d: d >= cand_d) >= kf, cand, v)

    top = lax.fori_loop(0, TOP_BITS, bisect_top, jnp.zeros((1, tq), I32)) - half
    top_d = top.astype(F32).astype(BF16)
    thr = lax.shift_left(top, 32 - TOP_BITS)
    cnt = count_top(lambda d: d >= top_d)

    def bisect(i, state):
        thr, cnt = state
        cand = thr | lax.shift_left(jnp.int32(1), jnp.int32(31 - TOP_BITS) - i)
        c = count(lambda k: k >= cand)
        return jnp.where(c >= kf, cand, thr), jnp.where(c >= kf, c, cnt)

    thr, cnt = lax.fori_loop(0, 32 - TOP_BITS, bisect, (thr, cnt))
    short = qcol1 + qi * tq + 1 <= topk
    thr = jnp.where(short, INT_MIN, thr)
    straddle = jnp.max(jnp.where(short, 0.0, cnt - kf)) > 0.0

    @pl.when(jnp.logical_not(straddle))
    def _():
        def far_mask(jb, _):
            add_scr[jb] = jnp.where(key_scr[jb] >= thr, 0.0, NEG)
            return 0
        lax.fori_loop(0, qi, far_mask, 0)
        add_scr[qi] = jnp.where((key_scr[qi] >= thr) & causal(qi), 0.0, NEG)

    @pl.when(straddle)
    def _():
        need = kf - count(lambda k: k > thr)
        tri = tri_ref[...]

        def mask_block(jb, eq_before):
            key = key_scr[jb]
            eq = key == thr
            rank = _dot(tri, jnp.where(eq, 1.0, 0.0).astype(BF16)) + eq_before
            sel = ((key > thr) | (eq & (rank <= need))) & causal(jb)
            add_scr[jb] = jnp.where(sel, 0.0, NEG)
            return rank[tq - 1:tq, :]

        lax.fori_loop(0, qi + 1, mask_block, jnp.zeros((1, tq), F32))

    prev = jnp.maximum(qi - 1, 0)
    prev_pen = jnp.where(qi >= 1, 0.0, NEG)
    nblk = prev + 2

    def near_tiles(hq, _):
        m_prev = add_scr[prev] + prev_pen
        m_diag = add_scr[qi]
        for k in range(4):
            h = 4 * hq + k
            add_scr[nkb + h] = m_prev + bias_scr[1, h]
            add_scr[nkb + DSA_HEADS + h] = m_diag + bias_scr[0, h]
        return 0

    lax.fori_loop(0, DSA_HEADS // 4, near_tiles, 0)

    def kv_block(t):
        t = jnp.maximum(t, 0)
        return jnp.minimum(jnp.where(t < prev, t, jnp.where(t == prev, prev, qi)), qi)

    def add_index(t, h):
        return jnp.where(t < prev, t,
                         jnp.where(t == prev, nkb + h,
                                   jnp.where(t == prev + 1, nkb + DSA_HEADS + h, nkb + 2 * DSA_HEADS)))

    gw = DSA_GROUP * tq

    def head_group(g, _):
        qg = q_ref[0, pl.ds(g * DSA_GROUP, DSA_GROUP)].reshape(gw, DSA_LATENT)

        def stage_qk(t, lk_ref):
            start = pl.multiple_of(kv_block(t) * tq, tq)
            lg = _dot_nt(kn_scr[pl.ds(start, tq), :], qg)
            tops = []
            for k in range(DSA_GROUP):
                lk = lg[:, k * tq:(k + 1) * tq] + add_scr[add_index(t, g * DSA_GROUP + k)]
                lk_ref[:, k * tq:(k + 1) * tq] = lk
                tops.append(jnp.max(lk, axis=0, keepdims=True))
            return tuple(tops)

        def stage_softmax(lk_ref, p_ref, stats, tops):
            new_stats, alphas = [], []
            for k, (m, top) in enumerate(zip(stats, tops)):
                m_new = jnp.maximum(m, top)
                alphas.append(jnp.exp2(m - m_new))
                p_ref[:, k * tq:(k + 1) * tq] = jnp.exp2(lk_ref[:, k * tq:(k + 1) * tq] - m_new).astype(BF16)
                new_stats.append(m_new)
            return tuple(new_stats), tuple(alphas)

        def stage_pv(t, p_ref, alphas):
            pv = _dot(vt_scr[kv_block(t)], p_ref[...])
            for k, alpha in enumerate(alphas):
                acc_scr[k] = alpha * acc_scr[k] + pv[:, k * tq:(k + 1) * tq]

        stats = (jnp.full((1, tq), 0.5 * NEG, F32),) * DSA_GROUP
        ones = (jnp.ones((1, tq), F32),) * DSA_GROUP
        acc_scr[...] = jnp.zeros_like(acc_scr)
        p_b[...] = jnp.zeros_like(p_b)
        tops_a = stage_qk(0, lg_a)

        def pair(u, carry):
            stats, alphas, tops_a = carry
            t0 = 2 * u
            tops_b = stage_qk(t0 + 1, lg_b)
            stage_pv(t0 - 1, p_b, alphas)
            stats, alphas = stage_softmax(lg_a, p_a, stats, tops_a)
            tops_a = stage_qk(t0 + 2, lg_a)
            stage_pv(t0, p_a, alphas)
            stats, alphas = stage_softmax(lg_b, p_b, stats, tops_b)
            return stats, alphas, tops_a

        npair = (nblk + 1) // 2
        stats, alphas, _ = lax.fori_loop(0, npair, pair, (stats, ones, tops_a))
        stage_pv(2 * npair - 1, p_b, alphas)
        for k in range(DSA_GROUP):
            acc = acc_scr[k]
            l = acc[DSA_LATENT:DSA_LATENT + 1, :]
            olat_scr[g * DSA_GROUP + k] = (acc[:DSA_LATENT, :] * (1.0 / l)).T.astype(BF16)
        return 0

    lax.fori_loop(0, DSA_HEADS // DSA_GROUP, head_group, 0)

    for pr in range(DSA_HEADS // 2):
        out = _dot(olat_scr[2 * pr], wuv_ref[2 * pr]) + _dot(olat_scr[2 * pr + 1], wuv_ref[2 * pr + 1])
        o_ref[0, :, pr * LANES:(pr + 1) * LANES] = out.astype(o_ref.dtype)


def _dsa_attention(qn, idx, lat, k_gain, wuv_ext, rel_bias, tq=MXU_DIM):
    b, _, s, _ = qn.shape
    assert DSA_GROUP >= IDX_HEADS
    topk = min(TOPK_MAX, s // 4)
    nkb = s // tq
    n_idx = idx.shape[-1]
    tri = (jnp.arange(tq)[:, None] >= jnp.arange(tq)[None, :]).astype(BF16)
    kern = functools.partial(_dsa_kernel, tq=tq, topk=topk, thresholds=tuple(_t5_bucket_thresholds()))
    return pl.pallas_call(
        kern,
        grid=(b, s // tq),
        in_specs=[
            pl.BlockSpec(memory_space=pltpu.SMEM),
            pl.BlockSpec((1, DSA_HEADS, tq, DSA_LATENT), lambda i, j: (i, 0, j, 0)),
            pl.BlockSpec((1, tq, n_idx), lambda i, j: (i, j, 0)),
            pl.BlockSpec((1, s, LANES), lambda i, j: (i, 0, 0)),
            pl.BlockSpec((1, s, LANES), lambda i, j: (i, 0, IDX_HEADS * IDX_DIM // LANES)),
            pl.BlockSpec((1, DSA_LATENT), lambda i, j: (0, 0)),
            pl.BlockSpec((DSA_HEADS, DSA_LATENT, LANES), lambda i, j: (0, 0, 0)),
            pl.BlockSpec((tq, tq), lambda i, j: (0, 0)),
        ],
        out_specs=pl.BlockSpec((1, tq, DSA_HEADS * DSA_V_DIM), lambda i, j: (i, j, 0)),
        out_shape=jax.ShapeDtypeStruct((b, s, DSA_HEADS * DSA_V_DIM), BF16),
        scratch_shapes=[
            pltpu.VMEM((s, DSA_LATENT), BF16),
            pltpu.VMEM((nkb, DSA_LATENT + V_PAD, tq), BF16),
            pltpu.VMEM((nkb, tq, tq), I32),
            pltpu.VMEM((nkb, tq, tq), BF16),
            pltpu.VMEM((nkb + 2 * DSA_HEADS + 1, tq, tq), F32),
            pltpu.VMEM((IDX_HEADS * tq, LANES), BF16),
            pltpu.VMEM((DSA_HEADS, tq, DSA_LATENT), BF16),
            pltpu.VMEM((2, DSA_HEADS, tq, tq), F32),
            pltpu.VMEM((tq, DSA_GROUP * tq), F32),
            pltpu.VMEM((tq, DSA_GROUP * tq), F32),
            pltpu.VMEM((tq, DSA_GROUP * tq), BF16),
            pltpu.VMEM((tq, DSA_GROUP * tq), BF16),
            pltpu.VMEM((DSA_GROUP, DSA_LATENT + V_PAD, tq), F32),
        ],
        compiler_params=_cparams("arbitrary", "arbitrary"),
        name="dsa_attn",
    )(rel_bias, qn, idx, lat, idx, k_gain.reshape(1, -1), wuv_ext, tri)


def _dsa_weights(w_in, w_uv):
    o1 = DSA_HEADS * DSA_LATENT
    o2 = o1 + DSA_LATENT
    o3 = o2 + IDX_HEADS * IDX_DIM
    o4 = o3 + IDX_DIM
    d = w_in.shape[0]
    w_ki = w_in[:, o3:o4]
    w_wi = w_in[:, o4:]
    pad = jnp.zeros((d, LANES - IDX_HEADS), w_in.dtype)
    w = jnp.concatenate([w_in[:, :o3], w_ki, w_ki, w_wi, pad], axis=1).astype(BF16)
    zeros = jnp.zeros_like(w_uv)
    even = jnp.concatenate([w_uv, zeros], axis=-1)
    odd = jnp.concatenate([zeros, w_uv], axis=-1)
    is_even = (jnp.arange(DSA_HEADS) % 2 == 0)[:, None, None]
    wuv_ext = jnp.where(is_even, even, odd).astype(BF16)
    splits = (o1, DSA_LATENT, IDX_HEADS * IDX_DIM + 2 * LANES)
    return w, wuv_ext, splits


def kernel(x, c, ada_w, ada_b, norm_mix, norm_ffn, sb_w_in, sb_w_out, dsa_w_in, dsa_q_norm,
           dsa_k_norm, dsa_w_uv, dsa_w_out, rel_bias, ffn_w_up, ffn_conv_w, ffn_conv_b, ffn_w_down):
    depth = ada_w.shape[0]
    d = x.shape[-1]
    mod = _modulation(c, ada_w, ada_b)
    for i in range(depth):
        sh1, sc1, g1, sh2, sc2, g2 = [mod[i, :, k * d:(k + 1) * d] for k in range(6)]
        j = i // 2
        if i % 2 == 0:
            w_in = sb_w_in[j].astype(BF16)
            (qkv,) = _norm_proj(x, norm_mix[i], sc1, sh1, w_in, (w_in.shape[1],))
            o = _sb_attention(qkv)
            w_out = sb_w_out[j]
        else:
            w_in, wuv_ext, splits = _dsa_weights(dsa_w_in[j], dsa_w_uv[j])
            q_gain = dsa_q_norm[j] * (DSA_LATENT ** -0.5 * LOG2E)
            qn, lat, idx = _norm_proj(x, norm_mix[i], sc1, sh1, w_in, splits, head_gain=q_gain)
            o = _dsa_attention(qn, idx, lat, dsa_k_norm[j], wuv_ext, rel_bias)
            w_out = dsa_w_out[j]
        x = _mix_ffn(x, o, w_out.astype(BF16), g1, norm_ffn[i], sc2, sh2, g2, ffn_w_up[i].astype(BF16),
                     ffn_conv_w[i], ffn_conv_b[i], ffn_w_down[i].astype(BF16))
    return x
```
